```python
import math
import jax, jax.numpy as jnp
from jax import lax
import numpy as np

D_MODEL = 2048
BATCH = 8
SEQ = 8192
DEPTH = 4

N_MIXERS = 2
CONV_WIDTH = 31
SSM_GROUP = 16
SSM_GROUPS = D_MODEL // SSM_GROUP
SSM_STATE = 64
D_FF = 4 * D_MODEL
N_CONV_LAYERS = (DEPTH + 1) // 2
N_SSM_LAYERS = DEPTH // 2
EPS = 1e-6
DT_MIN = 1e-3
DT_MAX = 1e-1

kernel_name = "interleaved_conformer_conv_s5_hybrid"


def rms_norm(x, g):
    xf = x.astype(jnp.float32)
    y = xf * lax.rsqrt(jnp.mean(xf * xf, axis=-1, keepdims=True) + EPS)
    return (y * g.astype(jnp.float32)).astype(x.dtype)


def layer_norm(x, g, b):
    xf = x.astype(jnp.float32)
    mu = jnp.mean(xf, axis=-1, keepdims=True)
    xc = xf - mu
    var = jnp.mean(xc * xc, axis=-1, keepdims=True)
    y = xc * lax.rsqrt(var + EPS) * g.astype(jnp.float32) + b.astype(jnp.float32)
    return y.astype(x.dtype)


def conv_module(h, w_in, b_in, dw, dw_b, ln_g, ln_b, w_out, b_out):
    u = h @ w_in + b_in
    a, gate = jnp.split(u, 2, axis=-1)
    v = a * jax.nn.sigmoid(gate)
    v = lax.conv_general_dilated(
        v, dw[:, None, :], window_strides=(1,),
        padding=[(CONV_WIDTH - 1, 0)],
        dimension_numbers=("NWC", "WIO", "NWC"),
        feature_group_count=D_MODEL) + dw_b
    v = layer_norm(v, ln_g, ln_b)
    v = jax.nn.silu(v)
    return v @ w_out + b_out


def ssm_module(h, lam_re, lam_im, log_dt, b_re, b_im, c_re, c_im, d_skip, w_glu):
    f32 = jnp.float32
    bsz, L, _ = h.shape
    u = h.astype(f32).reshape(bsz, L, SSM_GROUPS, SSM_GROUP)
    dt = jnp.exp(log_dt.astype(f32))[:, None]
    lr = lam_re.astype(f32)
    li = lam_im.astype(f32)
    mag = jnp.exp(lr * dt)
    ab_re = mag * jnp.cos(li * dt)
    ab_im = mag * jnp.sin(li * dt)
    nr = ab_re - 1.0
    ni = ab_im
    den = lr * lr + li * li
    k_re = ((nr * lr + ni * li) / den)[..., None]
    k_im = ((ni * lr - nr * li) / den)[..., None]
    br = b_re.astype(f32)
    bi = b_im.astype(f32)
    bb_re = k_re * br - k_im * bi
    bb_im = k_re * bi + k_im * br
    bu_re = jnp.einsum("blgc,gpc->blgp", u, bb_re)
    bu_im = jnp.einsum("blgc,gpc->blgp", u, bb_im)
    a_re = jnp.broadcast_to(ab_re, bu_re.shape)
    a_im = jnp.broadcast_to(ab_im, bu_im.shape)

    def combine(e1, e2):
        a1r, a1i, b1r, b1i = e1
        a2r, a2i, b2r, b2i = e2
        return (a2r * a1r - a2i * a1i,
                a2r * a1i + a2i * a1r,
                a2r * b1r - a2i * b1i + b2r,
                a2r * b1i + a2i * b1r + b2i)

    _, _, s_re, s_im = lax.associative_scan(combine, (a_re, a_im, bu_re, bu_im), axis=1)
    y = (jnp.einsum("blgp,gcp->blgc", s_re, c_re.astype(f32))
         - jnp.einsum("blgp,gcp->blgc", s_im, c_im.astype(f32)))
    y = y.reshape(bsz, L, D_MODEL) + d_skip.astype(f32) * h.astype(f32)
    y = jax.nn.gelu(y).astype(h.dtype)
    z = y @ w_glu
    val, gate = jnp.split(z, 2, axis=-1)
    return val * jax.nn.sigmoid(gate)


def mlp(h, w_up, w_down):
    a = jax.nn.relu(h @ w_up)
    return (a * a) @ w_down


def _fwd_setup_inputs(seed: int = 0) -> dict:
    key = jax.random.key(seed)
    ks = jax.random.split(key, 24)
    f32 = jnp.float32
    D, NC, NS, G, P, C = D_MODEL, N_CONV_LAYERS, N_SSM_LAYERS, SSM_GROUPS, SSM_STATE, SSM_GROUP

    def nrm(k, shape, scale):
        return jax.random.normal(k, shape, f32) * scale

    x = jax.random.normal(ks[0], (BATCH, SEQ, D), f32)
    mix_norm = 1.0 + nrm(ks[1], (DEPTH, D), 0.02)
    conv_w_in = nrm(ks[2], (NC, D, 2 * D), D ** -0.5)
    conv_b_in = nrm(ks[3], (NC, 2 * D), 0.02)
    conv_dw = nrm(ks[4], (NC, CONV_WIDTH, D), CONV_WIDTH ** -0.5)
    conv_dw_b = nrm(ks[5], (NC, D), 0.02)
    conv_ln_g = 1.0 + nrm(ks[6], (NC, D), 0.02)
    conv_ln_b = nrm(ks[7], (NC, D), 0.02)
    conv_w_out = nrm(ks[8], (NC, D, D), D ** -0.5)
    conv_b_out = nrm(ks[9], (NC, D), 0.02)
    ssm_lambda_re = -0.5 + nrm(ks[10], (NS, G, P), 0.01)
    ssm_lambda_im = (math.pi * jnp.arange(P, dtype=f32))[None, None, :] + nrm(ks[11], (NS, G, P), 0.01)
    ssm_log_dt = jax.random.uniform(ks[12], (NS, G), f32, math.log(DT_MIN), math.log(DT_MAX))
    ssm_b_re = nrm(ks[13], (NS, G, P, C), (2 * C) ** -0.5)
    ssm_b_im = nrm(ks[14], (NS, G, P, C), (2 * C) ** -0.5)
    ssm_c_re = nrm(ks[15], (NS, G, C, P), P ** -0.5)
    ssm_c_im = nrm(ks[16], (NS, G, C, P), P ** -0.5)
    ssm_d = 1.0 + nrm(ks[17], (NS, D), 0.1)
    ssm_w_glu = nrm(ks[18], (NS, D, 2 * D), D ** -0.5)
    mlp_norm = 1.0 + nrm(ks[19], (DEPTH, D), 0.02)
    mlp_w_up = nrm(ks[20], (DEPTH, D, D_FF), D ** -0.5)
    mlp_w_down = nrm(ks[21], (DEPTH, D_FF, D), D_FF ** -0.5)
    final_norm = 1.0 + nrm(ks[22], (D,), 0.02)
    return {"x": x, "mix_norm": mix_norm,
            "conv_w_in": conv_w_in, "conv_b_in": conv_b_in, "conv_dw": conv_dw,
            "conv_dw_b": conv_dw_b, "conv_ln_g": conv_ln_g, "conv_ln_b": conv_ln_b,
            "conv_w_out": conv_w_out, "conv_b_out": conv_b_out,
            "ssm_lambda_re": ssm_lambda_re, "ssm_lambda_im": ssm_lambda_im,
            "ssm_log_dt": ssm_log_dt, "ssm_b_re": ssm_b_re, "ssm_b_im": ssm_b_im,
            "ssm_c_re": ssm_c_re, "ssm_c_im": ssm_c_im, "ssm_d": ssm_d,
            "ssm_w_glu": ssm_w_glu, "mlp_norm": mlp_norm, "mlp_w_up": mlp_w_up,
            "mlp_w_down": mlp_w_down, "final_norm": final_norm}


def _fwd_reference(x, mix_norm, conv_w_in, conv_b_in, conv_dw, conv_dw_b, conv_ln_g, conv_ln_b,
              conv_w_out, conv_b_out, ssm_lambda_re, ssm_lambda_im, ssm_log_dt, ssm_b_re,
              ssm_b_im, ssm_c_re, ssm_c_im, ssm_d, ssm_w_glu, mlp_norm, mlp_w_up,
              mlp_w_down, final_norm):
    for i in range(DEPTH):
        h = rms_norm(x, mix_norm[i])
        j = i // N_MIXERS
        if i % N_MIXERS == 0:
            x = x + conv_module(h, conv_w_in[j], conv_b_in[j], conv_dw[j], conv_dw_b[j],
                                conv_ln_g[j], conv_ln_b[j], conv_w_out[j], conv_b_out[j])
        else:
            x = x + ssm_module(h, ssm_lambda_re[j], ssm_lambda_im[j], ssm_log_dt[j],
                               ssm_b_re[j], ssm_b_im[j], ssm_c_re[j], ssm_c_im[j],
                               ssm_d[j], ssm_w_glu[j])
        h = rms_norm(x, mlp_norm[i])
        x = x + mlp(h, mlp_w_up[i], mlp_w_down[i])
    return rms_norm(x, final_norm)


import jax as _jax
import jax.numpy as _jnp

TWIN_FORMAT = 'train_step'
FWD_PARAMS = ['x', 'mix_norm', 'conv_w_in', 'conv_b_in', 'conv_dw', 'conv_dw_b', 'conv_ln_g', 'conv_ln_b', 'conv_w_out', 'conv_b_out', 'ssm_lambda_re', 'ssm_lambda_im', 'ssm_log_dt', 'ssm_b_re', 'ssm_b_im', 'ssm_c_re', 'ssm_c_im', 'ssm_d', 'ssm_w_glu', 'mlp_norm', 'mlp_w_up', 'mlp_w_down', 'final_norm']
TWIN_WEIGHTS = ['mix_norm', 'conv_w_in', 'conv_b_in', 'conv_dw', 'conv_dw_b', 'conv_ln_g', 'conv_ln_b', 'conv_w_out', 'conv_b_out', 'ssm_lambda_re', 'ssm_lambda_im', 'ssm_log_dt', 'ssm_b_re', 'ssm_b_im', 'ssm_c_re', 'ssm_c_im', 'ssm_d', 'ssm_w_glu', 'mlp_norm', 'mlp_w_up', 'mlp_w_down', 'final_norm']
TWIN_DIFF_INPUT = 'x'
TWIN_INPUTS = ['x', 'mix_norm', 'conv_w_in', 'conv_b_in', 'conv_dw', 'conv_dw_b', 'conv_ln_g', 'conv_ln_b', 'conv_w_out', 'conv_b_out', 'ssm_lambda_re', 'ssm_lambda_im', 'ssm_log_dt', 'ssm_b_re', 'ssm_b_im', 'ssm_c_re', 'ssm_c_im', 'ssm_d', 'ssm_w_glu', 'mlp_norm', 'mlp_w_up', 'mlp_w_down', 'final_norm', 'loss_target', 'm_mix_norm', 'm_conv_w_in', 'm_conv_b_in', 'm_conv_dw', 'm_conv_dw_b', 'm_conv_ln_g', 'm_conv_ln_b', 'm_conv_w_out', 'm_conv_b_out', 'm_ssm_lambda_re', 'm_ssm_lambda_im', 'm_ssm_log_dt', 'm_ssm_b_re', 'm_ssm_b_im', 'm_ssm_c_re', 'm_ssm_c_im', 'm_ssm_d', 'm_ssm_w_glu', 'm_mlp_norm', 'm_mlp_w_up', 'm_mlp_w_down', 'm_final_norm', 'v_mix_norm', 'v_conv_w_in', 'v_conv_b_in', 'v_conv_dw', 'v_conv_dw_b', 'v_conv_ln_g', 'v_conv_ln_b', 'v_conv_w_out', 'v_conv_b_out', 'v_ssm_lambda_re', 'v_ssm_lambda_im', 'v_ssm_log_dt', 'v_ssm_b_re', 'v_ssm_b_im', 'v_ssm_c_re', 'v_ssm_c_im', 'v_ssm_d', 'v_ssm_w_glu', 'v_mlp_norm', 'v_mlp_w_up', 'v_mlp_w_down', 'v_final_norm']
TWIN_OUTPUTS = ['loss', 'grad_x', 'grad_mix_norm', 'grad_conv_w_in', 'grad_conv_b_in', 'grad_conv_dw', 'grad_conv_dw_b', 'grad_conv_ln_g', 'grad_conv_ln_b', 'grad_conv_w_out', 'grad_conv_b_out', 'grad_ssm_lambda_re', 'grad_ssm_lambda_im', 'grad_ssm_log_dt', 'grad_ssm_b_re', 'grad_ssm_b_im', 'grad_ssm_c_re', 'grad_ssm_c_im', 'grad_ssm_d', 'grad_ssm_w_glu', 'grad_mlp_norm', 'grad_mlp_w_up', 'grad_mlp_w_down', 'grad_final_norm', 'delta_mix_norm', 'delta_conv_w_in', 'delta_conv_b_in', 'delta_conv_dw', 'delta_conv_dw_b', 'delta_conv_ln_g', 'delta_conv_ln_b', 'delta_conv_w_out', 'delta_conv_b_out', 'delta_ssm_lambda_re', 'delta_ssm_lambda_im', 'delta_ssm_log_dt', 'delta_ssm_b_re', 'delta_ssm_b_im', 'delta_ssm_c_re', 'delta_ssm_c_im', 'delta_ssm_d', 'delta_ssm_w_glu', 'delta_mlp_norm', 'delta_mlp_w_up', 'delta_mlp_w_down', 'delta_final_norm', 'new_m_mix_norm', 'new_m_conv_w_in', 'new_m_conv_b_in', 'new_m_conv_dw', 'new_m_conv_dw_b', 'new_m_conv_ln_g', 'new_m_conv_ln_b', 'new_m_conv_w_out', 'new_m_conv_b_out', 'new_m_ssm_lambda_re', 'new_m_ssm_lambda_im', 'new_m_ssm_log_dt', 'new_m_ssm_b_re', 'new_m_ssm_b_im', 'new_m_ssm_c_re', 'new_m_ssm_c_im', 'new_m_ssm_d', 'new_m_ssm_w_glu', 'new_m_mlp_norm', 'new_m_mlp_w_up', 'new_m_mlp_w_down', 'new_m_final_norm', 'new_v_mix_norm', 'new_v_conv_w_in', 'new_v_conv_b_in', 'new_v_conv_dw', 'new_v_conv_dw_b', 'new_v_conv_ln_g', 'new_v_conv_ln_b', 'new_v_conv_w_out', 'new_v_conv_b_out', 'new_v_ssm_lambda_re', 'new_v_ssm_lambda_im', 'new_v_ssm_log_dt', 'new_v_ssm_b_re', 'new_v_ssm_b_im', 'new_v_ssm_c_re', 'new_v_ssm_c_im', 'new_v_ssm_d', 'new_v_ssm_w_glu', 'new_v_mlp_norm', 'new_v_mlp_w_up', 'new_v_mlp_w_down', 'new_v_final_norm']
TWIN_LEAF_KINDS = {'loss': 'loss', 'grad_x': 'grad_x', 'grad_mix_norm': 'grad_w', 'grad_conv_w_in': 'grad_w', 'grad_conv_b_in': 'grad_w', 'grad_conv_dw': 'grad_w', 'grad_conv_dw_b': 'grad_w', 'grad_conv_ln_g': 'grad_w', 'grad_conv_ln_b': 'grad_w', 'grad_conv_w_out': 'grad_w', 'grad_conv_b_out': 'grad_w', 'grad_ssm_lambda_re': 'grad_w', 'grad_ssm_lambda_im': 'grad_w', 'grad_ssm_log_dt': 'grad_w', 'grad_ssm_b_re': 'grad_w', 'grad_ssm_b_im': 'grad_w', 'grad_ssm_c_re': 'grad_w', 'grad_ssm_c_im': 'grad_w', 'grad_ssm_d': 'grad_w', 'grad_ssm_w_glu': 'grad_w', 'grad_mlp_norm': 'grad_w', 'grad_mlp_w_up': 'grad_w', 'grad_mlp_w_down': 'grad_w', 'grad_final_norm': 'grad_w', 'delta_mix_norm': 'delta_w', 'delta_conv_w_in': 'delta_w', 'delta_conv_b_in': 'delta_w', 'delta_conv_dw': 'delta_w', 'delta_conv_dw_b': 'delta_w', 'delta_conv_ln_g': 'delta_w', 'delta_conv_ln_b': 'delta_w', 'delta_conv_w_out': 'delta_w', 'delta_conv_b_out': 'delta_w', 'delta_ssm_lambda_re': 'delta_w', 'delta_ssm_lambda_im': 'delta_w', 'delta_ssm_log_dt': 'delta_w', 'delta_ssm_b_re': 'delta_w', 'delta_ssm_b_im': 'delta_w', 'delta_ssm_c_re': 'delta_w', 'delta_ssm_c_im': 'delta_w', 'delta_ssm_d': 'delta_w', 'delta_ssm_w_glu': 'delta_w', 'delta_mlp_norm': 'delta_w', 'delta_mlp_w_up': 'delta_w', 'delta_mlp_w_down': 'delta_w', 'delta_final_norm': 'delta_w', 'new_m_mix_norm': 'new_m', 'new_m_conv_w_in': 'new_m', 'new_m_conv_b_in': 'new_m', 'new_m_conv_dw': 'new_m', 'new_m_conv_dw_b': 'new_m', 'new_m_conv_ln_g': 'new_m', 'new_m_conv_ln_b': 'new_m', 'new_m_conv_w_out': 'new_m', 'new_m_conv_b_out': 'new_m', 'new_m_ssm_lambda_re': 'new_m', 'new_m_ssm_lambda_im': 'new_m', 'new_m_ssm_log_dt': 'new_m', 'new_m_ssm_b_re': 'new_m', 'new_m_ssm_b_im': 'new_m', 'new_m_ssm_c_re': 'new_m', 'new_m_ssm_c_im': 'new_m', 'new_m_ssm_d': 'new_m', 'new_m_ssm_w_glu': 'new_m', 'new_m_mlp_norm': 'new_m', 'new_m_mlp_w_up': 'new_m', 'new_m_mlp_w_down': 'new_m', 'new_m_final_norm': 'new_m', 'new_v_mix_norm': 'new_v', 'new_v_conv_w_in': 'new_v', 'new_v_conv_b_in': 'new_v', 'new_v_conv_dw': 'new_v', 'new_v_conv_dw_b': 'new_v', 'new_v_conv_ln_g': 'new_v', 'new_v_conv_ln_b': 'new_v', 'new_v_conv_w_out': 'new_v', 'new_v_conv_b_out': 'new_v', 'new_v_ssm_lambda_re': 'new_v', 'new_v_ssm_lambda_im': 'new_v', 'new_v_ssm_log_dt': 'new_v', 'new_v_ssm_b_re': 'new_v', 'new_v_ssm_b_im': 'new_v', 'new_v_ssm_c_re': 'new_v', 'new_v_ssm_c_im': 'new_v', 'new_v_ssm_d': 'new_v', 'new_v_ssm_w_glu': 'new_v', 'new_v_mlp_norm': 'new_v', 'new_v_mlp_w_up': 'new_v', 'new_v_mlp_w_down': 'new_v', 'new_v_final_norm': 'new_v'}


def _forward(args):
    return _fwd_reference(*[args[k] for k in FWD_PARAMS])


def _output_shape():
    def fwd():
        inp = _fwd_setup_inputs(0)
        return _fwd_reference(*[inp[k] for k in FWD_PARAMS])
    out = _jax.eval_shape(fwd)
    return out.shape, out.dtype

N_MICROBATCH = 1
ADAM_LR = 0.001
ADAM_B1 = 0.9
ADAM_B2 = 0.999
ADAM_EPS = 1e-08
ADAM_WD = 0.01
ADAM_STEP = 10
PER_EXAMPLE_BATCH_AXIS = {'x': 0, 'loss_target': 0}
SHARED_INPUTS = []
_WEIGHT_DTYPES = {'mix_norm': _jnp.float32, 'conv_w_in': _jnp.float32, 'conv_b_in': _jnp.float32, 'conv_dw': _jnp.float32, 'conv_dw_b': _jnp.float32, 'conv_ln_g': _jnp.float32, 'conv_ln_b': _jnp.float32, 'conv_w_out': _jnp.float32, 'conv_b_out': _jnp.float32, 'ssm_lambda_re': _jnp.float32, 'ssm_lambda_im': _jnp.float32, 'ssm_log_dt': _jnp.float32, 'ssm_b_re': _jnp.float32, 'ssm_b_im': _jnp.float32, 'ssm_c_re': _jnp.float32, 'ssm_c_im': _jnp.float32, 'ssm_d': _jnp.float32, 'ssm_w_glu': _jnp.float32, 'mlp_norm': _jnp.float32, 'mlp_w_up': _jnp.float32, 'mlp_w_down': _jnp.float32, 'final_norm': _jnp.float32}
MOMENT_SCALE = {'mix_norm': 5.922834e-02, 'conv_w_in': 5.010955e-02, 'conv_b_in': 7.670048e-02, 'conv_dw': 6.637299e-02, 'conv_dw_b': 1.769230e-01, 'conv_ln_g': 9.285966e-02, 'conv_ln_b': 1.093375e-01, 'conv_w_out': 7.284915e-02, 'conv_b_out': 2.168507e-01, 'ssm_lambda_re': 7.074314e-03, 'ssm_lambda_im': 6.442982e-03, 'ssm_log_dt': 1.442997e+00, 'ssm_b_re': 2.595865e-03, 'ssm_b_im': 2.861190e-03, 'ssm_c_re': 3.894658e-03, 'ssm_c_im': 3.710378e-03, 'ssm_d': 4.468884e-02, 'ssm_w_glu': 2.482203e-02, 'mlp_norm': 9.569538e-02, 'mlp_w_up': 4.795788e-02, 'mlp_w_down': 1.076480e-01, 'final_norm': 3.323682e+01}


def _to_microbatches(a, axis):
    t = _jnp.moveaxis(a, axis, 0)
    t = t.reshape((N_MICROBATCH, t.shape[0] // N_MICROBATCH) + t.shape[1:])
    return _jnp.moveaxis(t, 1, axis + 1)


def setup_inputs(seed: int = 0) -> dict:
    inp = _fwd_setup_inputs(seed)
    key = _jax.random.fold_in(_jax.random.key(seed), 7919)
    shape, _ = _output_shape()
    out = dict(inp)
    out["loss_target"] = _jax.random.normal(_jax.random.fold_in(key, 0), shape, _jnp.float32)
    for i, name in enumerate(TWIN_WEIGHTS):
        w = inp[name].astype(_jnp.float32)
        if MOMENT_SCALE is None:
            s = _jnp.sqrt(_jnp.mean(_jnp.square(w)) + 1e-30)
        else:
            s = MOMENT_SCALE[name]
        km, kv = _jax.random.split(_jax.random.fold_in(key, i + 1))
        out[name] = w
        out["m_" + name] = s * _jax.random.normal(km, w.shape, _jnp.float32)
        out["v_" + name] = (s * s) * _jax.random.uniform(kv, w.shape, _jnp.float32, 0.5, 1.5)
    if N_MICROBATCH > 1:
        for name, axis in PER_EXAMPLE_BATCH_AXIS.items():
            out[name] = _to_microbatches(out[name], axis)
    return {'x': out['x'], 'mix_norm': out['mix_norm'], 'conv_w_in': out['conv_w_in'], 'conv_b_in': out['conv_b_in'], 'conv_dw': out['conv_dw'], 'conv_dw_b': out['conv_dw_b'], 'conv_ln_g': out['conv_ln_g'], 'conv_ln_b': out['conv_ln_b'], 'conv_w_out': out['conv_w_out'], 'conv_b_out': out['conv_b_out'], 'ssm_lambda_re': out['ssm_lambda_re'], 'ssm_lambda_im': out['ssm_lambda_im'], 'ssm_log_dt': out['ssm_log_dt'], 'ssm_b_re': out['ssm_b_re'], 'ssm_b_im': out['ssm_b_im'], 'ssm_c_re': out['ssm_c_re'], 'ssm_c_im': out['ssm_c_im'], 'ssm_d': out['ssm_d'], 'ssm_w_glu': out['ssm_w_glu'], 'mlp_norm': out['mlp_norm'], 'mlp_w_up': out['mlp_w_up'], 'mlp_w_down': out['mlp_w_down'], 'final_norm': out['final_norm'], 'loss_target': out['loss_target'], 'm_mix_norm': out['m_mix_norm'], 'm_conv_w_in': out['m_conv_w_in'], 'm_conv_b_in': out['m_conv_b_in'], 'm_conv_dw': out['m_conv_dw'], 'm_conv_dw_b': out['m_conv_dw_b'], 'm_conv_ln_g': out['m_conv_ln_g'], 'm_conv_ln_b': out['m_conv_ln_b'], 'm_conv_w_out': out['m_conv_w_out'], 'm_conv_b_out': out['m_conv_b_out'], 'm_ssm_lambda_re': out['m_ssm_lambda_re'], 'm_ssm_lambda_im': out['m_ssm_lambda_im'], 'm_ssm_log_dt': out['m_ssm_log_dt'], 'm_ssm_b_re': out['m_ssm_b_re'], 'm_ssm_b_im': out['m_ssm_b_im'], 'm_ssm_c_re': out['m_ssm_c_re'], 'm_ssm_c_im': out['m_ssm_c_im'], 'm_ssm_d': out['m_ssm_d'], 'm_ssm_w_glu': out['m_ssm_w_glu'], 'm_mlp_norm': out['m_mlp_norm'], 'm_mlp_w_up': out['m_mlp_w_up'], 'm_mlp_w_down': out['m_mlp_w_down'], 'm_final_norm': out['m_final_norm'], 'v_mix_norm': out['v_mix_norm'], 'v_conv_w_in': out['v_conv_w_in'], 'v_conv_b_in': out['v_conv_b_in'], 'v_conv_dw': out['v_conv_dw'], 'v_conv_dw_b': out['v_conv_dw_b'], 'v_conv_ln_g': out['v_conv_ln_g'], 'v_conv_ln_b': out['v_conv_ln_b'], 'v_conv_w_out': out['v_conv_w_out'], 'v_conv_b_out': out['v_conv_b_out'], 'v_ssm_lambda_re': out['v_ssm_lambda_re'], 'v_ssm_lambda_im': out['v_ssm_lambda_im'], 'v_ssm_log_dt': out['v_ssm_log_dt'], 'v_ssm_b_re': out['v_ssm_b_re'], 'v_ssm_b_im': out['v_ssm_b_im'], 'v_ssm_c_re': out['v_ssm_c_re'], 'v_ssm_c_im': out['v_ssm_c_im'], 'v_ssm_d': out['v_ssm_d'], 'v_ssm_w_glu': out['v_ssm_w_glu'], 'v_mlp_norm': out['v_mlp_norm'], 'v_mlp_w_up': out['v_mlp_w_up'], 'v_mlp_w_down': out['v_mlp_w_down'], 'v_final_norm': out['v_final_norm']}


def _loss(weights, diff, rest, loss_target):
    with _jax.named_scope("forward"):
        args = {**rest, TWIN_DIFF_INPUT: diff, **{k: w.astype(_WEIGHT_DTYPES[k]) for k, w in weights.items()}}
        y = _forward(args)
    with _jax.named_scope("loss_head"):
        err = _jnp.square(y.astype(_jnp.float32) - loss_target)
        return 0.5 * _jnp.sum(_jnp.mean(err, axis=-1)) if err.ndim else 0.5 * err


def _adamw(w, g, m, v):
    m = ADAM_B1 * m + (1.0 - ADAM_B1) * g
    v = ADAM_B2 * v + (1.0 - ADAM_B2) * _jnp.square(g)
    m_hat = m / (1.0 - ADAM_B1 ** ADAM_STEP)
    v_hat = v / (1.0 - ADAM_B2 ** ADAM_STEP)
    delta = -ADAM_LR * (m_hat / (_jnp.sqrt(v_hat) + ADAM_EPS) + ADAM_WD * w)
    return delta, m, v


def reference(x, mix_norm, conv_w_in, conv_b_in, conv_dw, conv_dw_b, conv_ln_g, conv_ln_b, conv_w_out, conv_b_out, ssm_lambda_re, ssm_lambda_im, ssm_log_dt, ssm_b_re, ssm_b_im, ssm_c_re, ssm_c_im, ssm_d, ssm_w_glu, mlp_norm, mlp_w_up, mlp_w_down, final_norm, loss_target, m_mix_norm, m_conv_w_in, m_conv_b_in, m_conv_dw, m_conv_dw_b, m_conv_ln_g, m_conv_ln_b, m_conv_w_out, m_conv_b_out, m_ssm_lambda_re, m_ssm_lambda_im, m_ssm_log_dt, m_ssm_b_re, m_ssm_b_im, m_ssm_c_re, m_ssm_c_im, m_ssm_d, m_ssm_w_glu, m_mlp_norm, m_mlp_w_up, m_mlp_w_down, m_final_norm, v_mix_norm, v_conv_w_in, v_conv_b_in, v_conv_dw, v_conv_dw_b, v_conv_ln_g, v_conv_ln_b, v_conv_w_out, v_conv_b_out, v_ssm_lambda_re, v_ssm_lambda_im, v_ssm_log_dt, v_ssm_b_re, v_ssm_b_im, v_ssm_c_re, v_ssm_c_im, v_ssm_d, v_ssm_w_glu, v_mlp_norm, v_mlp_w_up, v_mlp_w_down, v_final_norm):
    given = dict(x=x, mix_norm=mix_norm, conv_w_in=conv_w_in, conv_b_in=conv_b_in, conv_dw=conv_dw, conv_dw_b=conv_dw_b, conv_ln_g=conv_ln_g, conv_ln_b=conv_ln_b, conv_w_out=conv_w_out, conv_b_out=conv_b_out, ssm_lambda_re=ssm_lambda_re, ssm_lambda_im=ssm_lambda_im, ssm_log_dt=ssm_log_dt, ssm_b_re=ssm_b_re, ssm_b_im=ssm_b_im, ssm_c_re=ssm_c_re, ssm_c_im=ssm_c_im, ssm_d=ssm_d, ssm_w_glu=ssm_w_glu, mlp_norm=mlp_norm, mlp_w_up=mlp_w_up, mlp_w_down=mlp_w_down, final_norm=final_norm, loss_target=loss_target, m_mix_norm=m_mix_norm, m_conv_w_in=m_conv_w_in, m_conv_b_in=m_conv_b_in, m_conv_dw=m_conv_dw, m_conv_dw_b=m_conv_dw_b, m_conv_ln_g=m_conv_ln_g, m_conv_ln_b=m_conv_ln_b, m_conv_w_out=m_conv_w_out, m_conv_b_out=m_conv_b_out, m_ssm_lambda_re=m_ssm_lambda_re, m_ssm_lambda_im=m_ssm_lambda_im, m_ssm_log_dt=m_ssm_log_dt, m_ssm_b_re=m_ssm_b_re, m_ssm_b_im=m_ssm_b_im, m_ssm_c_re=m_ssm_c_re, m_ssm_c_im=m_ssm_c_im, m_ssm_d=m_ssm_d, m_ssm_w_glu=m_ssm_w_glu, m_mlp_norm=m_mlp_norm, m_mlp_w_up=m_mlp_w_up, m_mlp_w_down=m_mlp_w_down, m_final_norm=m_final_norm, v_mix_norm=v_mix_norm, v_conv_w_in=v_conv_w_in, v_conv_b_in=v_conv_b_in, v_conv_dw=v_conv_dw, v_conv_dw_b=v_conv_dw_b, v_conv_ln_g=v_conv_ln_g, v_conv_ln_b=v_conv_ln_b, v_conv_w_out=v_conv_w_out, v_conv_b_out=v_conv_b_out, v_ssm_lambda_re=v_ssm_lambda_re, v_ssm_lambda_im=v_ssm_lambda_im, v_ssm_log_dt=v_ssm_log_dt, v_ssm_b_re=v_ssm_b_re, v_ssm_b_im=v_ssm_b_im, v_ssm_c_re=v_ssm_c_re, v_ssm_c_im=v_ssm_c_im, v_ssm_d=v_ssm_d, v_ssm_w_glu=v_ssm_w_glu, v_mlp_norm=v_mlp_norm, v_mlp_w_up=v_mlp_w_up, v_mlp_w_down=v_mlp_w_down, v_final_norm=v_final_norm)
    weights = {n: given[n] for n in TWIN_WEIGHTS}
    shared = {n: given[n] for n in SHARED_INPUTS}
    per_example = {n: given[n] for n in ['x']}
    grad_fn = _jax.value_and_grad(_loss, argnums=(0, 1))

    def one_microbatch(ex, loss_target):
        ex = dict(ex)
        diff = ex.pop(TWIN_DIFF_INPUT)
        return grad_fn(weights, diff, {**shared, **ex}, loss_target)

    if N_MICROBATCH == 1:
        loss, (grad_w, grad_x) = one_microbatch(per_example, given["loss_target"])
    else:
        def body(carry, xs):
            loss_sum, grad_sum = carry
            l_k, (gw_k, gx_k) = one_microbatch(xs[0], xs[1])
            with _jax.named_scope("update"):
                return (loss_sum + l_k, _jax.tree.map(_jnp.add, grad_sum, gw_k)), gx_k

        init = (_jnp.zeros((), _jnp.float32), _jax.tree.map(_jnp.zeros_like, weights))
        (loss, grad_w), grad_x = _jax.lax.scan(body, init, (per_example, given["loss_target"]))
    with _jax.named_scope("update"):
        delta_w, new_m, new_v = {}, {}, {}
        for n in TWIN_WEIGHTS:
            delta_w[n], new_m[n], new_v[n] = _adamw(weights[n], grad_w[n], given["m_" + n], given["v_" + n])
    return (loss, grad_x, *[grad_w[n] for n in TWIN_WEIGHTS], *[delta_w[n] for n in TWIN_WEIGHTS],
            *[new_m[n] for n in TWIN_WEIGHTS], *[new_v[n] for n in TWIN_WEIGHTS])
```

```python
import functools
import math

import jax
import jax.numpy as jnp
from jax import lax
from jax.experimental import pallas as pl
from jax.experimental.pallas import tpu as pltpu

F32 = jnp.float32
BF16 = jnp.bfloat16
MESH = pl.DeviceIdType.MESH

EPS = 1e-6
ADAM_LR = 0.001
ADAM_B1 = 0.9
ADAM_B2 = 0.999
ADAM_EPS = 1e-08
ADAM_WD = 0.01
ADAM_STEP = 10

N_CHIPS = 4
N_DEV = 8
SSM_C = 16
SSM_P = 64
GROUPS_PER_BLOCK = 8
CONV_HALO = 32
SCAN_SUB = 32
VMEM_LIMIT = 56 * 1024 * 1024


def _pick(n, pref, align):
    t = min(n, pref)
    t -= t % align
    while t >= align:
        if n % t == 0:
            return t
        t -= align
    return n


def _params(*sem):
    return pltpu.CompilerParams(dimension_semantics=sem, vmem_limit_bytes=VMEM_LIMIT)


def _sigmoid(x):
    return 1.0 / (1.0 + jnp.exp(-x))


def _fold8(z):
    r, n = z.shape
    return jnp.sum(z.reshape(r // 8, 8, n), axis=0)


def _w_col(j, kt, nps, off):
    def idx(m, n, k):
        return ((n + off) // nps, j * kt + k, (n + off) % nps)
    return idx


def _w_row(j, kps):
    def idx(m, n, k):
        return (k // kps, j * kps + k % kps, n)
    return idx


def _mm_nn(name, a, w, w_maps, extras, outs, epilogue, *, tm, tn, tk, a_square=False):
    M, K = a.shape
    N = outs[0][0][1]
    grid = (M // tm, N // tn, K // tk)
    nk, nv, ne, no = grid[2], len(w_maps), len(extras), len(outs)
    in_specs = [pl.BlockSpec((tm, tk), lambda m, n, k: (m, k))]
    args = [a]
    for wm in w_maps:
        in_specs.append(pl.BlockSpec((None, tk, tn), wm))
        args.append(w)
    for arr, kind, off in extras:
        if kind == "mn":
            in_specs.append(pl.BlockSpec((tm, tn), lambda m, n, k: (m, n)))
        else:
            in_specs.append(pl.BlockSpec((1, tn), functools.partial(lambda m, n, k, o: (0, n + o), o=off // tn)))
        args.append(arr)
    out_specs = [pl.BlockSpec((tm, tn), lambda m, n, k: (m, n)) for _ in outs]
    out_shape = [jax.ShapeDtypeStruct(s, d) for s, d in outs]

    def body(*refs):
        a_ref = refs[0]
        w_refs = refs[1:1 + nv]
        e_refs = refs[1 + nv:1 + nv + ne]
        o_refs = refs[1 + nv + ne:1 + nv + ne + no]
        acc_refs = refs[1 + nv + ne + no:]
        k = pl.program_id(2)

        @pl.when(k == 0)
        def _():
            for acc in acc_refs:
                acc[...] = jnp.zeros_like(acc)

        av = a_ref[...]
        if a_square:
            af = av.astype(F32)
            av = af * af
        av = av.astype(BF16)
        for acc, w_ref in zip(acc_refs, w_refs):
            acc[...] += jnp.dot(av, w_ref[...], preferred_element_type=F32)

        @pl.when(k == nk - 1)
        def _():
            res = epilogue([acc[...] for acc in acc_refs], [e[...] for e in e_refs])
            for o_ref, r in zip(o_refs, res):
                o_ref[...] = r.astype(o_ref.dtype)

    return pl.pallas_call(
        body, name=name, grid=grid, in_specs=in_specs, out_specs=out_specs, out_shape=out_shape,
        scratch_shapes=[pltpu.VMEM((tm, tn), F32) for _ in range(nv)],
        compiler_params=_params("parallel", "parallel", "arbitrary"),
    )(*args)


def _mm_nt(name, a, w, w_map, extras, outs, epilogue, *, tm, tr, tc):
    M, N = a.shape
    R = outs[0][0][1]
    grid = (M // tm, R // tr, N // tc)
    nc, ne, no = grid[2], len(extras), len(outs)
    in_specs = [pl.BlockSpec((tm, tc), lambda m, r, c: (m, c)), pl.BlockSpec((None, tr, tc), w_map)]
    args = [a, w]
    for arr in extras:
        in_specs.append(pl.BlockSpec((tm, tr), lambda m, r, c: (m, r)))
        args.append(arr)
    out_specs = [pl.BlockSpec((tm, tr), lambda m, r, c: (m, r)) for _ in outs]
    out_shape = [jax.ShapeDtypeStruct(s, d) for s, d in outs]

    def body(*refs):
        a_ref, w_ref = refs[0], refs[1]
        e_refs = refs[2:2 + ne]
        o_refs = refs[2 + ne:2 + ne + no]
        acc = refs[2 + ne + no]
        c = pl.program_id(2)

        @pl.when(c == 0)
        def _():
            acc[...] = jnp.zeros_like(acc)

        acc[...] += lax.dot_general(a_ref[...].astype(BF16), w_ref[...], (((1,), (1,)), ((), ())),
                                    preferred_element_type=F32)

        @pl.when(c == nc - 1)
        def _():
            res = epilogue(acc[...], [e[...] for e in e_refs])
            for o_ref, r in zip(o_refs, res):
                o_ref[...] = r.astype(o_ref.dtype)

    return pl.pallas_call(
        body, name=name, grid=grid, in_specs=in_specs, out_specs=out_specs, out_shape=out_shape,
        scratch_shapes=[pltpu.VMEM((tm, tr), F32)],
        compiler_params=_params("parallel", "parallel", "arbitrary"),
    )(*args)


def _nt_col(j, rt, cps):
    def idx(m, r, c):
        return (c // cps, j * rt + r, c % cps)
    return idx


def _nt_row(j, rps):
    def idx(m, r, c):
        return (r // rps, j * rps + r % rps, c)
    return idx


def _mm_tn(name, a, b, out_sharding, n_shards, *, tr, tc, tt, a_square=False):
    T, R = a.shape
    N = b.shape[1]
    S = n_shards
    grid = (R // tr, N // tc, T // tt)
    if out_sharding == "col":
        nps = (N // S) // tc
        out_shape = jax.ShapeDtypeStruct((S, R, N // S), F32)
        out_spec = pl.BlockSpec((None, tr, tc), lambda r, n, t: (n // nps, r, n % nps))
    else:
        rps = (R // S) // tr
        out_shape = jax.ShapeDtypeStruct((S, R // S, N), F32)
        out_spec = pl.BlockSpec((None, tr, tc), lambda r, n, t: (r // rps, r % rps, n))

    def body(a_ref, b_ref, o_ref):
        t = pl.program_id(2)

        @pl.when(t == 0)
        def _():
            o_ref[...] = jnp.zeros_like(o_ref)

        av = a_ref[...]
        if a_square:
            af = av.astype(F32)
            av = af * af
        o_ref[...] += lax.dot_general(av.astype(BF16), b_ref[...].astype(BF16), (((0,), (0,)), ((), ())),
                                      preferred_element_type=F32)

    return pl.pallas_call(
        body, name=name, grid=grid,
        in_specs=[pl.BlockSpec((tt, tr), lambda r, n, t: (t, r)), pl.BlockSpec((tt, tc), lambda r, n, t: (t, n))],
        out_specs=out_spec, out_shape=out_shape,
        compiler_params=_params("parallel", "parallel", "arbitrary"),
    )(a, b)


def _rms_fwd(name, x, g, *, tr):
    T, D = x.shape

    def body(x_ref, g_ref, h_ref):
        xv = x_ref[...]
        r = lax.rsqrt(jnp.mean(xv * xv, axis=-1, keepdims=True) + EPS)
        h_ref[...] = (xv * r * g_ref[...]).astype(BF16)

    return pl.pallas_call(
        body, name=name, grid=(T // tr,),
        in_specs=[pl.BlockSpec((tr, D), lambda i: (i, 0)), pl.BlockSpec((1, D), lambda i: (0, 0))],
        out_specs=pl.BlockSpec((tr, D), lambda i: (i, 0)),
        out_shape=jax.ShapeDtypeStruct((T, D), BF16),
        compiler_params=_params("parallel"),
    )(x, g)


def _rms_bwd_rows(xv, gv, dh):
    r = lax.rsqrt(jnp.mean(xv * xv, axis=-1, keepdims=True) + EPS)
    xh = xv * r
    gdy = dh * gv
    dx = r * (gdy - xh * jnp.mean(gdy * xh, axis=-1, keepdims=True))
    return dx, dh * xh


def _rms_bwd(name, x, g, dh, dx_in, *, tr):
    T, D = x.shape
    nt = T // tr

    def body(x_ref, g_ref, dh_ref, dxi_ref, dx_ref, dxb_ref, dg_ref, cs_ref, dg_acc, cs_acc):
        i = pl.program_id(0)

        @pl.when(i == 0)
        def _():
            dg_acc[...] = jnp.zeros_like(dg_acc)
            cs_acc[...] = jnp.zeros_like(cs_acc)

        dx, dgx = _rms_bwd_rows(x_ref[...], g_ref[...], dh_ref[...].astype(F32))
        dxo = dxi_ref[...] + dx
        dx_ref[...] = dxo
        dxb_ref[...] = dxo.astype(BF16)
        dg_acc[...] += _fold8(dgx)
        cs_acc[...] += _fold8(dxo)

        @pl.when(i == nt - 1)
        def _():
            dg_ref[...] = jnp.sum(dg_acc[...], axis=0, keepdims=True)
            cs_ref[...] = jnp.sum(cs_acc[...], axis=0, keepdims=True)

    row = pl.BlockSpec((tr, D), lambda i: (i, 0))
    vec = pl.BlockSpec((1, D), lambda i: (0, 0))
    return pl.pallas_call(
        body, name=name, grid=(nt,),
        in_specs=[row, vec, row, row], out_specs=[row, row, vec, vec],
        out_shape=[jax.ShapeDtypeStruct((T, D), F32), jax.ShapeDtypeStruct((T, D), BF16),
                   jax.ShapeDtypeStruct((1, D), F32), jax.ShapeDtypeStruct((1, D), F32)],
        scratch_shapes=[pltpu.VMEM((8, D), F32), pltpu.VMEM((8, D), F32)],
        compiler_params=_params("arbitrary"),
    )(x, g, dh, dx_in)


def _final_loss(name, x, g, target, *, tr):
    T, D = x.shape
    nt = T // tr

    def body(x_ref, g_ref, t_ref, dx_ref, dxb_ref, loss_ref, dg_ref, l_acc, dg_acc):
        i = pl.program_id(0)

        @pl.when(i == 0)
        def _():
            l_acc[...] = jnp.zeros_like(l_acc)
            dg_acc[...] = jnp.zeros_like(dg_acc)

        xv = x_ref[...]
        gv = g_ref[...]
        r = lax.rsqrt(jnp.mean(xv * xv, axis=-1, keepdims=True) + EPS)
        err = xv * r * gv - t_ref[...]
        l_acc[...] += _fold8(err * err)
        dx, dgx = _rms_bwd_rows(xv, gv, err * (1.0 / D))
        dx_ref[...] = dx
        dxb_ref[...] = dx.astype(BF16)
        dg_acc[...] += _fold8(dgx)

        @pl.when(i == nt - 1)
        def _():
            tot = jnp.sum(jnp.sum(l_acc[...], axis=0, keepdims=True), axis=1, keepdims=True)
            loss_ref[...] = jnp.broadcast_to(tot * (0.5 / D), loss_ref.shape)
            dg_ref[...] = jnp.sum(dg_acc[...], axis=0, keepdims=True)

    row = pl.BlockSpec((tr, D), lambda i: (i, 0))
    vec = pl.BlockSpec((1, D), lambda i: (0, 0))
    return pl.pallas_call(
        body, name=name, grid=(nt,),
        in_specs=[row, vec, row],
        out_specs=[row, row, pl.BlockSpec((8, 128), lambda i: (0, 0)), vec],
        out_shape=[jax.ShapeDtypeStruct((T, D), F32), jax.ShapeDtypeStruct((T, D), BF16),
                   jax.ShapeDtypeStruct((8, 128), F32), jax.ShapeDtypeStruct((1, D), F32)],
        scratch_shapes=[pltpu.VMEM((8, D), F32), pltpu.VMEM((8, D), F32)],
        compiler_params=_params("arbitrary"),
    )(x, g, target)


def _conv_taps(ext_ref, dwp_ref, tr, width, first):
    acc = None
    for j in range(width):
        term = dwp_ref[pl.ds(j, 1), :] * ext_ref[pl.ds(first + j, tr), :]
        acc = term if acc is None else acc + term
    return acc


def _ln_rows(v2, lg, lb):
    mu = jnp.mean(v2, axis=-1, keepdims=True)
    xc = v2 - mu
    rs = lax.rsqrt(jnp.mean(xc * xc, axis=-1, keepdims=True) + EPS)
    xh = xc * rs
    return xh, rs, xh * lg + lb


def _halo_prev(tr):
    q = tr // CONV_HALO
    return lambda i: (jnp.maximum(i * q - 1, 0), 0)


def _conv_fwd(name, v, dwp, dwb, lg, lb, *, tr, width):
    T, D = v.shape
    first = CONV_HALO - (width - 1)

    def body(v_ref, halo_ref, dwp_ref, dwb_ref, lg_ref, lb_ref, s_ref, ext):
        i = pl.program_id(0)
        ext[pl.ds(0, CONV_HALO), :] = jnp.where(i == 0, 0.0, halo_ref[...])
        ext[pl.ds(CONV_HALO, tr), :] = v_ref[...]
        v2 = _conv_taps(ext, dwp_ref, tr, width, first) + dwb_ref[...]
        _, _, ln = _ln_rows(v2, lg_ref[...], lb_ref[...])
        s_ref[...] = (ln * _sigmoid(ln)).astype(BF16)

    row = pl.BlockSpec((tr, D), lambda i: (i, 0))
    vec = pl.BlockSpec((1, D), lambda i: (0, 0))
    return pl.pallas_call(
        body, name=name, grid=(T // tr,),
        in_specs=[row, pl.BlockSpec((CONV_HALO, D), _halo_prev(tr)),
                  pl.BlockSpec((CONV_HALO, D), lambda i: (0, 0)), vec, vec, vec],
        out_specs=row, out_shape=jax.ShapeDtypeStruct((T, D), BF16),
        scratch_shapes=[pltpu.VMEM((tr + CONV_HALO, D), F32)],
        compiler_params=_params("parallel"),
    )(v, v, dwp, dwb, lg, lb)


def _conv_bwd_a(name, v, ds, dwp, dwb, lg, lb, *, tr, width):
    T, D = v.shape
    nt = T // tr
    first = CONV_HALO - (width - 1)

    def body(v_ref, halo_ref, ds_ref, dwp_ref, dwb_ref, lg_ref, lb_ref,
             dv2_ref, dlg_ref, dlb_ref, ddwb_ref, ext, a_lg, a_lb, a_dwb):
        i = pl.program_id(0)

        @pl.when(i == 0)
        def _():
            a_lg[...] = jnp.zeros_like(a_lg)
            a_lb[...] = jnp.zeros_like(a_lb)
            a_dwb[...] = jnp.zeros_like(a_dwb)

        ext[pl.ds(0, CONV_HALO), :] = jnp.where(i == 0, 0.0, halo_ref[...])
        ext[pl.ds(CONV_HALO, tr), :] = v_ref[...]
        v2 = _conv_taps(ext, dwp_ref, tr, width, first) + dwb_ref[...]
        lgv = lg_ref[...]
        xh, rs, ln = _ln_rows(v2, lgv, lb_ref[...])
        sg = _sigmoid(ln)
        dln = ds_ref[...] * (sg * (1.0 + ln * (1.0 - sg)))
        a_lg[...] += _fold8(dln * xh)
        a_lb[...] += _fold8(dln)
        dxh = dln * lgv
        dv2 = rs * (dxh - jnp.mean(dxh, axis=-1, keepdims=True)
                    - xh * jnp.mean(dxh * xh, axis=-1, keepdims=True))
        dv2_ref[...] = dv2
        a_dwb[...] += _fold8(dv2)

        @pl.when(i == nt - 1)
        def _():
            dlg_ref[...] = jnp.sum(a_lg[...], axis=0, keepdims=True)
            dlb_ref[...] = jnp.sum(a_lb[...], axis=0, keepdims=True)
            ddwb_ref[...] = jnp.sum(a_dwb[...], axis=0, keepdims=True)

    row = pl.BlockSpec((tr, D), lambda i: (i, 0))
    vec = pl.BlockSpec((1, D), lambda i: (0, 0))
    return pl.pallas_call(
        body, name=name, grid=(nt,),
        in_specs=[row, pl.BlockSpec((CONV_HALO, D), _halo_prev(tr)), row,
                  pl.BlockSpec((CONV_HALO, D), lambda i: (0, 0)), vec, vec, vec],
        out_specs=[row, vec, vec, vec],
        out_shape=[jax.ShapeDtypeStruct((T, D), F32)] + [jax.ShapeDtypeStruct((1, D), F32)] * 3,
        scratch_shapes=[pltpu.VMEM((tr + CONV_HALO, D), F32)] + [pltpu.VMEM((8, D), F32)] * 3,
        compiler_params=_params("arbitrary"),
    )(v, v, ds, dwp, dwb, lg, lb)


def _conv_bwd_b(name, dv2, v, a, g, dwp, *, tr, width):
    T, D = v.shape
    nt = T // tr
    q = tr // CONV_HALO
    first = CONV_HALO - (width - 1)
    last_halo = T // CONV_HALO - 1

    def body(dv2_ref, nxt_ref, v_ref, halo_ref, a_ref, g_ref, dwp_ref,
             du_ref, ddw_ref, dbin_ref, ext_v, ext_d, a_dw, a_b):
        i = pl.program_id(0)

        @pl.when(i == 0)
        def _():
            a_dw[...] = jnp.zeros_like(a_dw)
            a_b[...] = jnp.zeros_like(a_b)

        ext_v[pl.ds(0, CONV_HALO), :] = jnp.where(i == 0, 0.0, halo_ref[...])
        ext_v[pl.ds(CONV_HALO, tr), :] = v_ref[...]
        d_cur = dv2_ref[...]
        ext_d[pl.ds(0, tr), :] = d_cur
        ext_d[pl.ds(tr, CONV_HALO), :] = jnp.where(i == nt - 1, 0.0, nxt_ref[...])
        dv = None
        for j in range(width):
            term = dwp_ref[pl.ds(j, 1), :] * ext_d[pl.ds(width - 1 - j, tr), :]
            dv = term if dv is None else dv + term
            a_dw[j] += _fold8(d_cur * ext_v[pl.ds(first + j, tr), :])
        av = a_ref[...].astype(F32)
        sg = _sigmoid(g_ref[...].astype(F32))
        da = dv * sg
        dg = dv * av * sg * (1.0 - sg)
        du_ref[:, pl.ds(0, D)] = da.astype(BF16)
        du_ref[:, pl.ds(D, D)] = dg.astype(BF16)
        a_b[:, pl.ds(0, D)] += _fold8(da)
        a_b[:, pl.ds(D, D)] += _fold8(dg)

        @pl.when(i == nt - 1)
        def _():
            ddw_ref[...] = jnp.sum(a_dw[...], axis=1)
            dbin_ref[...] = jnp.sum(a_b[...], axis=0, keepdims=True)

    row = pl.BlockSpec((tr, D), lambda i: (i, 0))
    return pl.pallas_call(
        body, name=name, grid=(nt,),
        in_specs=[row, pl.BlockSpec((CONV_HALO, D), lambda i: (jnp.minimum((i + 1) * q, last_halo), 0)),
                  row, pl.BlockSpec((CONV_HALO, D), _halo_prev(tr)), row, row,
                  pl.BlockSpec((CONV_HALO, D), lambda i: (0, 0))],
        out_specs=[pl.BlockSpec((tr, 2 * D), lambda i: (i, 0)),
                   pl.BlockSpec((CONV_HALO, D), lambda i: (0, 0)),
                   pl.BlockSpec((1, 2 * D), lambda i: (0, 0))],
        out_shape=[jax.ShapeDtypeStruct((T, 2 * D), BF16), jax.ShapeDtypeStruct((CONV_HALO, D), F32),
                   jax.ShapeDtypeStruct((1, 2 * D), F32)],
        scratch_shapes=[pltpu.VMEM((tr + CONV_HALO, D), F32), pltpu.VMEM((tr + CONV_HALO, D), F32),
                        pltpu.VMEM((CONV_HALO, 8, D), F32), pltpu.VMEM((8, 2 * D), F32)],
        compiler_params=_params("arbitrary"),
    )(dv2, dv2, v, v, a, g, dwp)


def _glu_bwd(name, dout, val, gate, *, tr):
    T, D = dout.shape

    def body(d_ref, v_ref, g_ref, dz_ref):
        d = d_ref[...]
        sg = _sigmoid(g_ref[...].astype(F32))
        dz_ref[:, pl.ds(0, D)] = (d * sg).astype(BF16)
        dz_ref[:, pl.ds(D, D)] = (d * v_ref[...].astype(F32) * sg * (1.0 - sg)).astype(BF16)

    row = pl.BlockSpec((tr, D), lambda i: (i, 0))
    return pl.pallas_call(
        body, name=name, grid=(T // tr,), in_specs=[row, row, row],
        out_specs=pl.BlockSpec((tr, 2 * D), lambda i: (i, 0)),
        out_shape=jax.ShapeDtypeStruct((T, 2 * D), BF16),
        compiler_params=_params("parallel"),
    )(dout, val, gate)


GELU_C = math.sqrt(2.0 / math.pi)
GELU_A = 0.044715


def _gelu(x):
    return 0.5 * x * (1.0 + jnp.tanh(GELU_C * (x + GELU_A * x * x * x)))


def _gelu_grad(x):
    t = jnp.tanh(GELU_C * (x + GELU_A * x * x * x))
    return 0.5 * (1.0 + t) + 0.5 * x * (1.0 - t * t) * GELU_C * (1.0 + 3.0 * GELU_A * x * x)


def _cmul(ar, ai, br, bi):
    return ar * br - ai * bi, ar * bi + ai * br


def _shift_rows(x, shift, down):
    n = x.shape[0]
    if shift % 8 == 0:
        z = jnp.zeros((shift, x.shape[1]), x.dtype)
        return jnp.concatenate([z, x[:n - shift]], axis=0) if down else jnp.concatenate([x[shift:], z], axis=0)
    return pltpu.roll(x, shift if down else n - shift, 0)


def _scan_chunk(xr, xi, ap_ref, cin_r, cin_i, *, lanes, reverse):
    n = xr.shape[0]
    sub = ap_ref.shape[0]
    pos = lax.broadcasted_iota(jnp.int32, (n, 1), 0) % sub
    sign = -1.0 if reverse else 1.0
    shift = 1
    while shift < sub:
        row = sub - shift if reverse else shift - 1
        ar = ap_ref[pl.ds(row, 1), pl.ds(0, lanes)]
        ai = sign * ap_ref[pl.ds(row, 1), pl.ds(lanes, lanes)]
        keep = (pos < sub - shift) if reverse else (pos >= shift)
        pr = jnp.where(keep, _shift_rows(xr, shift, not reverse), 0.0)
        pi = jnp.where(keep, _shift_rows(xi, shift, not reverse), 0.0)
        dr, di = _cmul(ar, ai, pr, pi)
        xr, xi = xr + dr, xi + di
        shift *= 2
    pw_r = ap_ref[:, pl.ds(0, lanes)]
    pw_i = sign * ap_ref[:, pl.ds(lanes, lanes)]
    blocks = range(n // sub)
    out_r, out_i = [None] * len(blocks), [None] * len(blocks)
    cr, ci = cin_r, cin_i
    for b in (reversed(blocks) if reverse else blocks):
        dr, di = _cmul(pw_r, pw_i, cr, ci)
        sr = xr[b * sub:(b + 1) * sub] + dr
        si = xi[b * sub:(b + 1) * sub] + di
        edge = 0 if reverse else sub - 1
        cr, ci = sr[edge:edge + 1], si[edge:edge + 1]
        out_r[b], out_i[b] = sr, si
    return jnp.concatenate(out_r, axis=0), jnp.concatenate(out_i, axis=0), cr, ci


def _ssm_fwd(name, h, bbd, ccd, ap, dskip, *, tc):
    T, D = h.shape
    NG, CB, L2 = bbd.shape
    lanes = L2 // 2
    nch = T // tc

    def body(h_ref, bb_ref, cc_ref, ap_ref, d_ref, y_ref, cin_ref, carry):
        t = pl.program_id(1)

        @pl.when(t == 0)
        def _():
            carry[...] = jnp.zeros_like(carry)

        cin_ref[...] = carry[...]
        u = h_ref[...]
        bu = jnp.dot(u, bb_ref[...], preferred_element_type=F32)
        sr, si, cr, ci = _scan_chunk(bu[:, :lanes], bu[:, lanes:], ap_ref,
                                     carry[pl.ds(0, 1), pl.ds(0, lanes)], carry[pl.ds(0, 1), pl.ds(lanes, lanes)],
                                     lanes=lanes, reverse=False)
        carry[pl.ds(0, 1), pl.ds(0, lanes)] = cr
        carry[pl.ds(0, 1), pl.ds(lanes, lanes)] = ci
        s = jnp.concatenate([sr, si], axis=1).astype(BF16)
        yp = jnp.dot(s, cc_ref[...], preferred_element_type=F32) + d_ref[...] * u.astype(F32)
        y_ref[...] = _gelu(yp).astype(BF16)

    return pl.pallas_call(
        body, name=name, grid=(NG, nch),
        in_specs=[pl.BlockSpec((tc, CB), lambda b, t: (t, b)),
                  pl.BlockSpec((None, CB, L2), lambda b, t: (b, 0, 0)),
                  pl.BlockSpec((None, L2, CB), lambda b, t: (b, 0, 0)),
                  pl.BlockSpec((None, SCAN_SUB, L2), lambda b, t: (b, 0, 0)),
                  pl.BlockSpec((1, CB), lambda b, t: (0, b))],
        out_specs=[pl.BlockSpec((tc, CB), lambda b, t: (t, b)),
                   pl.BlockSpec((None, None, 8, L2), lambda b, t: (b, t, 0, 0))],
        out_shape=[jax.ShapeDtypeStruct((T, D), BF16), jax.ShapeDtypeStruct((NG, nch, 8, L2), F32)],
        scratch_shapes=[pltpu.VMEM((8, L2), F32)],
        compiler_params=_params("parallel", "arbitrary"),
    )(h, bbd, ccd, ap, dskip)


def _ssm_bwd(name, h, dy, cins, bbd, ccd, ap, apr, dskip, *, tc):
    T, D = h.shape
    NG, CB, L2 = bbd.shape
    lanes = L2 // 2
    nch = T // tc

    def body(h_ref, dy_ref, cin_ref, bb_ref, cc_ref, ap_ref, apr_ref, d_ref,
             dh_ref, dbb_ref, dcc_ref, da_ref, dd_ref, gcarry, a_da, a_dd):
        t = pl.program_id(1)

        @pl.when(t == 0)
        def _():
            gcarry[...] = jnp.zeros_like(gcarry)
            a_da[...] = jnp.zeros_like(a_da)
            a_dd[...] = jnp.zeros_like(a_dd)
            dbb_ref[...] = jnp.zeros_like(dbb_ref)
            dcc_ref[...] = jnp.zeros_like(dcc_ref)

        u = h_ref[...]
        uf = u.astype(F32)
        cin_r = cin_ref[pl.ds(0, 1), pl.ds(0, lanes)]
        cin_i = cin_ref[pl.ds(0, 1), pl.ds(lanes, lanes)]
        bu = jnp.dot(u, bb_ref[...], preferred_element_type=F32)
        sr, si, _, _ = _scan_chunk(bu[:, :lanes], bu[:, lanes:], ap_ref, cin_r, cin_i, lanes=lanes, reverse=False)
        s = jnp.concatenate([sr, si], axis=1).astype(BF16)
        dv = d_ref[...]
        yp = jnp.dot(s, cc_ref[...], preferred_element_type=F32) + dv * uf
        dyp = dy_ref[...] * _gelu_grad(yp)
        a_dd[...] += _fold8(dyp * uf)
        dypb = dyp.astype(BF16)
        dcc_ref[...] += lax.dot_general(s, dypb, (((0,), (0,)), ((), ())), preferred_element_type=F32)
        ds = lax.dot_general(dypb, cc_ref[...], (((1,), (1,)), ((), ())), preferred_element_type=F32)
        gr, gi, cr, ci = _scan_chunk(ds[:, :lanes], ds[:, lanes:], apr_ref,
                                     gcarry[pl.ds(0, 1), pl.ds(0, lanes)], gcarry[pl.ds(0, 1), pl.ds(lanes, lanes)],
                                     lanes=lanes, reverse=True)
        gcarry[pl.ds(0, 1), pl.ds(0, lanes)] = cr
        gcarry[pl.ds(0, 1), pl.ds(lanes, lanes)] = ci
        gb = jnp.concatenate([gr, gi], axis=1).astype(BF16)
        du = lax.dot_general(gb, bb_ref[...], (((1,), (1,)), ((), ())), preferred_element_type=F32)
        dh_ref[...] = du + dv * dyp
        dbb_ref[...] += lax.dot_general(u, gb, (((0,), (0,)), ((), ())), preferred_element_type=F32)
        first = lax.broadcasted_iota(jnp.int32, (tc, 1), 0) == 0
        pr = jnp.where(first, cin_r, pltpu.roll(sr, 1, 0))
        pi = jnp.where(first, cin_i, pltpu.roll(si, 1, 0))
        a_da[:, pl.ds(0, lanes)] += _fold8(pr * gr + pi * gi)
        a_da[:, pl.ds(lanes, lanes)] += _fold8(pr * gi - pi * gr)

        @pl.when(t == nch - 1)
        def _():
            da_ref[...] = jnp.sum(a_da[...], axis=0, keepdims=True)
            dd_ref[...] = jnp.sum(a_dd[...], axis=0, keepdims=True)

    rev = lambda b, t: (nch - 1 - t, b)
    return pl.pallas_call(
        body, name=name, grid=(NG, nch),
        in_specs=[pl.BlockSpec((tc, CB), rev), pl.BlockSpec((tc, CB), rev),
                  pl.BlockSpec((None, None, 8, L2), lambda b, t: (b, nch - 1 - t, 0, 0)),
                  pl.BlockSpec((None, CB, L2), lambda b, t: (b, 0, 0)),
                  pl.BlockSpec((None, L2, CB), lambda b, t: (b, 0, 0)),
                  pl.BlockSpec((None, SCAN_SUB, L2), lambda b, t: (b, 0, 0)),
                  pl.BlockSpec((None, SCAN_SUB, L2), lambda b, t: (b, 0, 0)),
                  pl.BlockSpec((1, CB), lambda b, t: (0, b))],
        out_specs=[pl.BlockSpec((tc, CB), rev),
                   pl.BlockSpec((None, CB, L2), lambda b, t: (b, 0, 0)),
                   pl.BlockSpec((None, L2, CB), lambda b, t: (b, 0, 0)),
                   pl.BlockSpec((None, 1, L2), lambda b, t: (b, 0, 0)),
                   pl.BlockSpec((1, CB), lambda b, t: (0, b))],
        out_shape=[jax.ShapeDtypeStruct((T, D), F32), jax.ShapeDtypeStruct((NG, CB, L2), F32),
                   jax.ShapeDtypeStruct((NG, L2, CB), F32), jax.ShapeDtypeStruct((NG, 1, L2), F32),
                   jax.ShapeDtypeStruct((1, D), F32)],
        scratch_shapes=[pltpu.VMEM((8, L2), F32), pltpu.VMEM((8, L2), F32), pltpu.VMEM((8, CB), F32)],
        compiler_params=_params("parallel", "arbitrary"),
    )(h, dy, cins, bbd, ccd, ap, apr, dskip)


def _zoh(lr, li, ldt):
    dt = jnp.exp(ldt)
    mag = jnp.exp(lr * dt)
    ar = mag * jnp.cos(li * dt)
    ai = mag * jnp.sin(li * dt)
    den = lr * lr + li * li
    nr = ar - 1.0
    kr = (nr * lr + ai * li) / den
    ki = (ai * lr - nr * li) / den
    return dt, ar, ai, kr, ki, den


def _ssm_prep(name, lr, li, ldt, br, bi):
    shp = jax.ShapeDtypeStruct(lr.shape, F32)

    def body(lr_ref, li_ref, ldt_ref, br_ref, bi_ref, ar_ref, ai_ref, kr_ref, ki_ref, bbr_ref, bbi_ref):
        _, ar, ai, kr, ki, _ = _zoh(lr_ref[...], li_ref[...], ldt_ref[...])
        ar_ref[...] = ar
        ai_ref[...] = ai
        kr_ref[...] = kr
        ki_ref[...] = ki
        bbr, bbi = _cmul(kr, ki, br_ref[...], bi_ref[...])
        bbr_ref[...] = bbr
        bbi_ref[...] = bbi

    return pl.pallas_call(body, name=name, out_shape=[shp] * 6)(lr, li, ldt, br, bi)


def _ssm_powers(name, ar, ai):
    NG, _, lanes = ar.shape

    def body(ar_ref, ai_ref, ap_ref, apr_ref):
        a_r, a_i = ar_ref[...], ai_ref[...]
        pw = [(a_r, a_i)]
        for _ in range(7):
            pw.append(_cmul(pw[-1][0], pw[-1][1], a_r, a_i))
        row = lax.broadcasted_iota(jnp.int32, (8, lanes), 0)
        fr = fi = rr = ri = jnp.zeros((8, lanes), F32)
        for n in range(8):
            fr = jnp.where(row == n, pw[n][0], fr)
            fi = jnp.where(row == n, pw[n][1], fi)
            rr = jnp.where(row == 7 - n, pw[n][0], rr)
            ri = jnp.where(row == 7 - n, pw[n][1], ri)
        top_r, top_i = pw[7]
        size = 8
        while size < SCAN_SUB:
            hr, hi = _cmul(fr, fi, top_r, top_i)
            fr, fi = jnp.concatenate([fr, hr], axis=0), jnp.concatenate([fi, hi], axis=0)
            hr, hi = _cmul(rr, ri, top_r, top_i)
            rr, ri = jnp.concatenate([hr, rr], axis=0), jnp.concatenate([hi, ri], axis=0)
            top_r, top_i = _cmul(top_r, top_i, top_r, top_i)
            size *= 2
        ap_ref[:, pl.ds(0, lanes)] = fr
        ap_ref[:, pl.ds(lanes, lanes)] = fi
        apr_ref[:, pl.ds(0, lanes)] = rr
        apr_ref[:, pl.ds(lanes, lanes)] = ri

    vec = pl.BlockSpec((None, 1, lanes), lambda b: (b, 0, 0))
    tab = pl.BlockSpec((None, SCAN_SUB, 2 * lanes), lambda b: (b, 0, 0))
    shp = jax.ShapeDtypeStruct((NG, SCAN_SUB, 2 * lanes), F32)
    return pl.pallas_call(body, name=name, grid=(NG,), in_specs=[vec, vec], out_specs=[tab, tab],
                          out_shape=[shp, shp], compiler_params=_params("parallel"))(ar, ai)


def _ssm_prep_bwd_b(name, kr, ki, br, bi, gbr, gbi):
    shp = jax.ShapeDtypeStruct(kr.shape, F32)

    def body(kr_ref, ki_ref, br_ref, bi_ref, gr_ref, gi_ref, dbr_ref, dbi_ref, tr_ref, ti_ref):
        gr, gi = gr_ref[...], gi_ref[...]
        dbr, dbi = _cmul(kr_ref[...], -ki_ref[...], gr, gi)
        dbr_ref[...] = dbr
        dbi_ref[...] = dbi
        t_r, t_i = _cmul(br_ref[...], -bi_ref[...], gr, gi)
        tr_ref[...] = t_r
        ti_ref[...] = t_i

    return pl.pallas_call(body, name=name, out_shape=[shp] * 4)(kr, ki, br, bi, gbr, gbi)


def _ssm_prep_bwd_a(name, lr, li, ldt, gar, gai, tkr, tki):
    G, P = lr.shape

    def body(lr_ref, li_ref, ldt_ref, gar_ref, gai_ref, tkr_ref, tki_ref, dlr_ref, dli_ref, dldt_ref):
        lr_v, li_v = lr_ref[...], li_ref[...]
        dt, ar, ai, kr, ki, den = _zoh(lr_v, li_v, ldt_ref[...])
        gkr = jnp.sum(tkr_ref[...], axis=0)
        gki = jnp.sum(tki_ref[...], axis=0)
        ir, ii = lr_v / den, -li_v / den
        t_r, t_i = _cmul(ir, -ii, gkr, gki)
        gar_t, gai_t = gar_ref[...] + t_r, gai_ref[...] + t_i
        qr, qi = _cmul(kr, ki, ir, ii)
        t_r, t_i = _cmul(-qr, qi, gkr, gki)
        u_r, u_i = _cmul(dt * ar, -dt * ai, gar_t, gai_t)
        dlr_ref[...] = u_r + t_r
        dli_ref[...] = u_i + t_i
        la_r, la_i = _cmul(lr_v, li_v, ar, ai)
        w_r, _ = _cmul(la_r, -la_i, gar_t, gai_t)
        ddt = jnp.sum(w_r, axis=1, keepdims=True)
        dldt_ref[...] = jnp.broadcast_to(ddt * dt[:, 0:1], dldt_ref.shape)

    shp = jax.ShapeDtypeStruct((G, P), F32)
    return pl.pallas_call(body, name=name, out_shape=[shp, shp, jax.ShapeDtypeStruct((G, 128), F32)])(
        lr, li, ldt, gar, gai, tkr, tki)


ROWS_CALL_TILE_ELEMS = 256 * 1024


def _rows_call(name, fn, ins, outs):
    R, Cn = ins[0].shape
    tr = _pick(R, max(16, ROWS_CALL_TILE_ELEMS // Cn), 16)
    spec = pl.BlockSpec((tr, Cn), lambda i: (i, 0))

    def body(*refs):
        res = fn(*[r[...] for r in refs[:len(ins)]])
        for o_ref, r in zip(refs[len(ins):], res):
            o_ref[...] = r.astype(o_ref.dtype)

    return pl.pallas_call(
        body, name=name, grid=(R // tr,), in_specs=[spec] * len(ins), out_specs=[spec] * len(outs),
        out_shape=[jax.ShapeDtypeStruct((R, Cn), d) for d in outs], compiler_params=_params("parallel"),
    )(*ins)


def _adamw_math(w, g, m, v):
    m = ADAM_B1 * m + (1.0 - ADAM_B1) * g
    v = ADAM_B2 * v + (1.0 - ADAM_B2) * (g * g)
    m_hat = m / (1.0 - ADAM_B1 ** ADAM_STEP)
    v_hat = v / (1.0 - ADAM_B2 ** ADAM_STEP)
    delta = -ADAM_LR * (m_hat / (jnp.sqrt(v_hat) + ADAM_EPS) + ADAM_WD * w)
    return delta, m, v


def _adamw(name, w, g, m, v):
    shape = w.shape
    cols = shape[-1]
    to2d = lambda t: t.reshape(-1, cols)
    res = _rows_call(name, _adamw_math, [to2d(w), to2d(g), to2d(m), to2d(v)], [F32, F32, F32])
    return [r.reshape(shape) for r in res]


ANY = pl.BlockSpec(memory_space=pl.ANY)


def _place():
    x, y, c = lax.axis_index("x"), lax.axis_index("y"), lax.axis_index("c")
    chips = [(1 - x, y), (x, 1 - y), (1 - x, 1 - y)]
    return x, y, c, chips


def _remote(src, dst, send_sem, recv_sem, dev):
    return pltpu.make_async_remote_copy(src_ref=src, dst_ref=dst, send_sem=send_sem, recv_sem=recv_sem,
                                        device_id=dev, device_id_type=MESH)


def _weight_gather(ws):
    n = len(ws)

    def body(*refs):
        w_refs, o_refs = refs[:n], refs[n:2 * n]
        send_sems, recv_sems, loc_sems = refs[2 * n:]
        x, y, c, chips = _place()
        me_k = 2 * x + y
        sibling = (x, y, 1 - c)
        locs, sends = [], []
        for i in range(n):
            half = ws[i].shape[0] // 2
            mine = pl.ds(c * half, half)
            loc = pltpu.make_async_copy(w_refs[i], o_refs[i].at[me_k], loc_sems.at[i])
            loc.start()
            locs.append(loc)
            for r, (cx, cy) in enumerate(chips):
                cp = _remote(w_refs[i].at[mine], o_refs[i].at[me_k, mine],
                             send_sems.at[6 * i + r], recv_sems.at[6 * i + r], (cx, cy, c))
                cp.start()
                sends.append(cp)
        for i in range(n):
            half = ws[i].shape[0] // 2
            mine = pl.ds(c * half, half)
            for r, (cx, cy) in enumerate(chips):
                got = o_refs[i].at[2 * cx + cy, mine]
                _remote(got, got, send_sems.at[6 * i + r], recv_sems.at[6 * i + r], sibling).wait_recv()
                fwd = _remote(got, got, send_sems.at[6 * i + 3 + r], recv_sems.at[6 * i + 3 + r], sibling)
                fwd.start()
                sends.append(fwd)
        for i in range(n):
            half = ws[i].shape[0] // 2
            theirs = pl.ds((1 - c) * half, half)
            for r, (cx, cy) in enumerate(chips):
                got = o_refs[i].at[2 * cx + cy, theirs]
                _remote(got, got, send_sems.at[6 * i + 3 + r], recv_sems.at[6 * i + 3 + r], sibling).wait_recv()
        for cp in sends:
            cp.wait_send()
        for loc in locs:
            loc.wait()

    return pl.pallas_call(
        body, name="weight_gather", in_specs=[ANY] * n, out_specs=[ANY] * n,
        out_shape=[jax.ShapeDtypeStruct((N_CHIPS,) + w.shape, w.dtype) for w in ws],
        scratch_shapes=[pltpu.SemaphoreType.DMA((6 * n,)), pltpu.SemaphoreType.DMA((6 * n,)),
                        pltpu.SemaphoreType.DMA((n,))],
    )(*ws)


def _pair_exchange(gs):
    n = len(gs)

    def body(*refs):
        g_refs, m_refs, r_refs = refs[:n], refs[n:2 * n], refs[2 * n:3 * n]
        send_sems, recv_sems, loc_sems = refs[3 * n:]
        x, y, c, _ = _place()
        sibling = (x, y, 1 - c)
        locs, cps = [], []
        for i in range(n):
            half = gs[i].shape[1] // 2
            loc = pltpu.make_async_copy(g_refs[i].at[:, pl.ds(c * half, half)], m_refs[i], loc_sems.at[i])
            loc.start()
            locs.append(loc)
            cp = _remote(g_refs[i].at[:, pl.ds((1 - c) * half, half)], r_refs[i],
                         send_sems.at[i], recv_sems.at[i], sibling)
            cp.start()
            cps.append(cp)
        for cp in cps:
            cp.wait_recv()
        for cp in cps:
            cp.wait_send()
        for loc in locs:
            loc.wait()

    halves = [jax.ShapeDtypeStruct((g.shape[0], g.shape[1] // 2, g.shape[2]), g.dtype) for g in gs]
    res = pl.pallas_call(
        body, name="grad_pair_exchange", in_specs=[ANY] * n, out_specs=[ANY] * (2 * n),
        out_shape=halves + halves,
        scratch_shapes=[pltpu.SemaphoreType.DMA((n,)), pltpu.SemaphoreType.DMA((n,)), pltpu.SemaphoreType.DMA((n,))],
    )(*gs)
    return res[:n], res[n:]


def _chip_exchange(ps, pbs):
    n = len(ps)

    def body(*refs):
        p_refs, pb_refs = refs[:n], refs[n:2 * n]
        own_refs = refs[2 * n:3 * n]
        rv_refs = refs[3 * n:6 * n]
        send_sems, recv_sems, loc_sems = refs[6 * n:]
        x, y, c, chips = _place()
        locs, cps = [], []
        for i in range(n):
            loc = pltpu.make_async_copy(p_refs[i].at[2 * x + y], own_refs[i], loc_sems.at[i])
            loc.start()
            locs.append(loc)
            for r, (cx, cy) in enumerate(chips):
                cp = _remote(pb_refs[i].at[2 * cx + cy], rv_refs[3 * i + r],
                             send_sems.at[3 * i + r], recv_sems.at[3 * i + r], (cx, cy, c))
                cp.start()
                cps.append(cp)
        for cp in cps:
            cp.wait_recv()
        for cp in cps:
            cp.wait_send()
        for loc in locs:
            loc.wait()

    own = [jax.ShapeDtypeStruct(p.shape[1:], p.dtype) for p in ps]
    rvs = [jax.ShapeDtypeStruct(pb.shape[1:], pb.dtype) for pb in pbs for _ in range(3)]
    res = pl.pallas_call(
        body, name="grad_chip_exchange", in_specs=[ANY] * (2 * n), out_specs=[ANY] * (4 * n),
        out_shape=own + rvs,
        scratch_shapes=[pltpu.SemaphoreType.DMA((3 * n,)), pltpu.SemaphoreType.DMA((3 * n,)),
                        pltpu.SemaphoreType.DMA((n,))],
    )(*ps, *pbs)
    return res[:n], [res[n + 3 * i:n + 3 * i + 3] for i in range(n)]


def _pair_gather(fs, groups):
    n = len(fs)
    ng = len(groups)

    def body(*refs):
        f_refs, o_refs = refs[:n], refs[n:n + ng]
        send_sems, recv_sems, loc_sems = refs[n + ng:]
        x, y, c, _ = _place()
        sibling = (x, y, 1 - c)
        locs, cps = [], []
        for gi, members in enumerate(groups):
            for i, layer in members:
                half = fs[i].shape[0]
                rows = pl.ds(layer * 2 * half + c * half, half)
                loc = pltpu.make_async_copy(f_refs[i], o_refs[gi].at[rows], loc_sems.at[i])
                loc.start()
                locs.append(loc)
                cp = _remote(f_refs[i], o_refs[gi].at[rows], send_sems.at[i], recv_sems.at[i], sibling)
                cp.start()
                cps.append(cp)
        for gi, members in enumerate(groups):
            for i, layer in members:
                half = fs[i].shape[0]
                theirs = o_refs[gi].at[pl.ds(layer * 2 * half + (1 - c) * half, half)]
                _remote(theirs, theirs, send_sems.at[i], recv_sems.at[i], sibling).wait_recv()
        for cp in cps:
            cp.wait_send()
        for loc in locs:
            loc.wait()

    out_shape = []
    for members in groups:
        i0 = members[0][0]
        out_shape.append(jax.ShapeDtypeStruct((len(members) * 2 * fs[i0].shape[0], fs[i0].shape[1]), fs[i0].dtype))
    return pl.pallas_call(
        body, name="grad_pair_gather", in_specs=[ANY] * n, out_specs=[ANY] * ng, out_shape=out_shape,
        scratch_shapes=[pltpu.SemaphoreType.DMA((n,)), pltpu.SemaphoreType.DMA((n,)), pltpu.SemaphoreType.DMA((n,))],
    )(*fs)


def _small_gather(name, v):
    def body(v_ref, o_ref, send_sems, recv_sems, loc_sem):
        x, y, c, chips = _place()
        me, sibling = (x, y, c), (x, y, 1 - c)

        def slot(px, py, pc):
            return o_ref.at[4 * px + 2 * py + pc]

        def copy(k, block, to, src=None):
            return _remote(slot(*block) if src is None else src, slot(*block), send_sems.at[k], recv_sems.at[k], to)

        mine = pltpu.make_async_copy(v_ref, slot(*me), loc_sem)
        mine.start()
        first = [copy(0, me, sibling, src=v_ref)]
        first += [copy(1 + j, me, (*chip, c), src=v_ref) for j, chip in enumerate(chips)]
        for cp in first:
            cp.start()
        passed = [copy(4 + j, (*chip, c), sibling) for j, chip in enumerate(chips)]
        for j, chip in enumerate(chips):
            copy(1 + j, (*chip, c), me).wait_recv()
            passed[j].start()
        copy(0, sibling, me).wait_recv()
        for j, chip in enumerate(chips):
            copy(4 + j, (*chip, 1 - c), me).wait_recv()
        for cp in first + passed:
            cp.wait_send()
        mine.wait()

    return pl.pallas_call(
        body, name=name, in_specs=[ANY], out_specs=ANY,
        out_shape=jax.ShapeDtypeStruct((N_DEV,) + v.shape, v.dtype),
        scratch_shapes=[pltpu.SemaphoreType.DMA((7,)), pltpu.SemaphoreType.DMA((7,)), pltpu.SemaphoreType.DMA],
    )(v)


def _sum_devices(name, g):
    n, R, Cn = g.shape
    tr = _pick(R, 512, 8)

    def body(g_ref, o_ref):
        acc = g_ref[0]
        for d in range(1, n):
            acc = acc + g_ref[d]
        o_ref[...] = acc

    return pl.pallas_call(
        body, name=name, grid=(R // tr,), in_specs=[pl.BlockSpec((n, tr, Cn), lambda i: (0, i, 0))],
        out_specs=pl.BlockSpec((tr, Cn), lambda i: (i, 0)), out_shape=jax.ShapeDtypeStruct((R, Cn), g.dtype),
        compiler_params=_params("parallel"),
    )(g)


WEIGHTS = ["mix_norm", "conv_w_in", "conv_b_in", "conv_dw", "conv_dw_b", "conv_ln_g", "conv_ln_b", "conv_w_out",
           "conv_b_out", "ssm_lambda_re", "ssm_lambda_im", "ssm_log_dt", "ssm_b_re", "ssm_b_im", "ssm_c_re",
           "ssm_c_im", "ssm_d", "ssm_w_glu", "mlp_norm", "mlp_w_up", "mlp_w_down", "final_norm"]
LARGE = ["conv_w_in", "conv_w_out", "ssm_w_glu", "mlp_w_up", "mlp_w_down"]
SHARDED_SMALL = ["conv_dw", "ssm_d"]
REPLICATED = [n for n in WEIGHTS if n not in LARGE and n not in SHARDED_SMALL]
PACK_QUANTUM = 8 * 128


def _pack(parts):
    rows = []
    for p in parts:
        f = p.reshape(-1)
        pad = (-f.shape[0]) % PACK_QUANTUM
        if pad:
            f = jnp.pad(f, (0, pad))
        rows.append(f.reshape(-1, 128))
    return jnp.concatenate(rows, axis=0)


def _packed_rows(shape):
    return -(-math.prod(shape) // PACK_QUANTUM) * 8


def _unpack(buf, shapes):
    out, r = [], 0
    for s in shapes:
        rows = _packed_rows(s)
        out.append(buf[r:r + rows].reshape(-1)[:math.prod(s)].reshape(s))
        r += rows
    return out


def _block_diag(t, pattern):
    return jnp.einsum(pattern, t, jnp.eye(GROUPS_PER_BLOCK, dtype=t.dtype))


def _local_step(xs, tgt, p, wg, S):
    T, D = xs.shape
    depth = p["mix_norm"].shape[0]
    width = p["conv_dw"].shape[1]
    G, P, C = D // SSM_C, SSM_P, SSM_C
    NG = G // GROUPS_PER_BLOCK
    lanes = GROUPS_PER_BLOCK * P
    F = wg["mlp_w_up"].shape[2] * S
    tm = _pick(T, 1024, 16)
    tmh = _pick(T, 512, 16)
    tre = _pick(T, 256, 16)
    trc = _pick(T, 128, CONV_HALO)
    tcs = _pick(T, 256, SCAN_SUB)
    row = lambda v: v.reshape(1, -1)

    def nn_col(name, a, w, j, offs, extras, outs, epi, tm_):
        K, Ns = a.shape[1], w.shape[2]
        tn, tk = _pick(Ns, 1024, 128), _pick(K, 1024, 128)
        maps = [_w_col(j, K // tk, Ns // tn, o // tn) for o in offs]
        return _mm_nn(name, a, w, maps, extras, outs, epi, tm=tm_, tn=tn, tk=tk)

    def nn_row(name, a, w, j, extras, outs, epi, a_square=False):
        Ks, N = a.shape[1] // S, w.shape[2]
        tn, tk = _pick(N, 1024, 128), _pick(Ks, 1024, 128)
        return _mm_nn(name, a, w, [_w_row(j, Ks // tk)], extras, outs, epi, tm=tm, tn=tn, tk=tk, a_square=a_square)

    def nt_col(name, a, w, j, R, extras, outs, epi):
        Cs = w.shape[2]
        tr, tc = _pick(R, 1024, 128), _pick(Cs, 1024, 128)
        return _mm_nt(name, a, w, _nt_col(j, R // tr, Cs // tc), extras, outs, epi, tm=tm, tr=tr, tc=tc)

    def nt_row(name, a, w, j, R, extras, outs, epi):
        Rs = R // S
        tr, tc = _pick(Rs, 1024, 128), _pick(w.shape[2], 1024, 128)
        return _mm_nt(name, a, w, _nt_row(j, Rs // tr), extras, outs, epi, tm=tm, tr=tr, tc=tc)

    def tn(name, a, b, sharding, a_square=False):
        R, N = a.shape[1], b.shape[1]
        if sharding == "col":
            tr, tc = _pick(R, 1024, 128), _pick(N // S, 1024, 128)
        else:
            tr, tc = _pick(R // S, 1024, 128), _pick(N, 1024, 128)
        return _mm_tn(name, a, b, sharding, S, tr=tr, tc=tc, tt=_pick(T, 1024, 128), a_square=a_square)

    ssm = []
    for j in range(p["ssm_lambda_re"].shape[0]):
        rep = lambda t: jnp.repeat(t, C, axis=0)
        lam_r, lam_i = p["ssm_lambda_re"][j], p["ssm_lambda_im"][j]
        ldt = jnp.broadcast_to(p["ssm_log_dt"][j][:, None], (G, P))
        b_r = p["ssm_b_re"][j].transpose(0, 2, 1).reshape(G * C, P)
        b_i = p["ssm_b_im"][j].transpose(0, 2, 1).reshape(G * C, P)
        ar, ai, kr, ki, bbr, bbi = _ssm_prep(f"ssm_prep_{j}", rep(lam_r), rep(lam_i), rep(ldt), b_r, b_i)
        ap, apr = _ssm_powers(f"ssm_powers_{j}", ar[::C].reshape(NG, 1, lanes), ai[::C].reshape(NG, 1, lanes))
        bd = lambda t: _block_diag(t.reshape(NG, GROUPS_PER_BLOCK, C, P), "bgcp,gh->bgchp").reshape(NG, 128, lanes)
        bbd = jnp.concatenate([bd(bbr), bd(bbi)], axis=2).astype(BF16)
        cd = lambda t: _block_diag(t.reshape(NG, GROUPS_PER_BLOCK, C, P), "bgcp,gh->bhpgc").reshape(NG, lanes, 128)
        ccd = jnp.concatenate([cd(p["ssm_c_re"][j]), -cd(p["ssm_c_im"][j])], axis=1).astype(BF16)
        ssm.append(dict(lam_r=lam_r, lam_i=lam_i, ldt=ldt, b_r=b_r, b_i=b_i, kr=kr, ki=ki, ap=ap, apr=apr,
                        bbd=bbd, ccd=ccd, dskip=row(p["ssm_d"][j])))

    dwp = [jnp.pad(p["conv_dw"][j], ((0, CONV_HALO - width), (0, 0))) for j in range(p["conv_dw"].shape[0])]

    saved = []
    x = xs
    for i in range(depth):
        j = i // 2
        s = dict(x_in=x)
        h = _rms_fwd(f"mix_norm_fwd_{i}", x, row(p["mix_norm"][i]), tr=tre)
        s["h"] = h
        if i % 2 == 0:
            def epi_in(accs, ex):
                a_, g_ = accs[0] + ex[0], accs[1] + ex[1]
                return [a_, g_, a_ * _sigmoid(g_)]
            b_in = row(p["conv_b_in"][j])
            a_, g_, v = nn_col(f"conv_in_{j}", h, wg["conv_w_in"], j, [0, D], [(b_in, "n", 0), (b_in, "n", D)],
                               [((T, D), BF16), ((T, D), BF16), ((T, D), F32)], epi_in, tmh)
            sl = _conv_fwd(f"conv_fwd_{j}", v, dwp[j], row(p["conv_dw_b"][j]), row(p["conv_ln_g"][j]),
                           row(p["conv_ln_b"][j]), tr=trc, width=width)
            x = nn_row(f"conv_out_{j}", sl, wg["conv_w_out"], j, [(row(p["conv_b_out"][j]), "n", 0), (x, "mn", 0)],
                       [((T, D), F32)], lambda accs, ex: [accs[0] + ex[0] + ex[1]])[0]
            s.update(a=a_, g=g_, v=v, s=sl)
        else:
            q = ssm[j]
            y, cins = _ssm_fwd(f"ssm_fwd_{j}", h, q["bbd"], q["ccd"], q["ap"], q["dskip"], tc=tcs)
            def epi_glu(accs, ex):
                return [accs[0], accs[1], accs[0] * _sigmoid(accs[1]) + ex[0]]
            val, gate, x = nn_col(f"ssm_glu_{j}", y, wg["ssm_w_glu"], j, [0, D], [(x, "mn", 0)],
                                  [((T, D), BF16), ((T, D), BF16), ((T, D), F32)], epi_glu, tmh)
            s.update(y=y, cins=cins, val=val, gate=gate)
        s["x_mid"] = x
        h2 = _rms_fwd(f"mlp_norm_fwd_{i}", x, row(p["mlp_norm"][i]), tr=tre)
        r = nn_col(f"mlp_up_{i}", h2, wg["mlp_w_up"], i, [0], [], [((T, F), BF16)],
                   lambda accs, ex: [jnp.maximum(accs[0], 0.0)], tm)[0]
        x = nn_row(f"mlp_down_{i}", r, wg["mlp_w_down"], i, [(x, "mn", 0)], [((T, D), F32)],
                   lambda accs, ex: [accs[0] + ex[0]], a_square=True)[0]
        s.update(h2=h2, r=r)
        saved.append(s)

    dx, dxb, loss8, dgf = _final_loss("final_loss", x, row(p["final_norm"]), tgt, tr=tre)

    n_conv, n_ssm = p["conv_dw"].shape[0], p["ssm_d"].shape[0]
    gs = {n: [None] * p[n].shape[0] for n in WEIGHTS if n not in LARGE and n != "final_norm"}
    gs.update(conv_w_in=[None] * n_conv, conv_w_out=[None] * n_conv, ssm_w_glu=[None] * n_ssm,
              mlp_w_up=[None] * depth, mlp_w_down=[None] * depth)
    gs["final_norm"] = dgf.reshape(-1)
    for i in reversed(range(depth)):
        j = i // 2
        s = saved[i]
        dz = nt_row(f"mlp_down_dx_{i}", dxb, wg["mlp_w_down"], i, F, [s["r"]], [((T, F), BF16)],
                    lambda acc, ex: [acc * (2.0 * ex[0].astype(F32))])[0]
        gs["mlp_w_down"][i] = tn(f"mlp_down_dw_{i}", s["r"], dxb, "row", a_square=True)
        gs["mlp_w_up"][i] = tn(f"mlp_up_dw_{i}", s["h2"], dz, "col")
        dh2 = nt_col(f"mlp_up_dx_{i}", dz, wg["mlp_w_up"], i, D, [], [((T, D), F32)], lambda acc, ex: [acc])[0]
        dx, dxb, dg, cs = _rms_bwd(f"mlp_norm_bwd_{i}", s["x_mid"], row(p["mlp_norm"][i]), dh2, dx, tr=tre)
        gs["mlp_norm"][i] = dg.reshape(-1)
        if i % 2 == 0:
            gs["conv_b_out"][j] = cs.reshape(-1)
            gs["conv_w_out"][j] = tn(f"conv_out_dw_{j}", s["s"], dxb, "row")
            dsl = nt_row(f"conv_out_dx_{j}", dxb, wg["conv_w_out"], j, D, [], [((T, D), F32)],
                         lambda acc, ex: [acc])[0]
            dv2, dlg, dlb, ddwb = _conv_bwd_a(f"conv_bwd_a_{j}", s["v"], dsl, dwp[j], row(p["conv_dw_b"][j]),
                                              row(p["conv_ln_g"][j]), row(p["conv_ln_b"][j]), tr=trc, width=width)
            du, ddw, dbin = _conv_bwd_b(f"conv_bwd_b_{j}", dv2, s["v"], s["a"], s["g"], dwp[j], tr=trc, width=width)
            gs["conv_ln_g"][j], gs["conv_ln_b"][j] = dlg.reshape(-1), dlb.reshape(-1)
            gs["conv_dw_b"][j], gs["conv_dw"][j], gs["conv_b_in"][j] = ddwb.reshape(-1), ddw[:width], dbin.reshape(-1)
            gs["conv_w_in"][j] = tn(f"conv_in_dw_{j}", s["h"], du, "col")
            dh = nt_col(f"conv_in_dx_{j}", du, wg["conv_w_in"], j, D, [], [((T, D), F32)], lambda acc, ex: [acc])[0]
        else:
            q = ssm[j]
            dz2 = _glu_bwd(f"ssm_glu_bwd_{j}", dx, s["val"], s["gate"], tr=tre)
            gs["ssm_w_glu"][j] = tn(f"ssm_glu_dw_{j}", s["y"], dz2, "col")
            dy = nt_col(f"ssm_glu_dx_{j}", dz2, wg["ssm_w_glu"], j, D, [], [((T, D), F32)], lambda acc, ex: [acc])[0]
            dh, dbbd, dccd, da, dd = _ssm_bwd(f"ssm_bwd_{j}", s["h"], dy, s["cins"], q["bbd"], q["ccd"],
                                              q["ap"], q["apr"], q["dskip"], tc=tcs)
            gbb = _block_diag(dbbd.reshape(NG, GROUPS_PER_BLOCK, C, 2, GROUPS_PER_BLOCK, P), "bgcrhp,gh->rbgcp")
            dbr, dbi, tkr, tki = _ssm_prep_bwd_b(f"ssm_prep_bwd_b_{j}", q["kr"], q["ki"], q["b_r"], q["b_i"],
                                                 gbb[0].reshape(G * C, P), gbb[1].reshape(G * C, P))
            unrow = lambda t: t.reshape(G, C, P).transpose(0, 2, 1)
            gs["ssm_b_re"][j], gs["ssm_b_im"][j] = unrow(dbr), unrow(dbi)
            per_c = lambda t: t.reshape(G, C, P).transpose(1, 0, 2)
            dlr, dli, dldt = _ssm_prep_bwd_a(f"ssm_prep_bwd_a_{j}", q["lam_r"], q["lam_i"], q["ldt"],
                                             da[:, 0, :lanes].reshape(G, P), da[:, 0, lanes:].reshape(G, P),
                                             per_c(tkr), per_c(tki))
            gs["ssm_lambda_re"][j], gs["ssm_lambda_im"][j], gs["ssm_log_dt"][j] = dlr, dli, dldt[:, 0]
            gcc = _block_diag(dccd.reshape(NG, 2, GROUPS_PER_BLOCK, P, GROUPS_PER_BLOCK, C), "brhpgc,gh->rbgcp")
            gs["ssm_c_re"][j], gs["ssm_c_im"][j] = gcc[0].reshape(G, C, P), -gcc[1].reshape(G, C, P)
            gs["ssm_d"][j] = dd.reshape(-1)
        dx, dxb, dg, _ = _rms_bwd(f"mix_norm_bwd_{i}", s["x_in"], row(p["mix_norm"][i]), dh, dx, tr=tre)
        gs["mix_norm"][i] = dg.reshape(-1)

    small = {n: (gs[n] if n == "final_norm" else jnp.stack(gs[n])) for n in WEIGHTS if n not in LARGE}
    large = {n: gs[n] for n in LARGE}
    return loss8[0, 0], dx, small, large


def kernel(x, mix_norm, conv_w_in, conv_b_in, conv_dw, conv_dw_b, conv_ln_g, conv_ln_b, conv_w_out, conv_b_out, ssm_lambda_re, ssm_lambda_im, ssm_log_dt, ssm_b_re, ssm_b_im, ssm_c_re, ssm_c_im, ssm_d, ssm_w_glu, mlp_norm, mlp_w_up, mlp_w_down, final_norm, loss_target, m_mix_norm, m_conv_w_in, m_conv_b_in, m_conv_dw, m_conv_dw_b, m_conv_ln_g, m_conv_ln_b, m_conv_w_out, m_conv_b_out, m_ssm_lambda_re, m_ssm_lambda_im, m_ssm_log_dt, m_ssm_b_re, m_ssm_b_im, m_ssm_c_re, m_ssm_c_im, m_ssm_d, m_ssm_w_glu, m_mlp_norm, m_mlp_w_up, m_mlp_w_down, m_final_norm, v_mix_norm, v_conv_w_in, v_conv_b_in, v_conv_dw, v_conv_dw_b, v_conv_ln_g, v_conv_ln_b, v_conv_w_out, v_conv_b_out, v_ssm_lambda_re, v_ssm_lambda_im, v_ssm_log_dt, v_ssm_b_re, v_ssm_b_im, v_ssm_c_re, v_ssm_c_im, v_ssm_d, v_ssm_w_glu, v_mlp_norm, v_mlp_w_up, v_mlp_w_down, v_final_norm):
    a = dict(locals())
    S = N_CHIPS
    w = {n: a[n] for n in WEIGHTS}
    k_chip = 2 * lax.axis_index("x") + lax.axis_index("y")

    flat = lambda t: t.reshape(-1, t.shape[-1])
    gathered = _weight_gather([flat(w[n]).astype(BF16) for n in LARGE])
    wg = dict(zip(LARGE, gathered))
    sh_shapes = [w[n].shape for n in SHARDED_SMALL]
    sh_all = _small_gather("small_weight_gather", _pack([w[n] for n in SHARDED_SMALL]))
    per_chip = [_unpack(sh_all[2 * k], sh_shapes) for k in range(S)]
    p = {n: w[n] for n in REPLICATED}
    for idx, n in enumerate(SHARDED_SMALL):
        p[n] = jnp.concatenate([per_chip[k][idx] for k in range(S)], axis=-1)

    loss_local, dx, small, large = _local_step(x[0], loss_target[0], p, wg, S)
    loss = lax.psum(loss_local, ("x", "y", "c"))

    order = [(n, j) for n in LARGE for j in range(len(large[n]))]
    arrs = [large[n][j] for n, j in order]
    mine, theirs = _pair_exchange(arrs)
    ps, pbs = [], []
    for idx, (m_, t_) in enumerate(zip(mine, theirs)):
        shp = m_.shape
        p32, p16 = _rows_call(f"grad_pair_sum_{idx}", lambda u, v_: (u + v_, u + v_),
                              [m_.reshape(-1, shp[2]), t_.reshape(-1, shp[2])], [F32, BF16])
        ps.append(p32.reshape(shp))
        pbs.append(p16.reshape(shp))
    owns, rvs = _chip_exchange(ps, pbs)
    fs = [_rows_call(f"grad_chip_sum_{idx}",
                     lambda o, r0, r1, r2: (o + r0.astype(F32) + r1.astype(F32) + r2.astype(F32),),
                     [own, *rv], [F32])[0] for idx, (own, rv) in enumerate(zip(owns, rvs))]
    groups = [[(idx, j) for idx, (n, j) in enumerate(order) if n == name] for name in LARGE]
    stacks = _pair_gather(fs, groups)
    grads = {n: st.reshape(w[n].shape) for n, st in zip(LARGE, stacks)}

    small_names = REPLICATED + SHARDED_SMALL
    g_all = _small_gather("small_grad_gather", _pack([small[n] for n in small_names]))
    g_sum = _sum_devices("small_grad_sum", g_all)
    g_parts = dict(zip(small_names, _unpack(g_sum, [small[n].shape for n in small_names])))
    for n in REPLICATED:
        grads[n] = g_parts[n]
    for n in SHARDED_SMALL:
        cols = w[n].shape[-1]
        grads[n] = lax.dynamic_slice_in_dim(g_parts[n], k_chip * cols, cols, axis=g_parts[n].ndim - 1)

    delta, new_m, new_v = {}, {}, {}
    for n in LARGE + SHARDED_SMALL:
        delta[n], new_m[n], new_v[n] = _adamw(f"adamw_{n}", w[n], grads[n], a["m_" + n], a["v_" + n])
    rep_shapes = [w[n].shape for n in REPLICATED]
    rep_rows = sum(_packed_rows(s) for s in rep_shapes)
    res = _adamw("adamw_replicated", _pack([w[n] for n in REPLICATED]), g_sum[:rep_rows],
                 _pack([a["m_" + n] for n in REPLICATED]), _pack([a["v_" + n] for n in REPLICATED]))
    for dst, buf in zip((delta, new_m, new_v), res):
        dst.update(zip(REPLICATED, _unpack(buf, rep_shapes)))

    return (loss, dx[None], *[grads[n] for n in WEIGHTS], *[delta[n] for n in WEIGHTS],
            *[new_m[n] for n in WEIGHTS], *[new_v[n] for n in WEIGHTS])
```

```python
import functools
import math

import jax
import jax.numpy as jnp
from jax import lax
from jax.experimental import pallas as pl
from jax.experimental.pallas import tpu as pltpu

F32 = jnp.float32
BF16 = jnp.bfloat16
MESH = pl.DeviceIdType.MESH

EPS = 1e-6
ADAM_LR = 0.001
ADAM_B1 = 0.9
ADAM_B2 = 0.999
ADAM_EPS = 1e-08
ADAM_WD = 0.01
ADAM_STEP = 10

N_CHIPS = 4
N_DEV = 8
SSM_C = 16
SSM_P = 64
GROUPS_PER_BLOCK = 8
CONV_HALO = 32
SCAN_SUB = 16
VMEM_LIMIT = 56 * 1024 * 1024


def _pick(n, pref, align):
    t = min(n, pref)
    t -= t % align
    while t >= align:
        if n % t == 0:
            return t
        t -= align
    return n


def _params(*sem):
    return pltpu.CompilerParams(dimension_semantics=sem, vmem_limit_bytes=VMEM_LIMIT)


def _sigmoid(x):
    return 1.0 / (1.0 + jnp.exp(-x))


def _fold8(z):
    r, n = z.shape
    return jnp.sum(z.reshape(r // 8, 8, n), axis=0)


def _w_col(j, kt, nps, off):
    def idx(m, n, k):
        return ((n + off) // nps, j * kt + k, (n + off) % nps)
    return idx


def _w_row(j, kps):
    def idx(m, n, k):
        return (k // kps, j * kps + k % kps, n)
    return idx


def _mm_nn(name, a, w, w_maps, extras, outs, epilogue, *, tm, tn, tk, a_square=False):
    M, K = a.shape
    N = outs[0][0][1]
    grid = (M // tm, N // tn, K // tk)
    nk, nv, ne, no = grid[2], len(w_maps), len(extras), len(outs)
    in_specs = [pl.BlockSpec((tm, tk), lambda m, n, k: (m, k))]
    args = [a]
    for wm in w_maps:
        in_specs.append(pl.BlockSpec((None, tk, tn), wm))
        args.append(w)
    for arr, kind, off in extras:
        if kind == "mn":
            in_specs.append(pl.BlockSpec((tm, tn), lambda m, n, k: (m, n)))
        else:
            in_specs.append(pl.BlockSpec((1, tn), functools.partial(lambda m, n, k, o: (0, n + o), o=off // tn)))
        args.append(arr)
    out_specs = [pl.BlockSpec((tm, tn), lambda m, n, k: (m, n)) for _ in outs]
    out_shape = [jax.ShapeDtypeStruct(s, d) for s, d in outs]

    def body(*refs):
        a_ref = refs[0]
        w_refs = refs[1:1 + nv]
        e_refs = refs[1 + nv:1 + nv + ne]
        o_refs = refs[1 + nv + ne:1 + nv + ne + no]
        acc_refs = refs[1 + nv + ne + no:]
        k = pl.program_id(2)

        @pl.when(k == 0)
        def _():
            for acc in acc_refs:
                acc[...] = jnp.zeros_like(acc)

        av = a_ref[...]
        if a_square:
            af = av.astype(F32)
            av = af * af
        av = av.astype(BF16)
        for acc, w_ref in zip(acc_refs, w_refs):
            acc[...] += jnp.dot(av, w_ref[...], preferred_element_type=F32)

        @pl.when(k == nk - 1)
        def _():
            res = epilogue([acc[...] for acc in acc_refs], [e[...] for e in e_refs])
            for o_ref, r in zip(o_refs, res):
                o_ref[...] = r.astype(o_ref.dtype)

    return pl.pallas_call(
        body, name=name, grid=grid, in_specs=in_specs, out_specs=out_specs, out_shape=out_shape,
        scratch_shapes=[pltpu.VMEM((tm, tn), F32) for _ in range(nv)],
        compiler_params=_params("parallel", "parallel", "arbitrary"),
    )(*args)


def _mm_nt(name, a, w, w_map, extras, outs, epilogue, *, tm, tr, tc):
    M, N = a.shape
    R = outs[0][0][1]
    grid = (M // tm, R // tr, N // tc)
    nc, ne, no = grid[2], len(extras), len(outs)
    in_specs = [pl.BlockSpec((tm, tc), lambda m, r, c: (m, c)), pl.BlockSpec((None, tr, tc), w_map)]
    args = [a, w]
    for arr in extras:
        in_specs.append(pl.BlockSpec((tm, tr), lambda m, r, c: (m, r)))
        args.append(arr)
    out_specs = [pl.BlockSpec((tm, tr), lambda m, r, c: (m, r)) for _ in outs]
    out_shape = [jax.ShapeDtypeStruct(s, d) for s, d in outs]

    def body(*refs):
        a_ref, w_ref = refs[0], refs[1]
        e_refs = refs[2:2 + ne]
        o_refs = refs[2 + ne:2 + ne + no]
        acc = refs[2 + ne + no]
        c = pl.program_id(2)

        @pl.when(c == 0)
        def _():
            acc[...] = jnp.zeros_like(acc)

        acc[...] += lax.dot_general(a_ref[...].astype(BF16), w_ref[...], (((1,), (1,)), ((), ())),
                                    preferred_element_type=F32)

        @pl.when(c == nc - 1)
        def _():
            res = epilogue(acc[...], [e[...] for e in e_refs])
            for o_ref, r in zip(o_refs, res):
                o_ref[...] = r.astype(o_ref.dtype)

    return pl.pallas_call(
        body, name=name, grid=grid, in_specs=in_specs, out_specs=out_specs, out_shape=out_shape,
        scratch_shapes=[pltpu.VMEM((tm, tr), F32)],
        compiler_params=_params("parallel", "parallel", "arbitrary"),
    )(*args)


def _nt_col(j, rt, cps):
    def idx(m, r, c):
        return (c // cps, j * rt + r, c % cps)
    return idx


def _nt_row(j, rps):
    def idx(m, r, c):
        return (r // rps, j * rps + r % rps, c)
    return idx


def _mm_tn(name, a, b, out_sharding, n_shards, *, tr, tc, tt, a_square=False):
    T, R = a.shape
    N = b.shape[1]
    S = n_shards
    grid = (R // tr, N // tc, T // tt)
    if out_sharding == "col":
        nps = (N // S) // tc
        out_shape = jax.ShapeDtypeStruct((S, R, N // S), F32)
        out_spec = pl.BlockSpec((None, tr, tc), lambda r, n, t: (n // nps, r, n % nps))
    else:
        rps = (R // S) // tr
        out_shape = jax.ShapeDtypeStruct((S, R // S, N), F32)
        out_spec = pl.BlockSpec((None, tr, tc), lambda r, n, t: (r // rps, r % rps, n))

    def body(a_ref, b_ref, o_ref):
        t = pl.program_id(2)

        @pl.when(t == 0)
        def _():
            o_ref[...] = jnp.zeros_like(o_ref)

        av = a_ref[...]
        if a_square:
            af = av.astype(F32)
            av = af * af
        o_ref[...] += lax.dot_general(av.astype(BF16), b_ref[...].astype(BF16), (((0,), (0,)), ((), ())),
                                      preferred_element_type=F32)

    return pl.pallas_call(
        body, name=name, grid=grid,
        in_specs=[pl.BlockSpec((tt, tr), lambda r, n, t: (t, r)), pl.BlockSpec((tt, tc), lambda r, n, t: (t, n))],
        out_specs=out_spec, out_shape=out_shape,
        compiler_params=_params("parallel", "parallel", "arbitrary"),
    )(a, b)


def _rms_fwd(name, x, g, *, tr):
    T, D = x.shape

    def body(x_ref, g_ref, h_ref):
        xv = x_ref[...]
        r = lax.rsqrt(jnp.mean(xv * xv, axis=-1, keepdims=True) + EPS)
        h_ref[...] = (xv * r * g_ref[...]).astype(BF16)

    return pl.pallas_call(
        body, name=name, grid=(T // tr,),
        in_specs=[pl.BlockSpec((tr, D), lambda i: (i, 0)), pl.BlockSpec((1, D), lambda i: (0, 0))],
        out_specs=pl.BlockSpec((tr, D), lambda i: (i, 0)),
        out_shape=jax.ShapeDtypeStruct((T, D), BF16),
        compiler_params=_params("parallel"),
    )(x, g)


def _rms_bwd_rows(xv, gv, dh):
    r = lax.rsqrt(jnp.mean(xv * xv, axis=-1, keepdims=True) + EPS)
    xh = xv * r
    gdy = dh * gv
    dx = r * (gdy - xh * jnp.mean(gdy * xh, axis=-1, keepdims=True))
    return dx, dh * xh


def _rms_bwd(name, x, g, dh, dx_in, *, tr):
    T, D = x.shape
    nt = T // tr

    def body(x_ref, g_ref, dh_ref, dxi_ref, dx_ref, dxb_ref, dg_ref, cs_ref, dg_acc, cs_acc):
        i = pl.program_id(0)

        @pl.when(i == 0)
        def _():
            dg_acc[...] = jnp.zeros_like(dg_acc)
            cs_acc[...] = jnp.zeros_like(cs_acc)

        dx, dgx = _rms_bwd_rows(x_ref[...], g_ref[...], dh_ref[...].astype(F32))
        dxo = dxi_ref[...] + dx
        dx_ref[...] = dxo
        dxb_ref[...] = dxo.astype(BF16)
        dg_acc[...] += _fold8(dgx)
        cs_acc[...] += _fold8(dxo)

        @pl.when(i == nt - 1)
        def _():
            dg_ref[...] = jnp.sum(dg_acc[...], axis=0, keepdims=True)
            cs_ref[...] = jnp.sum(cs_acc[...], axis=0, keepdims=True)

    row = pl.BlockSpec((tr, D), lambda i: (i, 0))
    vec = pl.BlockSpec((1, D), lambda i: (0, 0))
    return pl.pallas_call(
        body, name=name, grid=(nt,),
        in_specs=[row, vec, row, row], out_specs=[row, row, vec, vec],
        out_shape=[jax.ShapeDtypeStruct((T, D), F32), jax.ShapeDtypeStruct((T, D), BF16),
                   jax.ShapeDtypeStruct((1, D), F32), jax.ShapeDtypeStruct((1, D), F32)],
        scratch_shapes=[pltpu.VMEM((8, D), F32), pltpu.VMEM((8, D), F32)],
        compiler_params=_params("arbitrary"),
    )(x, g, dh, dx_in)


def _final_loss(name, x, g, target, *, tr):
    T, D = x.shape
    nt = T // tr

    def body(x_ref, g_ref, t_ref, dx_ref, dxb_ref, loss_ref, dg_ref, l_acc, dg_acc):
        i = pl.program_id(0)

        @pl.when(i == 0)
        def _():
            l_acc[...] = jnp.zeros_like(l_acc)
            dg_acc[...] = jnp.zeros_like(dg_acc)

        xv = x_ref[...]
        gv = g_ref[...]
        r = lax.rsqrt(jnp.mean(xv * xv, axis=-1, keepdims=True) + EPS)
        err = xv * r * gv - t_ref[...]
        l_acc[...] += _fold8(err * err)
        dx, dgx = _rms_bwd_rows(xv, gv, err * (1.0 / D))
        dx_ref[...] = dx
        dxb_ref[...] = dx.astype(BF16)
        dg_acc[...] += _fold8(dgx)

        @pl.when(i == nt - 1)
        def _():
            tot = jnp.sum(jnp.sum(l_acc[...], axis=0, keepdims=True), axis=1, keepdims=True)
            loss_ref[...] = jnp.broadcast_to(tot * (0.5 / D), loss_ref.shape)
            dg_ref[...] = jnp.sum(dg_acc[...], axis=0, keepdims=True)

    row = pl.BlockSpec((tr, D), lambda i: (i, 0))
    vec = pl.BlockSpec((1, D), lambda i: (0, 0))
    return pl.pallas_call(
        body, name=name, grid=(nt,),
        in_specs=[row, vec, row],
        out_specs=[row, row, pl.BlockSpec((8, 128), lambda i: (0, 0)), vec],
        out_shape=[jax.ShapeDtypeStruct((T, D), F32), jax.ShapeDtypeStruct((T, D), BF16),
                   jax.ShapeDtypeStruct((8, 128), F32), jax.ShapeDtypeStruct((1, D), F32)],
        scratch_shapes=[pltpu.VMEM((8, D), F32), pltpu.VMEM((8, D), F32)],
        compiler_params=_params("arbitrary"),
    )(x, g, target)


CONV_ROWS = 64
CONV_LANES = 128


def _tap_windows(ext_ref, r0, cols, first, width):
    last = first + width - 1
    for r in range(8):
        qs = [q for q in range(last // 8 + 1) if first <= 8 * q + r <= last]
        if not qs:
            continue
        n = CONV_ROWS + 8 * qs[-1] + (8 if r else 0)
        win = ext_ref[pl.ds(r0, n), cols]
        if r:
            win = pltpu.roll(win, n - r, 0)
        for q in qs:
            yield 8 * q + r - first, win[8 * q:8 * q + CONV_ROWS]


def _conv_blocks(tr, D):
    for cb in range(D // CONV_LANES):
        for rb in range(tr // CONV_ROWS):
            yield rb * CONV_ROWS, pl.ds(cb * CONV_LANES, CONV_LANES)


def _conv_taps(ext_ref, dwp_ref, out_ref, tr, width, first):
    for r0, cols in _conv_blocks(tr, out_ref.shape[1]):
        acc = None
        for j, win in _tap_windows(ext_ref, r0, cols, first, width):
            term = dwp_ref[pl.ds(j, 1), cols] * win
            acc = term if acc is None else acc + term
        out_ref[pl.ds(r0, CONV_ROWS), cols] = acc


def _ln_rows(v2, lg, lb):
    mu = jnp.mean(v2, axis=-1, keepdims=True)
    xc = v2 - mu
    rs = lax.rsqrt(jnp.mean(xc * xc, axis=-1, keepdims=True) + EPS)
    xh = xc * rs
    return xh, rs, xh * lg + lb


def _halo_prev(tr):
    q = tr // CONV_HALO
    return lambda i: (jnp.maximum(i * q - 1, 0), 0)


def _conv_fwd(name, v, dwp, dwb, lg, lb, *, tr, width):
    T, D = v.shape
    first = CONV_HALO - (width - 1)

    def body(v_ref, halo_ref, dwp_ref, dwb_ref, lg_ref, lb_ref, s_ref, ext, conv):
        i = pl.program_id(0)
        ext[pl.ds(0, CONV_HALO), :] = jnp.where(i == 0, 0.0, halo_ref[...])
        ext[pl.ds(CONV_HALO, tr), :] = v_ref[...]
        _conv_taps(ext, dwp_ref, conv, tr, width, first)
        _, _, ln = _ln_rows(conv[...] + dwb_ref[...], lg_ref[...], lb_ref[...])
        s_ref[...] = (ln * _sigmoid(ln)).astype(BF16)

    row = pl.BlockSpec((tr, D), lambda i: (i, 0))
    vec = pl.BlockSpec((1, D), lambda i: (0, 0))
    return pl.pallas_call(
        body, name=name, grid=(T // tr,),
        in_specs=[row, pl.BlockSpec((CONV_HALO, D), _halo_prev(tr)),
                  pl.BlockSpec((CONV_HALO, D), lambda i: (0, 0)), vec, vec, vec],
        out_specs=row, out_shape=jax.ShapeDtypeStruct((T, D), BF16),
        scratch_shapes=[pltpu.VMEM((tr + CONV_HALO, D), F32), pltpu.VMEM((tr, D), F32)],
        compiler_params=_params("parallel"),
    )(v, v, dwp, dwb, lg, lb)


def _conv_bwd_a(name, v, ds, dwp, dwb, lg, lb, *, tr, width):
    T, D = v.shape
    nt = T // tr
    first = CONV_HALO - (width - 1)

    def body(v_ref, halo_ref, ds_ref, dwp_ref, dwb_ref, lg_ref, lb_ref,
             dv2_ref, dlg_ref, dlb_ref, ddwb_ref, ext, conv, a_lg, a_lb, a_dwb):
        i = pl.program_id(0)

        @pl.when(i == 0)
        def _():
            a_lg[...] = jnp.zeros_like(a_lg)
            a_lb[...] = jnp.zeros_like(a_lb)
            a_dwb[...] = jnp.zeros_like(a_dwb)

        ext[pl.ds(0, CONV_HALO), :] = jnp.where(i == 0, 0.0, halo_ref[...])
        ext[pl.ds(CONV_HALO, tr), :] = v_ref[...]
        _conv_taps(ext, dwp_ref, conv, tr, width, first)
        v2 = conv[...] + dwb_ref[...]
        lgv = lg_ref[...]
        xh, rs, ln = _ln_rows(v2, lgv, lb_ref[...])
        sg = _sigmoid(ln)
        dln = ds_ref[...] * (sg * (1.0 + ln * (1.0 - sg)))
        a_lg[...] += _fold8(dln * xh)
        a_lb[...] += _fold8(dln)
        dxh = dln * lgv
        dv2 = rs * (dxh - jnp.mean(dxh, axis=-1, keepdims=True)
                    - xh * jnp.mean(dxh * xh, axis=-1, keepdims=True))
        dv2_ref[...] = dv2
        a_dwb[...] += _fold8(dv2)

        @pl.when(i == nt - 1)
        def _():
            dlg_ref[...] = jnp.sum(a_lg[...], axis=0, keepdims=True)
            dlb_ref[...] = jnp.sum(a_lb[...], axis=0, keepdims=True)
            ddwb_ref[...] = jnp.sum(a_dwb[...], axis=0, keepdims=True)

    row = pl.BlockSpec((tr, D), lambda i: (i, 0))
    vec = pl.BlockSpec((1, D), lambda i: (0, 0))
    return pl.pallas_call(
        body, name=name, grid=(nt,),
        in_specs=[row, pl.BlockSpec((CONV_HALO, D), _halo_prev(tr)), row,
                  pl.BlockSpec((CONV_HALO, D), lambda i: (0, 0)), vec, vec, vec],
        out_specs=[row, vec, vec, vec],
        out_shape=[jax.ShapeDtypeStruct((T, D), F32)] + [jax.ShapeDtypeStruct((1, D), F32)] * 3,
        scratch_shapes=[pltpu.VMEM((tr + CONV_HALO, D), F32), pltpu.VMEM((tr, D), F32)] + [pltpu.VMEM((8, D), F32)] * 3,
        compiler_params=_params("arbitrary"),
    )(v, v, ds, dwp, dwb, lg, lb)


def _conv_bwd_b(name, dv2, v, a, g, dwp, *, tr, width):
    T, D = v.shape
    nt = T // tr
    q = tr // CONV_HALO
    first = CONV_HALO - (width - 1)
    last_halo = T // CONV_HALO - 1

    def body(dv2_ref, nxt_ref, v_ref, halo_ref, a_ref, g_ref, dwp_ref,
             du_ref, ddw_ref, dbin_ref, ext_v, ext_d, dvs, a_dw, a_b):
        i = pl.program_id(0)

        @pl.when(i == 0)
        def _():
            a_dw[...] = jnp.zeros_like(a_dw)
            a_b[...] = jnp.zeros_like(a_b)

        ext_v[pl.ds(0, CONV_HALO), :] = jnp.where(i == 0, 0.0, halo_ref[...])
        ext_v[pl.ds(CONV_HALO, tr), :] = v_ref[...]
        ext_d[pl.ds(0, tr), :] = dv2_ref[...]
        ext_d[pl.ds(tr, CONV_HALO), :] = jnp.where(i == nt - 1, 0.0, nxt_ref[...])
        for r0, cols in _conv_blocks(tr, D):
            dv = None
            for o, win in _tap_windows(ext_d, r0, cols, 0, width):
                term = dwp_ref[pl.ds(width - 1 - o, 1), cols] * win
                dv = term if dv is None else dv + term
            dvs[pl.ds(r0, CONV_ROWS), cols] = dv
            d_cur = ext_d[pl.ds(r0, CONV_ROWS), cols]
            for j, win in _tap_windows(ext_v, r0, cols, first, width):
                a_dw[j, :, cols] += _fold8(d_cur * win)
        dv = dvs[...]
        av = a_ref[...].astype(F32)
        sg = _sigmoid(g_ref[...].astype(F32))
        da = dv * sg
        dg = dv * av * sg * (1.0 - sg)
        du_ref[:, pl.ds(0, D)] = da.astype(BF16)
        du_ref[:, pl.ds(D, D)] = dg.astype(BF16)
        a_b[:, pl.ds(0, D)] += _fold8(da)
        a_b[:, pl.ds(D, D)] += _fold8(dg)

        @pl.when(i == nt - 1)
        def _():
            ddw_ref[...] = jnp.sum(a_dw[...], axis=1)
            dbin_ref[...] = jnp.sum(a_b[...], axis=0, keepdims=True)

    row = pl.BlockSpec((tr, D), lambda i: (i, 0))
    return pl.pallas_call(
        body, name=name, grid=(nt,),
        in_specs=[row, pl.BlockSpec((CONV_HALO, D), lambda i: (jnp.minimum((i + 1) * q, last_halo), 0)),
                  row, pl.BlockSpec((CONV_HALO, D), _halo_prev(tr)), row, row,
                  pl.BlockSpec((CONV_HALO, D), lambda i: (0, 0))],
        out_specs=[pl.BlockSpec((tr, 2 * D), lambda i: (i, 0)),
                   pl.BlockSpec((CONV_HALO, D), lambda i: (0, 0)),
                   pl.BlockSpec((1, 2 * D), lambda i: (0, 0))],
        out_shape=[jax.ShapeDtypeStruct((T, 2 * D), BF16), jax.ShapeDtypeStruct((CONV_HALO, D), F32),
                   jax.ShapeDtypeStruct((1, 2 * D), F32)],
        scratch_shapes=[pltpu.VMEM((tr + CONV_HALO, D), F32), pltpu.VMEM((tr + CONV_HALO, D), F32),
                        pltpu.VMEM((tr, D), F32), pltpu.VMEM((CONV_HALO, 8, D), F32), pltpu.VMEM((8, 2 * D), F32)],
        compiler_params=_params("arbitrary"),
    )(dv2, dv2, v, v, a, g, dwp)


def _glu_bwd(name, dout, val, gate, *, tr):
    T, D = dout.shape

    def body(d_ref, v_ref, g_ref, dz_ref):
        d = d_ref[...]
        sg = _sigmoid(g_ref[...].astype(F32))
        dz_ref[:, pl.ds(0, D)] = (d * sg).astype(BF16)
        dz_ref[:, pl.ds(D, D)] = (d * v_ref[...].astype(F32) * sg * (1.0 - sg)).astype(BF16)

    row = pl.BlockSpec((tr, D), lambda i: (i, 0))
    return pl.pallas_call(
        body, name=name, grid=(T // tr,), in_specs=[row, row, row],
        out_specs=pl.BlockSpec((tr, 2 * D), lambda i: (i, 0)),
        out_shape=jax.ShapeDtypeStruct((T, 2 * D), BF16),
        compiler_params=_params("parallel"),
    )(dout, val, gate)


GELU_C = math.sqrt(2.0 / math.pi)
GELU_A = 0.044715


def _gelu(x):
    return 0.5 * x * (1.0 + jnp.tanh(GELU_C * (x + GELU_A * x * x * x)))


def _gelu_grad(x):
    t = jnp.tanh(GELU_C * (x + GELU_A * x * x * x))
    return 0.5 * (1.0 + t) + 0.5 * x * (1.0 - t * t) * GELU_C * (1.0 + 3.0 * GELU_A * x * x)


def _cmul(ar, ai, br, bi):
    return ar * br - ai * bi, ar * bi + ai * br


def _shift_rows(x, shift, down):
    n = x.shape[0]
    if shift % 8 == 0:
        z = jnp.zeros((shift, x.shape[1]), x.dtype)
        return jnp.concatenate([z, x[:n - shift]], axis=0) if down else jnp.concatenate([x[shift:], z], axis=0)
    return pltpu.roll(x, shift if down else n - shift, 0)


def _scan_chunk(xr, xi, tab_ref, cin_r, cin_i, *, lanes, reverse):
    n = xr.shape[0]
    steps, sub = tab_ref.shape[0] - 1, tab_ref.shape[1]
    to3 = lambda v: v.reshape(n // sub, sub, lanes)
    for j in range(steps):
        mr = tab_ref[j, :, pl.ds(0, lanes)]
        mi = tab_ref[j, :, pl.ds(lanes, lanes)]
        pr = to3(_shift_rows(xr, 1 << j, not reverse))
        pi = to3(_shift_rows(xi, 1 << j, not reverse))
        dr, di = _cmul(mr[None], mi[None], pr, pi)
        xr, xi = (to3(xr) + dr).reshape(n, lanes), (to3(xi) + di).reshape(n, lanes)
    pw_r = tab_ref[steps, :, pl.ds(0, lanes)]
    pw_i = tab_ref[steps, :, pl.ds(lanes, lanes)]
    blocks = range(n // sub)
    out_r, out_i = [None] * len(blocks), [None] * len(blocks)
    cr, ci = cin_r, cin_i
    for b in (reversed(blocks) if reverse else blocks):
        dr, di = _cmul(pw_r, pw_i, cr, ci)
        sr = xr[b * sub:(b + 1) * sub] + dr
        si = xi[b * sub:(b + 1) * sub] + di
        edge = 0 if reverse else sub - 1
        cr, ci = sr[edge:edge + 1], si[edge:edge + 1]
        out_r[b], out_i[b] = sr, si
    return jnp.concatenate(out_r, axis=0), jnp.concatenate(out_i, axis=0), cr, ci


def _ssm_fwd(name, h, bbd, ccd, tab, dskip, *, tc):
    T, D = h.shape
    NG, CB, L2 = bbd.shape
    lanes = L2 // 2
    nch = T // tc

    def body(h_ref, bb_ref, cc_ref, tab_ref, d_ref, y_ref, s_ref, cin_ref, carry):
        t = pl.program_id(1)

        @pl.when(t == 0)
        def _():
            carry[...] = jnp.zeros_like(carry)

        cin_ref[...] = carry[...]
        u = h_ref[...]
        bu = jnp.dot(u, bb_ref[...], preferred_element_type=F32)
        sr, si, cr, ci = _scan_chunk(bu[:, :lanes], bu[:, lanes:], tab_ref,
                                     carry[pl.ds(0, 1), pl.ds(0, lanes)], carry[pl.ds(0, 1), pl.ds(lanes, lanes)],
                                     lanes=lanes, reverse=False)
        carry[pl.ds(0, 1), pl.ds(0, lanes)] = cr
        carry[pl.ds(0, 1), pl.ds(lanes, lanes)] = ci
        s = jnp.concatenate([sr, si], axis=1).astype(BF16)
        s_ref[...] = s
        yp = jnp.dot(s, cc_ref[...], preferred_element_type=F32) + d_ref[...] * u.astype(F32)
        y_ref[...] = _gelu(yp).astype(BF16)

    return pl.pallas_call(
        body, name=name, grid=(NG, nch),
        in_specs=[pl.BlockSpec((tc, CB), lambda b, t: (t, b)),
                  pl.BlockSpec((None, CB, L2), lambda b, t: (b, 0, 0)),
                  pl.BlockSpec((None, L2, CB), lambda b, t: (b, 0, 0)),
                  pl.BlockSpec((None,) + tab.shape[1:], lambda b, t: (b, 0, 0, 0)),
                  pl.BlockSpec((1, CB), lambda b, t: (0, b))],
        out_specs=[pl.BlockSpec((tc, CB), lambda b, t: (t, b)),
                   pl.BlockSpec((tc, L2), lambda b, t: (t, b)),
                   pl.BlockSpec((None, None, 8, L2), lambda b, t: (b, t, 0, 0))],
        out_shape=[jax.ShapeDtypeStruct((T, D), BF16), jax.ShapeDtypeStruct((T, NG * L2), BF16),
                   jax.ShapeDtypeStruct((NG, nch, 8, L2), F32)],
        scratch_shapes=[pltpu.VMEM((8, L2), F32)],
        compiler_params=_params("parallel", "arbitrary"),
    )(h, bbd, ccd, tab, dskip)


def _ssm_bwd(name, h, dy, states, cins, bbd, ccd, tabr, dskip, *, tc):
    T, D = h.shape
    NG, CB, L2 = bbd.shape
    lanes = L2 // 2
    nch = T // tc

    def body(h_ref, dy_ref, s_ref, cin_ref, bb_ref, cc_ref, tabr_ref, d_ref,
             dh_ref, dbb_ref, dcc_ref, da_ref, dd_ref, gcarry, a_da, a_dd):
        t = pl.program_id(1)

        @pl.when(t == 0)
        def _():
            gcarry[...] = jnp.zeros_like(gcarry)
            a_da[...] = jnp.zeros_like(a_da)
            a_dd[...] = jnp.zeros_like(a_dd)
            dbb_ref[...] = jnp.zeros_like(dbb_ref)
            dcc_ref[...] = jnp.zeros_like(dcc_ref)

        u = h_ref[...]
        uf = u.astype(F32)
        cin_r = cin_ref[pl.ds(0, 1), pl.ds(0, lanes)]
        cin_i = cin_ref[pl.ds(0, 1), pl.ds(lanes, lanes)]
        s = s_ref[...]
        sr, si = s[:, :lanes].astype(F32), s[:, lanes:].astype(F32)
        dv = d_ref[...]
        yp = jnp.dot(s, cc_ref[...], preferred_element_type=F32) + dv * uf
        dyp = dy_ref[...] * _gelu_grad(yp)
        a_dd[...] += _fold8(dyp * uf)
        dypb = dyp.astype(BF16)
        dcc_ref[...] += lax.dot_general(s, dypb, (((0,), (0,)), ((), ())), preferred_element_type=F32)
        ds = lax.dot_general(dypb, cc_ref[...], (((1,), (1,)), ((), ())), preferred_element_type=F32)
        gr, gi, cr, ci = _scan_chunk(ds[:, :lanes], ds[:, lanes:], tabr_ref,
                                     gcarry[pl.ds(0, 1), pl.ds(0, lanes)], gcarry[pl.ds(0, 1), pl.ds(lanes, lanes)],
                                     lanes=lanes, reverse=True)
        gcarry[pl.ds(0, 1), pl.ds(0, lanes)] = cr
        gcarry[pl.ds(0, 1), pl.ds(lanes, lanes)] = ci
        gb = jnp.concatenate([gr, gi], axis=1).astype(BF16)
        du = lax.dot_general(gb, bb_ref[...], (((1,), (1,)), ((), ())), preferred_element_type=F32)
        dh_ref[...] = du + dv * dyp
        dbb_ref[...] += lax.dot_general(u, gb, (((0,), (0,)), ((), ())), preferred_element_type=F32)
        first = lax.broadcasted_iota(jnp.int32, (tc, 1), 0) == 0
        pr = jnp.where(first, cin_r, pltpu.roll(sr, 1, 0))
        pi = jnp.where(first, cin_i, pltpu.roll(si, 1, 0))
        a_da[:, pl.ds(0, lanes)] += _fold8(pr * gr + pi * gi)
        a_da[:, pl.ds(lanes, lanes)] += _fold8(pr * gi - pi * gr)

        @pl.when(t == nch - 1)
        def _():
            da_ref[...] = jnp.sum(a_da[...], axis=0, keepdims=True)
            dd_ref[...] = jnp.sum(a_dd[...], axis=0, keepdims=True)

    rev = lambda b, t: (nch - 1 - t, b)
    return pl.pallas_call(
        body, name=name, grid=(NG, nch),
        in_specs=[pl.BlockSpec((tc, CB), rev), pl.BlockSpec((tc, CB), rev), pl.BlockSpec((tc, L2), rev),
                  pl.BlockSpec((None, None, 8, L2), lambda b, t: (b, nch - 1 - t, 0, 0)),
                  pl.BlockSpec((None, CB, L2), lambda b, t: (b, 0, 0)),
                  pl.BlockSpec((None, L2, CB), lambda b, t: (b, 0, 0)),
                  pl.BlockSpec((None,) + tabr.shape[1:], lambda b, t: (b, 0, 0, 0)),
                  pl.BlockSpec((1, CB), lambda b, t: (0, b))],
        out_specs=[pl.BlockSpec((tc, CB), rev),
                   pl.BlockSpec((None, CB, L2), lambda b, t: (b, 0, 0)),
                   pl.BlockSpec((None, L2, CB), lambda b, t: (b, 0, 0)),
                   pl.BlockSpec((None, 1, L2), lambda b, t: (b, 0, 0)),
                   pl.BlockSpec((1, CB), lambda b, t: (0, b))],
        out_shape=[jax.ShapeDtypeStruct((T, D), F32), jax.ShapeDtypeStruct((NG, CB, L2), F32),
                   jax.ShapeDtypeStruct((NG, L2, CB), F32), jax.ShapeDtypeStruct((NG, 1, L2), F32),
                   jax.ShapeDtypeStruct((1, D), F32)],
        scratch_shapes=[pltpu.VMEM((8, L2), F32), pltpu.VMEM((8, L2), F32), pltpu.VMEM((8, CB), F32)],
        compiler_params=_params("parallel", "arbitrary"),
    )(h, dy, states, cins, bbd, ccd, tabr, dskip)


def _zoh(lr, li, ldt):
    dt = jnp.exp(ldt)
    mag = jnp.exp(lr * dt)
    ar = mag * jnp.cos(li * dt)
    ai = mag * jnp.sin(li * dt)
    den = lr * lr + li * li
    nr = ar - 1.0
    kr = (nr * lr + ai * li) / den
    ki = (ai * lr - nr * li) / den
    return dt, ar, ai, kr, ki, den


def _ssm_prep(name, lr, li, ldt, br, bi):
    shp = jax.ShapeDtypeStruct(lr.shape, F32)

    def body(lr_ref, li_ref, ldt_ref, br_ref, bi_ref, ar_ref, ai_ref, kr_ref, ki_ref, bbr_ref, bbi_ref):
        _, ar, ai, kr, ki, _ = _zoh(lr_ref[...], li_ref[...], ldt_ref[...])
        ar_ref[...] = ar
        ai_ref[...] = ai
        kr_ref[...] = kr
        ki_ref[...] = ki
        bbr, bbi = _cmul(kr, ki, br_ref[...], bi_ref[...])
        bbr_ref[...] = bbr
        bbi_ref[...] = bbi

    return pl.pallas_call(body, name=name, out_shape=[shp] * 6)(lr, li, ldt, br, bi)


def _ssm_powers(name, ar, ai):
    NG, _, lanes = ar.shape
    steps = SCAN_SUB.bit_length() - 1

    def body(ar_ref, ai_ref, tf_ref, tr_ref):
        a_r, a_i = ar_ref[...], ai_ref[...]
        pw = [(a_r, a_i)]
        for _ in range(7):
            pw.append(_cmul(pw[-1][0], pw[-1][1], a_r, a_i))
        pos = lax.broadcasted_iota(jnp.int32, (SCAN_SUB, lanes), 0)
        top = pw[7]
        big = {8: top}
        while max(big) * 2 < SCAN_SUB:
            m = max(big)
            big[2 * m] = _cmul(*big[m], *big[m])
        for j in range(steps):
            shift = 1 << j
            m_r, m_i = pw[shift - 1] if shift <= 8 else big[shift]
            tf_ref[j, :, pl.ds(0, lanes)] = jnp.where(pos >= shift, m_r, 0.0)
            tf_ref[j, :, pl.ds(lanes, lanes)] = jnp.where(pos >= shift, m_i, 0.0)
            tr_ref[j, :, pl.ds(0, lanes)] = jnp.where(pos < SCAN_SUB - shift, m_r, 0.0)
            tr_ref[j, :, pl.ds(lanes, lanes)] = jnp.where(pos < SCAN_SUB - shift, -m_i, 0.0)
        row = lax.broadcasted_iota(jnp.int32, (8, lanes), 0)
        fr = fi = rr = ri = jnp.zeros((8, lanes), F32)
        for n in range(8):
            fr = jnp.where(row == n, pw[n][0], fr)
            fi = jnp.where(row == n, pw[n][1], fi)
            rr = jnp.where(row == 7 - n, pw[n][0], rr)
            ri = jnp.where(row == 7 - n, pw[n][1], ri)
        top_r, top_i = top
        size = 8
        while size < SCAN_SUB:
            hr, hi = _cmul(fr, fi, top_r, top_i)
            fr, fi = jnp.concatenate([fr, hr], axis=0), jnp.concatenate([fi, hi], axis=0)
            hr, hi = _cmul(rr, ri, top_r, top_i)
            rr, ri = jnp.concatenate([hr, rr], axis=0), jnp.concatenate([hi, ri], axis=0)
            top_r, top_i = _cmul(top_r, top_i, top_r, top_i)
            size *= 2
        tf_ref[steps, :, pl.ds(0, lanes)] = fr
        tf_ref[steps, :, pl.ds(lanes, lanes)] = fi
        tr_ref[steps, :, pl.ds(0, lanes)] = rr
        tr_ref[steps, :, pl.ds(lanes, lanes)] = -ri

    vec = pl.BlockSpec((None, 1, lanes), lambda b: (b, 0, 0))
    tab = pl.BlockSpec((None, steps + 1, SCAN_SUB, 2 * lanes), lambda b: (b, 0, 0, 0))
    shp = jax.ShapeDtypeStruct((NG, steps + 1, SCAN_SUB, 2 * lanes), F32)
    return pl.pallas_call(body, name=name, grid=(NG,), in_specs=[vec, vec], out_specs=[tab, tab],
                          out_shape=[shp, shp], compiler_params=_params("parallel"))(ar, ai)


def _ssm_prep_bwd_b(name, kr, ki, br, bi, gbr, gbi):
    shp = jax.ShapeDtypeStruct(kr.shape, F32)

    def body(kr_ref, ki_ref, br_ref, bi_ref, gr_ref, gi_ref, dbr_ref, dbi_ref, tr_ref, ti_ref):
        gr, gi = gr_ref[...], gi_ref[...]
        dbr, dbi = _cmul(kr_ref[...], -ki_ref[...], gr, gi)
        dbr_ref[...] = dbr
        dbi_ref[...] = dbi
        t_r, t_i = _cmul(br_ref[...], -bi_ref[...], gr, gi)
        tr_ref[...] = t_r
        ti_ref[...] = t_i

    return pl.pallas_call(body, name=name, out_shape=[shp] * 4)(kr, ki, br, bi, gbr, gbi)


def _ssm_prep_bwd_a(name, lr, li, ldt, gar, gai, tkr, tki):
    G, P = lr.shape

    def body(lr_ref, li_ref, ldt_ref, gar_ref, gai_ref, tkr_ref, tki_ref, dlr_ref, dli_ref, dldt_ref):
        lr_v, li_v = lr_ref[...], li_ref[...]
        dt, ar, ai, kr, ki, den = _zoh(lr_v, li_v, ldt_ref[...])
        gkr = jnp.sum(tkr_ref[...], axis=0)
        gki = jnp.sum(tki_ref[...], axis=0)
        ir, ii = lr_v / den, -li_v / den
        t_r, t_i = _cmul(ir, -ii, gkr, gki)
        gar_t, gai_t = gar_ref[...] + t_r, gai_ref[...] + t_i
        qr, qi = _cmul(kr, ki, ir, ii)
        t_r, t_i = _cmul(-qr, qi, gkr, gki)
        u_r, u_i = _cmul(dt * ar, -dt * ai, gar_t, gai_t)
        dlr_ref[...] = u_r + t_r
        dli_ref[...] = u_i + t_i
        la_r, la_i = _cmul(lr_v, li_v, ar, ai)
        w_r, _ = _cmul(la_r, -la_i, gar_t, gai_t)
        ddt = jnp.sum(w_r, axis=1, keepdims=True)
        dldt_ref[...] = jnp.broadcast_to(ddt * dt[:, 0:1], dldt_ref.shape)

    shp = jax.ShapeDtypeStruct((G, P), F32)
    return pl.pallas_call(body, name=name, out_shape=[shp, shp, jax.ShapeDtypeStruct((G, 128), F32)])(
        lr, li, ldt, gar, gai, tkr, tki)


ROWS_CALL_TILE_ELEMS = 256 * 1024


def _rows_call(name, fn, ins, outs):
    R, Cn = ins[0].shape
    tr = _pick(R, max(16, ROWS_CALL_TILE_ELEMS // Cn), 16)
    spec = pl.BlockSpec((tr, Cn), lambda i: (i, 0))

    def body(*refs):
        res = fn(*[r[...] for r in refs[:len(ins)]])
        for o_ref, r in zip(refs[len(ins):], res):
            o_ref[...] = r.astype(o_ref.dtype)

    return pl.pallas_call(
        body, name=name, grid=(R // tr,), in_specs=[spec] * len(ins), out_specs=[spec] * len(outs),
        out_shape=[jax.ShapeDtypeStruct((R, Cn), d) for d in outs], compiler_params=_params("parallel"),
    )(*ins)


def _adamw_math(w, g, m, v):
    m = ADAM_B1 * m + (1.0 - ADAM_B1) * g
    v = ADAM_B2 * v + (1.0 - ADAM_B2) * (g * g)
    m_hat = m / (1.0 - ADAM_B1 ** ADAM_STEP)
    v_hat = v / (1.0 - ADAM_B2 ** ADAM_STEP)
    delta = -ADAM_LR * (m_hat / (jnp.sqrt(v_hat) + ADAM_EPS) + ADAM_WD * w)
    return delta, m, v


def _adamw(name, w, g, m, v):
    shape = w.shape
    cols = shape[-1]
    to2d = lambda t: t.reshape(-1, cols)
    res = _rows_call(name, _adamw_math, [to2d(w), to2d(g), to2d(m), to2d(v)], [F32, F32, F32])
    return [r.reshape(shape) for r in res]


ANY = pl.BlockSpec(memory_space=pl.ANY)


def _place():
    x, y, c = lax.axis_index("x"), lax.axis_index("y"), lax.axis_index("c")
    chips = [(1 - x, y), (x, 1 - y), (1 - x, 1 - y)]
    return x, y, c, chips


def _remote(src, dst, send_sem, recv_sem, dev):
    return pltpu.make_async_remote_copy(src_ref=src, dst_ref=dst, send_sem=send_sem, recv_sem=recv_sem,
                                        device_id=dev, device_id_type=MESH)


def _weight_gather(ws):
    n = len(ws)

    def body(*refs):
        w_refs, o_refs = refs[:n], refs[n:2 * n]
        send_sems, recv_sems = refs[2 * n:]
        x, y, c, chips = _place()
        me_k = 2 * x + y
        sibling = (x, y, 1 - c)
        sends = []
        for i in range(n):
            half = ws[i].shape[0] // 2
            mine = pl.ds(c * half, half)
            for r, (cx, cy) in enumerate(chips):
                cp = _remote(w_refs[i].at[mine], o_refs[i].at[me_k, mine],
                             send_sems.at[6 * i + r], recv_sems.at[6 * i + r], (cx, cy, c))
                cp.start()
                sends.append(cp)
        for i in range(n):
            half = ws[i].shape[0] // 2
            mine = pl.ds(c * half, half)
            for r, (cx, cy) in enumerate(chips):
                got = o_refs[i].at[2 * cx + cy, mine]
                _remote(got, got, send_sems.at[6 * i + r], recv_sems.at[6 * i + r], sibling).wait_recv()
                fwd = _remote(got, got, send_sems.at[6 * i + 3 + r], recv_sems.at[6 * i + 3 + r], sibling)
                fwd.start()
                sends.append(fwd)
        for i in range(n):
            half = ws[i].shape[0] // 2
            theirs = pl.ds((1 - c) * half, half)
            for r, (cx, cy) in enumerate(chips):
                got = o_refs[i].at[2 * cx + cy, theirs]
                _remote(got, got, send_sems.at[6 * i + 3 + r], recv_sems.at[6 * i + 3 + r], sibling).wait_recv()
        for cp in sends:
            cp.wait_send()

    return pl.pallas_call(
        body, name="weight_gather", in_specs=[ANY] * n, out_specs=[ANY] * n,
        out_shape=[jax.ShapeDtypeStruct((N_CHIPS,) + w.shape, w.dtype) for w in ws],
        scratch_shapes=[pltpu.SemaphoreType.DMA((6 * n,)), pltpu.SemaphoreType.DMA((6 * n,))],
    )(*ws)


def _tile_rows(rows, cols):
    return _pick(rows, max(16, ROWS_CALL_TILE_ELEMS // cols), 16)


def _place_own(name, own, stack, meta):
    R, Cn = own.shape
    tr = _tile_rows(R, Cn)

    def body(m_ref, own_ref, stack_ref, o_ref):
        o_ref[...] = own_ref[...]

    return pl.pallas_call(
        body, name=name, out_shape=jax.ShapeDtypeStruct(stack.shape, stack.dtype), input_output_aliases={2: 0},
        grid_spec=pltpu.PrefetchScalarGridSpec(
            num_scalar_prefetch=1, grid=(R // tr,),
            in_specs=[pl.BlockSpec((tr, Cn), lambda i, m: (i, 0)), ANY],
            out_specs=pl.BlockSpec((None, tr, Cn), lambda i, m: (m[1], i, 0))),
        compiler_params=_params("parallel"),
    )(meta, own, stack)


def _pair_exchange(gs):
    n = len(gs)

    def body(*refs):
        g_refs, r_refs = refs[:n], refs[n:2 * n]
        send_sems, recv_sems = refs[2 * n:]
        x, y, c, _ = _place()
        sibling = (x, y, 1 - c)
        cps = []
        for i in range(n):
            half = gs[i].shape[1] // 2
            cp = _remote(g_refs[i].at[:, pl.ds((1 - c) * half, half)], r_refs[i],
                         send_sems.at[i], recv_sems.at[i], sibling)
            cp.start()
            cps.append(cp)
        for cp in cps:
            cp.wait_recv()
        for cp in cps:
            cp.wait_send()

    return pl.pallas_call(
        body, name="grad_pair_exchange", in_specs=[ANY] * n, out_specs=[ANY] * n,
        out_shape=[jax.ShapeDtypeStruct((g.shape[0], g.shape[1] // 2, g.shape[2]), g.dtype) for g in gs],
        scratch_shapes=[pltpu.SemaphoreType.DMA((n,)), pltpu.SemaphoreType.DMA((n,))],
    )(*gs)


def _pair_sum(name, g, recv, meta):
    S, R, Cn = g.shape
    H = R // 2
    tr = _tile_rows(H, Cn)
    nh = H // tr

    def body(m_ref, g_ref, r_ref, p32_ref, p16_ref):
        v = g_ref[...] + r_ref[...]
        p32_ref[...] = v
        p16_ref[...] = v.astype(BF16)

    blk = pl.BlockSpec((None, tr, Cn), lambda s, i, m: (s, i, 0))
    return pl.pallas_call(
        body, name=name,
        out_shape=[jax.ShapeDtypeStruct((S, H, Cn), F32), jax.ShapeDtypeStruct((S, H, Cn), BF16)],
        grid_spec=pltpu.PrefetchScalarGridSpec(
            num_scalar_prefetch=1, grid=(S, nh),
            in_specs=[pl.BlockSpec((None, tr, Cn), lambda s, i, m: (s, m[0] * nh + i, 0)), blk],
            out_specs=[blk, blk]),
        compiler_params=_params("parallel", "parallel"),
    )(meta, g, recv)


def _chip_exchange(pbs):
    n = len(pbs)

    def body(*refs):
        pb_refs, rv_refs = refs[:n], refs[n:4 * n]
        send_sems, recv_sems = refs[4 * n:]
        x, y, c, chips = _place()
        cps = []
        for i in range(n):
            for r, (cx, cy) in enumerate(chips):
                cp = _remote(pb_refs[i].at[2 * cx + cy], rv_refs[3 * i + r],
                             send_sems.at[3 * i + r], recv_sems.at[3 * i + r], (cx, cy, c))
                cp.start()
                cps.append(cp)
        for cp in cps:
            cp.wait_recv()
        for cp in cps:
            cp.wait_send()

    res = pl.pallas_call(
        body, name="grad_chip_exchange", in_specs=[ANY] * n, out_specs=[ANY] * (3 * n),
        out_shape=[jax.ShapeDtypeStruct(pb.shape[1:], pb.dtype) for pb in pbs for _ in range(3)],
        scratch_shapes=[pltpu.SemaphoreType.DMA((3 * n,)), pltpu.SemaphoreType.DMA((3 * n,))],
    )(*pbs)
    return [res[3 * i:3 * i + 3] for i in range(n)]


def _chip_sum(name, p32, rvs, stack, layer, n_layers, meta):
    S, H, Cn = p32.shape
    tr = _tile_rows(H, Cn)
    nh = H // tr
    half = pl.BlockSpec((tr, Cn), lambda i, m: (i, 0))
    in_specs = [pl.BlockSpec((None, tr, Cn), lambda i, m: (m[1], i, 0)), half, half, half]
    args = [meta, p32, *rvs]
    aliases = {}
    if stack is not None:
        in_specs.append(ANY)
        args.append(stack)
        aliases = {len(args) - 1: 0}

    def body(m_ref, p_ref, r0_ref, r1_ref, r2_ref, *rest):
        rest[-1][...] = p_ref[...] + r0_ref[...].astype(F32) + r1_ref[...].astype(F32) + r2_ref[...].astype(F32)

    return pl.pallas_call(
        body, name=name, out_shape=jax.ShapeDtypeStruct((n_layers * 2 * H, Cn), F32), input_output_aliases=aliases,
        grid_spec=pltpu.PrefetchScalarGridSpec(
            num_scalar_prefetch=1, grid=(nh,), in_specs=in_specs,
            out_specs=pl.BlockSpec((tr, Cn), lambda i, m: (layer * 2 * nh + m[0] * nh + i, 0))),
        compiler_params=_params("parallel"),
    )(*args)


def _pair_gather(stacks, halves):
    ng = len(stacks)
    copies = [(g, l) for g in range(ng) for l in range(stacks[g].shape[0] // (2 * halves[g]))]
    n = len(copies)

    def body(*refs):
        o_refs = refs[ng:2 * ng]
        send_sems, recv_sems = refs[2 * ng:]
        x, y, c, _ = _place()
        sibling = (x, y, 1 - c)
        cps = []
        for k, (g, l) in enumerate(copies):
            H = halves[g]
            rows = o_refs[g].at[pl.ds(l * 2 * H + c * H, H)]
            cp = _remote(rows, rows, send_sems.at[k], recv_sems.at[k], sibling)
            cp.start()
            cps.append(cp)
        for k, (g, l) in enumerate(copies):
            H = halves[g]
            theirs = o_refs[g].at[pl.ds(l * 2 * H + (1 - c) * H, H)]
            _remote(theirs, theirs, send_sems.at[k], recv_sems.at[k], sibling).wait_recv()
        for cp in cps:
            cp.wait_send()

    return pl.pallas_call(
        body, name="grad_pair_gather", in_specs=[ANY] * ng, out_specs=[ANY] * ng,
        out_shape=[jax.ShapeDtypeStruct(s.shape, s.dtype) for s in stacks],
        input_output_aliases={g: g for g in range(ng)},
        scratch_shapes=[pltpu.SemaphoreType.DMA((n,)), pltpu.SemaphoreType.DMA((n,))],
    )(*stacks)


def _small_gather(name, v):
    def body(v_ref, o_ref, send_sems, recv_sems, loc_sem):
        x, y, c, chips = _place()
        me, sibling = (x, y, c), (x, y, 1 - c)

        def slot(px, py, pc):
            return o_ref.at[4 * px + 2 * py + pc]

        def copy(k, block, to, src=None):
            return _remote(slot(*block) if src is None else src, slot(*block), send_sems.at[k], recv_sems.at[k], to)

        mine = pltpu.make_async_copy(v_ref, slot(*me), loc_sem)
        mine.start()
        first = [copy(0, me, sibling, src=v_ref)]
        first += [copy(1 + j, me, (*chip, c), src=v_ref) for j, chip in enumerate(chips)]
        for cp in first:
            cp.start()
        passed = [copy(4 + j, (*chip, c), sibling) for j, chip in enumerate(chips)]
        for j, chip in enumerate(chips):
            copy(1 + j, (*chip, c), me).wait_recv()
            passed[j].start()
        copy(0, sibling, me).wait_recv()
        for j, chip in enumerate(chips):
            copy(4 + j, (*chip, 1 - c), me).wait_recv()
        for cp in first + passed:
            cp.wait_send()
        mine.wait()

    return pl.pallas_call(
        body, name=name, in_specs=[ANY], out_specs=ANY,
        out_shape=jax.ShapeDtypeStruct((N_DEV,) + v.shape, v.dtype),
        scratch_shapes=[pltpu.SemaphoreType.DMA((7,)), pltpu.SemaphoreType.DMA((7,)), pltpu.SemaphoreType.DMA],
    )(v)


def _sum_devices(name, g):
    n, R, Cn = g.shape
    tr = _pick(R, 512, 8)

    def body(g_ref, o_ref):
        acc = g_ref[0]
        for d in range(1, n):
            acc = acc + g_ref[d]
        o_ref[...] = acc

    return pl.pallas_call(
        body, name=name, grid=(R // tr,), in_specs=[pl.BlockSpec((n, tr, Cn), lambda i: (0, i, 0))],
        out_specs=pl.BlockSpec((tr, Cn), lambda i: (i, 0)), out_shape=jax.ShapeDtypeStruct((R, Cn), g.dtype),
        compiler_params=_params("parallel"),
    )(g)


WEIGHTS = ["mix_norm", "conv_w_in", "conv_b_in", "conv_dw", "conv_dw_b", "conv_ln_g", "conv_ln_b", "conv_w_out",
           "conv_b_out", "ssm_lambda_re", "ssm_lambda_im", "ssm_log_dt", "ssm_b_re", "ssm_b_im", "ssm_c_re",
           "ssm_c_im", "ssm_d", "ssm_w_glu", "mlp_norm", "mlp_w_up", "mlp_w_down", "final_norm"]
LARGE = ["conv_w_in", "conv_w_out", "ssm_w_glu", "mlp_w_up", "mlp_w_down"]
SHARDED_SMALL = ["conv_dw", "ssm_d"]
REPLICATED = [n for n in WEIGHTS if n not in LARGE and n not in SHARDED_SMALL]
PACK_QUANTUM = 8 * 128


def _pack(parts):
    rows = []
    for p in parts:
        f = p.reshape(-1)
        pad = (-f.shape[0]) % PACK_QUANTUM
        if pad:
            f = jnp.pad(f, (0, pad))
        rows.append(f.reshape(-1, 128))
    return jnp.concatenate(rows, axis=0)


def _packed_rows(shape):
    return -(-math.prod(shape) // PACK_QUANTUM) * 8


def _unpack(buf, shapes):
    out, r = [], 0
    for s in shapes:
        rows = _packed_rows(s)
        out.append(buf[r:r + rows].reshape(-1)[:math.prod(s)].reshape(s))
        r += rows
    return out


def _block_diag(t, pattern):
    return jnp.einsum(pattern, t, jnp.eye(GROUPS_PER_BLOCK, dtype=t.dtype))


def _local_step(xs, tgt, p, wg, S):
    T, D = xs.shape
    depth = p["mix_norm"].shape[0]
    width = p["conv_dw"].shape[1]
    G, P, C = D // SSM_C, SSM_P, SSM_C
    NG = G // GROUPS_PER_BLOCK
    lanes = GROUPS_PER_BLOCK * P
    F = wg["mlp_w_up"].shape[2] * S
    tm = _pick(T, 1024, 16)
    tmh = _pick(T, 512, 16)
    tre = _pick(T, 256, 16)
    trc = _pick(T, 128, CONV_HALO)
    tcs = _pick(T, 256, SCAN_SUB)
    row = lambda v: v.reshape(1, -1)

    def nn_col(name, a, w, j, offs, extras, outs, epi, tm_):
        K, Ns = a.shape[1], w.shape[2]
        tn, tk = _pick(Ns, 1024, 128), _pick(K, 1024, 128)
        maps = [_w_col(j, K // tk, Ns // tn, o // tn) for o in offs]
        return _mm_nn(name, a, w, maps, extras, outs, epi, tm=tm_, tn=tn, tk=tk)

    def nn_row(name, a, w, j, extras, outs, epi, a_square=False):
        Ks, N = a.shape[1] // S, w.shape[2]
        tn, tk = _pick(N, 1024, 128), _pick(Ks, 1024, 128)
        return _mm_nn(name, a, w, [_w_row(j, Ks // tk)], extras, outs, epi, tm=tm, tn=tn, tk=tk, a_square=a_square)

    def nt_col(name, a, w, j, R, extras, outs, epi):
        Cs = w.shape[2]
        tr, tc = _pick(R, 1024, 128), _pick(Cs, 1024, 128)
        return _mm_nt(name, a, w, _nt_col(j, R // tr, Cs // tc), extras, outs, epi, tm=tm, tr=tr, tc=tc)

    def nt_row(name, a, w, j, R, extras, outs, epi):
        Rs = R // S
        tr, tc = _pick(Rs, 1024, 128), _pick(w.shape[2], 1024, 128)
        return _mm_nt(name, a, w, _nt_row(j, Rs // tr), extras, outs, epi, tm=tm, tr=tr, tc=tc)

    def tn(name, a, b, sharding, a_square=False):
        R, N = a.shape[1], b.shape[1]
        if sharding == "col":
            tr, tc = _pick(R, 1024, 128), _pick(N // S, 1024, 128)
        else:
            tr, tc = _pick(R // S, 1024, 128), _pick(N, 1024, 128)
        return _mm_tn(name, a, b, sharding, S, tr=tr, tc=tc, tt=_pick(T, 1024, 128), a_square=a_square)

    ssm = []
    for j in range(p["ssm_lambda_re"].shape[0]):
        rep = lambda t: jnp.repeat(t, C, axis=0)
        lam_r, lam_i = p["ssm_lambda_re"][j], p["ssm_lambda_im"][j]
        ldt = jnp.broadcast_to(p["ssm_log_dt"][j][:, None], (G, P))
        b_r = p["ssm_b_re"][j].transpose(0, 2, 1).reshape(G * C, P)
        b_i = p["ssm_b_im"][j].transpose(0, 2, 1).reshape(G * C, P)
        ar, ai, kr, ki, bbr, bbi = _ssm_prep(f"ssm_prep_{j}", rep(lam_r), rep(lam_i), rep(ldt), b_r, b_i)
        tabf, tabr = _ssm_powers(f"ssm_powers_{j}", ar[::C].reshape(NG, 1, lanes), ai[::C].reshape(NG, 1, lanes))
        bd = lambda t: _block_diag(t.reshape(NG, GROUPS_PER_BLOCK, C, P), "bgcp,gh->bgchp").reshape(NG, 128, lanes)
        bbd = jnp.concatenate([bd(bbr), bd(bbi)], axis=2).astype(BF16)
        cd = lambda t: _block_diag(t.reshape(NG, GROUPS_PER_BLOCK, C, P), "bgcp,gh->bhpgc").reshape(NG, lanes, 128)
        ccd = jnp.concatenate([cd(p["ssm_c_re"][j]), -cd(p["ssm_c_im"][j])], axis=1).astype(BF16)
        ssm.append(dict(lam_r=lam_r, lam_i=lam_i, ldt=ldt, b_r=b_r, b_i=b_i, kr=kr, ki=ki, tabf=tabf, tabr=tabr,
                        bbd=bbd, ccd=ccd, dskip=row(p["ssm_d"][j])))

    dwp = [jnp.pad(p["conv_dw"][j], ((0, CONV_HALO - width), (0, 0))) for j in range(p["conv_dw"].shape[0])]

    saved = []
    x = xs
    for i in range(depth):
        j = i // 2
        s = dict(x_in=x)
        h = _rms_fwd(f"mix_norm_fwd_{i}", x, row(p["mix_norm"][i]), tr=tre)
        s["h"] = h
        if i % 2 == 0:
            def epi_in(accs, ex):
                a_, g_ = accs[0] + ex[0], accs[1] + ex[1]
                return [a_, g_, a_ * _sigmoid(g_)]
            b_in = row(p["conv_b_in"][j])
            a_, g_, v = nn_col(f"conv_in_{j}", h, wg["conv_w_in"], j, [0, D], [(b_in, "n", 0), (b_in, "n", D)],
                               [((T, D), BF16), ((T, D), BF16), ((T, D), F32)], epi_in, tmh)
            sl = _conv_fwd(f"conv_fwd_{j}", v, dwp[j], row(p["conv_dw_b"][j]), row(p["conv_ln_g"][j]),
                           row(p["conv_ln_b"][j]), tr=trc, width=width)
            x = nn_row(f"conv_out_{j}", sl, wg["conv_w_out"], j, [(row(p["conv_b_out"][j]), "n", 0), (x, "mn", 0)],
                       [((T, D), F32)], lambda accs, ex: [accs[0] + ex[0] + ex[1]])[0]
            s.update(a=a_, g=g_, v=v, s=sl)
        else:
            q = ssm[j]
            y, states, cins = _ssm_fwd(f"ssm_fwd_{j}", h, q["bbd"], q["ccd"], q["tabf"], q["dskip"], tc=tcs)
            def epi_glu(accs, ex):
                return [accs[0], accs[1], accs[0] * _sigmoid(accs[1]) + ex[0]]
            val, gate, x = nn_col(f"ssm_glu_{j}", y, wg["ssm_w_glu"], j, [0, D], [(x, "mn", 0)],
                                  [((T, D), BF16), ((T, D), BF16), ((T, D), F32)], epi_glu, tmh)
            s.update(y=y, states=states, cins=cins, val=val, gate=gate)
        s["x_mid"] = x
        h2 = _rms_fwd(f"mlp_norm_fwd_{i}", x, row(p["mlp_norm"][i]), tr=tre)
        r = nn_col(f"mlp_up_{i}", h2, wg["mlp_w_up"], i, [0], [], [((T, F), BF16)],
                   lambda accs, ex: [jnp.maximum(accs[0], 0.0)], tm)[0]
        x = nn_row(f"mlp_down_{i}", r, wg["mlp_w_down"], i, [(x, "mn", 0)], [((T, D), F32)],
                   lambda accs, ex: [accs[0] + ex[0]], a_square=True)[0]
        s.update(h2=h2, r=r)
        saved.append(s)

    dx, dxb, loss8, dgf = _final_loss("final_loss", x, row(p["final_norm"]), tgt, tr=tre)

    n_conv, n_ssm = p["conv_dw"].shape[0], p["ssm_d"].shape[0]
    gs = {n: [None] * p[n].shape[0] for n in WEIGHTS if n not in LARGE and n != "final_norm"}
    gs.update(conv_w_in=[None] * n_conv, conv_w_out=[None] * n_conv, ssm_w_glu=[None] * n_ssm,
              mlp_w_up=[None] * depth, mlp_w_down=[None] * depth)
    gs["final_norm"] = dgf.reshape(-1)
    for i in reversed(range(depth)):
        j = i // 2
        s = saved[i]
        dz = nt_row(f"mlp_down_dx_{i}", dxb, wg["mlp_w_down"], i, F, [s["r"]], [((T, F), BF16)],
                    lambda acc, ex: [acc * (2.0 * ex[0].astype(F32))])[0]
        gs["mlp_w_down"][i] = tn(f"mlp_down_dw_{i}", s["r"], dxb, "row", a_square=True)
        gs["mlp_w_up"][i] = tn(f"mlp_up_dw_{i}", s["h2"], dz, "col")
        dh2 = nt_col(f"mlp_up_dx_{i}", dz, wg["mlp_w_up"], i, D, [], [((T, D), F32)], lambda acc, ex: [acc])[0]
        dx, dxb, dg, cs = _rms_bwd(f"mlp_norm_bwd_{i}", s["x_mid"], row(p["mlp_norm"][i]), dh2, dx, tr=tre)
        gs["mlp_norm"][i] = dg.reshape(-1)
        if i % 2 == 0:
            gs["conv_b_out"][j] = cs.reshape(-1)
            gs["conv_w_out"][j] = tn(f"conv_out_dw_{j}", s["s"], dxb, "row")
            dsl = nt_row(f"conv_out_dx_{j}", dxb, wg["conv_w_out"], j, D, [], [((T, D), F32)],
                         lambda acc, ex: [acc])[0]
            dv2, dlg, dlb, ddwb = _conv_bwd_a(f"conv_bwd_a_{j}", s["v"], dsl, dwp[j], row(p["conv_dw_b"][j]),
                                              row(p["conv_ln_g"][j]), row(p["conv_ln_b"][j]), tr=trc, width=width)
            du, ddw, dbin = _conv_bwd_b(f"conv_bwd_b_{j}", dv2, s["v"], s["a"], s["g"], dwp[j], tr=trc, width=width)
            gs["conv_ln_g"][j], gs["conv_ln_b"][j] = dlg.reshape(-1), dlb.reshape(-1)
            gs["conv_dw_b"][j], gs["conv_dw"][j], gs["conv_b_in"][j] = ddwb.reshape(-1), ddw[:width], dbin.reshape(-1)
            gs["conv_w_in"][j] = tn(f"conv_in_dw_{j}", s["h"], du, "col")
            dh = nt_col(f"conv_in_dx_{j}", du, wg["conv_w_in"], j, D, [], [((T, D), F32)], lambda acc, ex: [acc])[0]
        else:
            q = ssm[j]
            dz2 = _glu_bwd(f"ssm_glu_bwd_{j}", dx, s["val"], s["gate"], tr=tre)
            gs["ssm_w_glu"][j] = tn(f"ssm_glu_dw_{j}", s["y"], dz2, "col")
            dy = nt_col(f"ssm_glu_dx_{j}", dz2, wg["ssm_w_glu"], j, D, [], [((T, D), F32)], lambda acc, ex: [acc])[0]
            dh, dbbd, dccd, da, dd = _ssm_bwd(f"ssm_bwd_{j}", s["h"], dy, s["states"], s["cins"], q["bbd"], q["ccd"],
                                              q["tabr"], q["dskip"], tc=tcs)
            gbb = _block_diag(dbbd.reshape(NG, GROUPS_PER_BLOCK, C, 2, GROUPS_PER_BLOCK, P), "bgcrhp,gh->rbgcp")
            dbr, dbi, tkr, tki = _ssm_prep_bwd_b(f"ssm_prep_bwd_b_{j}", q["kr"], q["ki"], q["b_r"], q["b_i"],
                                                 gbb[0].reshape(G * C, P), gbb[1].reshape(G * C, P))
            unrow = lambda t: t.reshape(G, C, P).transpose(0, 2, 1)
            gs["ssm_b_re"][j], gs["ssm_b_im"][j] = unrow(dbr), unrow(dbi)
            per_c = lambda t: t.reshape(G, C, P).transpose(1, 0, 2)
            dlr, dli, dldt = _ssm_prep_bwd_a(f"ssm_prep_bwd_a_{j}", q["lam_r"], q["lam_i"], q["ldt"],
                                             da[:, 0, :lanes].reshape(G, P), da[:, 0, lanes:].reshape(G, P),
                                             per_c(tkr), per_c(tki))
            gs["ssm_lambda_re"][j], gs["ssm_lambda_im"][j], gs["ssm_log_dt"][j] = dlr, dli, dldt[:, 0]
            gcc = _block_diag(dccd.reshape(NG, 2, GROUPS_PER_BLOCK, P, GROUPS_PER_BLOCK, C), "brhpgc,gh->rbgcp")
            gs["ssm_c_re"][j], gs["ssm_c_im"][j] = gcc[0].reshape(G, C, P), -gcc[1].reshape(G, C, P)
            gs["ssm_d"][j] = dd.reshape(-1)
        dx, dxb, dg, _ = _rms_bwd(f"mix_norm_bwd_{i}", s["x_in"], row(p["mix_norm"][i]), dh, dx, tr=tre)
        gs["mix_norm"][i] = dg.reshape(-1)

    small = {n: (gs[n] if n == "final_norm" else jnp.stack(gs[n])) for n in WEIGHTS if n not in LARGE}
    large = {n: gs[n] for n in LARGE}
    return loss8[0, 0], dx, small, large


def kernel(x, mix_norm, conv_w_in, conv_b_in, conv_dw, conv_dw_b, conv_ln_g, conv_ln_b, conv_w_out, conv_b_out, ssm_lambda_re, ssm_lambda_im, ssm_log_dt, ssm_b_re, ssm_b_im, ssm_c_re, ssm_c_im, ssm_d, ssm_w_glu, mlp_norm, mlp_w_up, mlp_w_down, final_norm, loss_target, m_mix_norm, m_conv_w_in, m_conv_b_in, m_conv_dw, m_conv_dw_b, m_conv_ln_g, m_conv_ln_b, m_conv_w_out, m_conv_b_out, m_ssm_lambda_re, m_ssm_lambda_im, m_ssm_log_dt, m_ssm_b_re, m_ssm_b_im, m_ssm_c_re, m_ssm_c_im, m_ssm_d, m_ssm_w_glu, m_mlp_norm, m_mlp_w_up, m_mlp_w_down, m_final_norm, v_mix_norm, v_conv_w_in, v_conv_b_in, v_conv_dw, v_conv_dw_b, v_conv_ln_g, v_conv_ln_b, v_conv_w_out, v_conv_b_out, v_ssm_lambda_re, v_ssm_lambda_im, v_ssm_log_dt, v_ssm_b_re, v_ssm_b_im, v_ssm_c_re, v_ssm_c_im, v_ssm_d, v_ssm_w_glu, v_mlp_norm, v_mlp_w_up, v_mlp_w_down, v_final_norm):
    a = dict(locals())
    S = N_CHIPS
    w = {n: a[n] for n in WEIGHTS}
    k_chip = 2 * lax.axis_index("x") + lax.axis_index("y")

    meta = jnp.stack([lax.axis_index("c"), k_chip]).astype(jnp.int32)
    flat = lambda t: t.reshape(-1, t.shape[-1])
    own = [flat(w[n]).astype(BF16) for n in LARGE]
    gathered = _weight_gather(own)
    wg = {n: _place_own(f"place_own_{n}", o, g, meta) for n, o, g in zip(LARGE, own, gathered)}
    sh_shapes = [w[n].shape for n in SHARDED_SMALL]
    sh_all = _small_gather("small_weight_gather", _pack([w[n] for n in SHARDED_SMALL]))
    per_chip = [_unpack(sh_all[2 * k], sh_shapes) for k in range(S)]
    p = {n: w[n] for n in REPLICATED}
    for idx, n in enumerate(SHARDED_SMALL):
        p[n] = jnp.concatenate([per_chip[k][idx] for k in range(S)], axis=-1)

    loss_local, dx, small, large = _local_step(x[0], loss_target[0], p, wg, S)
    loss = lax.psum(loss_local, ("x", "y", "c"))

    order = [(n, j) for n in LARGE for j in range(len(large[n]))]
    arrs = [large[n][j] for n, j in order]
    theirs = _pair_exchange(arrs)
    sums = [_pair_sum(f"grad_pair_sum_{idx}", g, t_, meta) for idx, (g, t_) in enumerate(zip(arrs, theirs))]
    rvs = _chip_exchange([s16 for _, s16 in sums])
    stack = {n: None for n in LARGE}
    for idx, (n, j) in enumerate(order):
        stack[n] = _chip_sum(f"grad_chip_sum_{idx}", sums[idx][0], rvs[idx], stack[n], j, len(large[n]), meta)
    stacks = _pair_gather([stack[n] for n in LARGE], [large[n][0].shape[1] // 2 for n in LARGE])
    grads = {n: st.reshape(w[n].shape) for n, st in zip(LARGE, stacks)}

    small_names = REPLICATED + SHARDED_SMALL
    g_all = _small_gather("small_grad_gather", _pack([small[n] for n in small_names]))
    g_sum = _sum_devices("small_grad_sum", g_all)
    g_parts = dict(zip(small_names, _unpack(g_sum, [small[n].shape for n in small_names])))
    for n in REPLICATED:
        grads[n] = g_parts[n]
    for n in SHARDED_SMALL:
        cols = w[n].shape[-1]
        grads[n] = lax.dynamic_slice_in_dim(g_parts[n], k_chip * cols, cols, axis=g_parts[n].ndim - 1)

    delta, new_m, new_v = {}, {}, {}
    for n in LARGE + SHARDED_SMALL:
        delta[n], new_m[n], new_v[n] = _adamw(f"adamw_{n}", w[n], grads[n], a["m_" + n], a["v_" + n])
    rep_shapes = [w[n].shape for n in REPLICATED]
    rep_rows = sum(_packed_rows(s) for s in rep_shapes)
    res = _adamw("adamw_replicated", _pack([w[n] for n in REPLICATED]), g_sum[:rep_rows],
                 _pack([a["m_" + n] for n in REPLICATED]), _pack([a["v_" + n] for n in REPLICATED]))
    for dst, buf in zip((delta, new_m, new_v), res):
        dst.update(zip(REPLICATED, _unpack(buf, rep_shapes)))

    return (loss, dx[None], *[grads[n] for n in WEIGHTS], *[delta[n] for n in WEIGHTS],
            *[new_m[n] for n in WEIGHTS], *[new_v[n] for n in WEIGHTS])
```

```python
import functools
import math

import jax
import jax.numpy as jnp
from jax import lax
from jax.experimental import pallas as pl
from jax.experimental.pallas import tpu as pltpu

F32 = jnp.float32
BF16 = jnp.bfloat16
MESH = pl.DeviceIdType.MESH

EPS = 1e-6
ADAM_LR = 0.001
ADAM_B1 = 0.9
ADAM_B2 = 0.999
ADAM_EPS = 1e-08
ADAM_WD = 0.01
ADAM_STEP = 10

N_CHIPS = 4
N_DEV = 8
SSM_C = 16
SSM_P = 64
GROUPS_PER_BLOCK = 8
CONV_HALO = 32
VMEM_LIMIT = 56 * 1024 * 1024
MM_OUT_TILE = 1024
MM_K_TILE = 2048


def _pick(n, pref, align):
    t = min(n, pref)
    t -= t % align
    while t >= align:
        if n % t == 0:
            return t
        t -= align
    return n


def _params(*sem):
    return pltpu.CompilerParams(dimension_semantics=sem, vmem_limit_bytes=VMEM_LIMIT)


def _sigmoid(x):
    return 1.0 / (1.0 + jnp.exp(-x))


def _fold8(z):
    r, n = z.shape
    return jnp.sum(z.reshape(r // 8, 8, n), axis=0)


def _w_col(j, kt, nps, off):
    def idx(m, n, k):
        return ((n + off) // nps, j * kt + k, (n + off) % nps)
    return idx


def _w_row(j, kps):
    def idx(m, n, k):
        return (k // kps, j * kps + k % kps, n)
    return idx


def _mm_nn(name, a, w, w_maps, extras, outs, epilogue, *, tm, tn, tk):
    M, K = a.shape
    N = outs[0][0][1]
    grid = (M // tm, N // tn, K // tk)
    nk, nv, ne, no = grid[2], len(w_maps), len(extras), len(outs)
    in_specs = [pl.BlockSpec((tm, tk), lambda m, n, k: (m, k))]
    args = [a]
    for wm in w_maps:
        in_specs.append(pl.BlockSpec((None, tk, tn), wm))
        args.append(w)
    for arr, kind, off in extras:
        if kind == "mn":
            in_specs.append(pl.BlockSpec((tm, tn), lambda m, n, k: (m, n)))
        else:
            in_specs.append(pl.BlockSpec((1, tn), functools.partial(lambda m, n, k, o: (0, n + o), o=off // tn)))
        args.append(arr)
    out_specs = [pl.BlockSpec((tm, tn), lambda m, n, k: (m, n)) for _ in outs]
    out_shape = [jax.ShapeDtypeStruct(s, d) for s, d in outs]

    def body(*refs):
        a_ref = refs[0]
        w_refs = refs[1:1 + nv]
        e_refs = refs[1 + nv:1 + nv + ne]
        o_refs = refs[1 + nv + ne:1 + nv + ne + no]
        acc_refs = refs[1 + nv + ne + no:]
        k = pl.program_id(2)

        if nk == 1:
            av = a_ref[...].astype(BF16)
            res = epilogue([jnp.dot(av, w_ref[...], preferred_element_type=F32) for w_ref in w_refs],
                           [e[...] for e in e_refs])
            for o_ref, r in zip(o_refs, res):
                o_ref[...] = r.astype(o_ref.dtype)
            return

        @pl.when(k == 0)
        def _():
            for acc in acc_refs:
                acc[...] = jnp.zeros_like(acc)

        av = a_ref[...].astype(BF16)
        for acc, w_ref in zip(acc_refs, w_refs):
            acc[...] += jnp.dot(av, w_ref[...], preferred_element_type=F32)

        @pl.when(k == nk - 1)
        def _():
            res = epilogue([acc[...] for acc in acc_refs], [e[...] for e in e_refs])
            for o_ref, r in zip(o_refs, res):
                o_ref[...] = r.astype(o_ref.dtype)

    return pl.pallas_call(
        body, name=name, grid=grid, in_specs=in_specs, out_specs=out_specs, out_shape=out_shape,
        scratch_shapes=[pltpu.VMEM((tm, tn), F32) for _ in range(nv if nk > 1 else 0)],
        compiler_params=_params("parallel", "parallel", "arbitrary"),
    )(*args)


def _mm_nt(name, a, w, w_map, extras, outs, epilogue, *, tm, tr, tc):
    M, N = a.shape
    R = outs[0][0][1]
    grid = (M // tm, R // tr, N // tc)
    nc, ne, no = grid[2], len(extras), len(outs)
    in_specs = [pl.BlockSpec((tm, tc), lambda m, r, c: (m, c)), pl.BlockSpec((None, tr, tc), w_map)]
    args = [a, w]
    for arr in extras:
        in_specs.append(pl.BlockSpec((tm, tr), lambda m, r, c: (m, r)))
        args.append(arr)
    out_specs = [pl.BlockSpec((tm, tr), lambda m, r, c: (m, r)) for _ in outs]
    out_shape = [jax.ShapeDtypeStruct(s, d) for s, d in outs]

    def body(*refs):
        a_ref, w_ref = refs[0], refs[1]
        e_refs = refs[2:2 + ne]
        o_refs = refs[2 + ne:2 + ne + no]
        def partial_product():
            return lax.dot_general(a_ref[...].astype(BF16), w_ref[...], (((1,), (1,)), ((), ())),
                                   preferred_element_type=F32)

        if nc == 1:
            for o_ref, r in zip(o_refs, epilogue(partial_product(), [e[...] for e in e_refs])):
                o_ref[...] = r.astype(o_ref.dtype)
            return
        acc = refs[2 + ne + no]
        c = pl.program_id(2)

        @pl.when(c == 0)
        def _():
            acc[...] = jnp.zeros_like(acc)

        acc[...] += partial_product()

        @pl.when(c == nc - 1)
        def _():
            res = epilogue(acc[...], [e[...] for e in e_refs])
            for o_ref, r in zip(o_refs, res):
                o_ref[...] = r.astype(o_ref.dtype)

    return pl.pallas_call(
        body, name=name, grid=grid, in_specs=in_specs, out_specs=out_specs, out_shape=out_shape,
        scratch_shapes=[pltpu.VMEM((tm, tr), F32)] if nc > 1 else [],
        compiler_params=_params("parallel", "parallel", "arbitrary"),
    )(*args)


def _nt_col(j, rt, cps):
    def idx(m, r, c):
        return (c // cps, j * rt + r, c % cps)
    return idx


def _nt_row(j, rps):
    def idx(m, r, c):
        return (r // rps, j * rps + r % rps, c)
    return idx


def _mm_tn(name, a, b, out_sharding, n_shards, *, tr, tc, tt):
    T, R = a.shape
    N = b.shape[1]
    S = n_shards
    grid = (R // tr, N // tc, T // tt)
    if out_sharding == "col":
        nps = (N // S) // tc
        out_shape = jax.ShapeDtypeStruct((S, R, N // S), F32)
        out_spec = pl.BlockSpec((None, tr, tc), lambda r, n, t: (n // nps, r, n % nps))
    else:
        rps = (R // S) // tr
        out_shape = jax.ShapeDtypeStruct((S, R // S, N), F32)
        out_spec = pl.BlockSpec((None, tr, tc), lambda r, n, t: (r // rps, r % rps, n))

    def body(a_ref, b_ref, o_ref):
        t = pl.program_id(2)

        @pl.when(t == 0)
        def _():
            o_ref[...] = jnp.zeros_like(o_ref)

        o_ref[...] += lax.dot_general(a_ref[...].astype(BF16), b_ref[...].astype(BF16), (((0,), (0,)), ((), ())),
                                      preferred_element_type=F32)

    return pl.pallas_call(
        body, name=name, grid=grid,
        in_specs=[pl.BlockSpec((tt, tr), lambda r, n, t: (t, r)), pl.BlockSpec((tt, tc), lambda r, n, t: (t, n))],
        out_specs=out_spec, out_shape=out_shape,
        compiler_params=_params("parallel", "parallel", "arbitrary"),
    )(a, b)


def _rms_fwd(name, x, g, *, tr):
    T, D = x.shape

    def body(x_ref, g_ref, h_ref):
        xv = x_ref[...]
        r = lax.rsqrt(jnp.mean(xv * xv, axis=-1, keepdims=True) + EPS)
        h_ref[...] = (xv * r * g_ref[...]).astype(BF16)

    return pl.pallas_call(
        body, name=name, grid=(T // tr,),
        in_specs=[pl.BlockSpec((tr, D), lambda i: (i, 0)), pl.BlockSpec((1, D), lambda i: (0, 0))],
        out_specs=pl.BlockSpec((tr, D), lambda i: (i, 0)),
        out_shape=jax.ShapeDtypeStruct((T, D), BF16),
        compiler_params=_params("parallel"),
    )(x, g)


def _rms_bwd_rows(xv, gv, dh):
    r = lax.rsqrt(jnp.mean(xv * xv, axis=-1, keepdims=True) + EPS)
    xh = xv * r
    gdy = dh * gv
    dx = r * (gdy - xh * jnp.mean(gdy * xh, axis=-1, keepdims=True))
    return dx, dh * xh


def _rms_bwd(name, x, g, dh, dx_in, *, tr):
    T, D = x.shape
    nt = T // tr

    def body(x_ref, g_ref, dh_ref, dxi_ref, dx_ref, dxb_ref, dg_ref, cs_ref, dg_acc, cs_acc):
        i = pl.program_id(0)

        @pl.when(i == 0)
        def _():
            dg_acc[...] = jnp.zeros_like(dg_acc)
            cs_acc[...] = jnp.zeros_like(cs_acc)

        dx, dgx = _rms_bwd_rows(x_ref[...], g_ref[...], dh_ref[...].astype(F32))
        dxo = dxi_ref[...] + dx
        dx_ref[...] = dxo
        dxb_ref[...] = dxo.astype(BF16)
        dg_acc[...] += _fold8(dgx)
        cs_acc[...] += _fold8(dxo)

        @pl.when(i == nt - 1)
        def _():
            dg_ref[...] = jnp.sum(dg_acc[...], axis=0, keepdims=True)
            cs_ref[...] = jnp.sum(cs_acc[...], axis=0, keepdims=True)

    row = pl.BlockSpec((tr, D), lambda i: (i, 0))
    vec = pl.BlockSpec((1, D), lambda i: (0, 0))
    return pl.pallas_call(
        body, name=name, grid=(nt,),
        in_specs=[row, vec, row, row], out_specs=[row, row, vec, vec],
        out_shape=[jax.ShapeDtypeStruct((T, D), F32), jax.ShapeDtypeStruct((T, D), BF16),
                   jax.ShapeDtypeStruct((1, D), F32), jax.ShapeDtypeStruct((1, D), F32)],
        scratch_shapes=[pltpu.VMEM((8, D), F32), pltpu.VMEM((8, D), F32)],
        compiler_params=_params("arbitrary"),
    )(x, g, dh, dx_in)


def _final_loss(name, x, g, target, *, tr):
    T, D = x.shape
    nt = T // tr

    def body(x_ref, g_ref, t_ref, dx_ref, dxb_ref, loss_ref, dg_ref, l_acc, dg_acc):
        i = pl.program_id(0)

        @pl.when(i == 0)
        def _():
            l_acc[...] = jnp.zeros_like(l_acc)
            dg_acc[...] = jnp.zeros_like(dg_acc)

        xv = x_ref[...]
        gv = g_ref[...]
        r = lax.rsqrt(jnp.mean(xv * xv, axis=-1, keepdims=True) + EPS)
        err = xv * r * gv - t_ref[...]
        l_acc[...] += _fold8(err * err)
        dx, dgx = _rms_bwd_rows(xv, gv, err * (1.0 / D))
        dx_ref[...] = dx
        dxb_ref[...] = dx.astype(BF16)
        dg_acc[...] += _fold8(dgx)

        @pl.when(i == nt - 1)
        def _():
            tot = jnp.sum(jnp.sum(l_acc[...], axis=0, keepdims=True), axis=1, keepdims=True)
            loss_ref[...] = jnp.broadcast_to(tot * (0.5 / D), loss_ref.shape)
            dg_ref[...] = jnp.sum(dg_acc[...], axis=0, keepdims=True)

    row = pl.BlockSpec((tr, D), lambda i: (i, 0))
    vec = pl.BlockSpec((1, D), lambda i: (0, 0))
    return pl.pallas_call(
        body, name=name, grid=(nt,),
        in_specs=[row, vec, row],
        out_specs=[row, row, pl.BlockSpec((8, 128), lambda i: (0, 0)), vec],
        out_shape=[jax.ShapeDtypeStruct((T, D), F32), jax.ShapeDtypeStruct((T, D), BF16),
                   jax.ShapeDtypeStruct((8, 128), F32), jax.ShapeDtypeStruct((1, D), F32)],
        scratch_shapes=[pltpu.VMEM((8, D), F32), pltpu.VMEM((8, D), F32)],
        compiler_params=_params("arbitrary"),
    )(x, g, target)


CONV_ROWS = 64
CONV_LANES = 128


def _tap_windows(ext_ref, r0, cols, first, width):
    last = first + width - 1
    for r in range(8):
        qs = [q for q in range(last // 8 + 1) if first <= 8 * q + r <= last]
        if not qs:
            continue
        n = CONV_ROWS + 8 * qs[-1] + (8 if r else 0)
        win = ext_ref[pl.ds(r0, n), cols]
        if r:
            win = pltpu.roll(win, n - r, 0)
        for q in qs:
            yield 8 * q + r - first, win[8 * q:8 * q + CONV_ROWS]


def _conv_blocks(tr, D):
    for cb in range(D // CONV_LANES):
        for rb in range(tr // CONV_ROWS):
            yield rb * CONV_ROWS, pl.ds(cb * CONV_LANES, CONV_LANES)


def _conv_taps(ext_ref, dwp_ref, out_ref, tr, width, first):
    for r0, cols in _conv_blocks(tr, out_ref.shape[1]):
        acc = None
        for j, win in _tap_windows(ext_ref, r0, cols, first, width):
            term = dwp_ref[pl.ds(j, 1), cols] * win
            acc = term if acc is None else acc + term
        out_ref[pl.ds(r0, CONV_ROWS), cols] = acc


def _ln_rows(v2, lg, lb):
    mu = jnp.mean(v2, axis=-1, keepdims=True)
    xc = v2 - mu
    rs = lax.rsqrt(jnp.mean(xc * xc, axis=-1, keepdims=True) + EPS)
    xh = xc * rs
    return xh, rs, xh * lg + lb


def _halo_prev(tr):
    q = tr // CONV_HALO
    return lambda i: (jnp.maximum(i * q - 1, 0), 0)


def _conv_fwd(name, v, dwp, dwb, lg, lb, *, tr, width):
    T, D = v.shape
    first = CONV_HALO - (width - 1)

    def body(v_ref, halo_ref, dwp_ref, dwb_ref, lg_ref, lb_ref, s_ref, ext, conv):
        i = pl.program_id(0)
        ext[pl.ds(0, CONV_HALO), :] = jnp.where(i == 0, 0.0, halo_ref[...])
        ext[pl.ds(CONV_HALO, tr), :] = v_ref[...]
        _conv_taps(ext, dwp_ref, conv, tr, width, first)
        _, _, ln = _ln_rows(conv[...] + dwb_ref[...], lg_ref[...], lb_ref[...])
        s_ref[...] = (ln * _sigmoid(ln)).astype(BF16)

    row = pl.BlockSpec((tr, D), lambda i: (i, 0))
    vec = pl.BlockSpec((1, D), lambda i: (0, 0))
    return pl.pallas_call(
        body, name=name, grid=(T // tr,),
        in_specs=[row, pl.BlockSpec((CONV_HALO, D), _halo_prev(tr)),
                  pl.BlockSpec((CONV_HALO, D), lambda i: (0, 0)), vec, vec, vec],
        out_specs=row, out_shape=jax.ShapeDtypeStruct((T, D), BF16),
        scratch_shapes=[pltpu.VMEM((tr + CONV_HALO, D), F32), pltpu.VMEM((tr, D), F32)],
        compiler_params=_params("parallel"),
    )(v, v, dwp, dwb, lg, lb)


def _conv_bwd_a(name, v, ds, dwp, dwb, lg, lb, *, tr, width):
    T, D = v.shape
    nt = T // tr
    first = CONV_HALO - (width - 1)

    def body(v_ref, halo_ref, ds_ref, dwp_ref, dwb_ref, lg_ref, lb_ref,
             dv2_ref, dlg_ref, dlb_ref, ddwb_ref, ext, conv, a_lg, a_lb, a_dwb):
        i = pl.program_id(0)

        @pl.when(i == 0)
        def _():
            a_lg[...] = jnp.zeros_like(a_lg)
            a_lb[...] = jnp.zeros_like(a_lb)
            a_dwb[...] = jnp.zeros_like(a_dwb)

        ext[pl.ds(0, CONV_HALO), :] = jnp.where(i == 0, 0.0, halo_ref[...])
        ext[pl.ds(CONV_HALO, tr), :] = v_ref[...]
        _conv_taps(ext, dwp_ref, conv, tr, width, first)
        v2 = conv[...] + dwb_ref[...]
        lgv = lg_ref[...]
        xh, rs, ln = _ln_rows(v2, lgv, lb_ref[...])
        sg = _sigmoid(ln)
        dln = ds_ref[...] * (sg * (1.0 + ln * (1.0 - sg)))
        a_lg[...] += _fold8(dln * xh)
        a_lb[...] += _fold8(dln)
        dxh = dln * lgv
        dv2 = rs * (dxh - jnp.mean(dxh, axis=-1, keepdims=True)
                    - xh * jnp.mean(dxh * xh, axis=-1, keepdims=True))
        dv2_ref[...] = dv2
        a_dwb[...] += _fold8(dv2)

        @pl.when(i == nt - 1)
        def _():
            dlg_ref[...] = jnp.sum(a_lg[...], axis=0, keepdims=True)
            dlb_ref[...] = jnp.sum(a_lb[...], axis=0, keepdims=True)
            ddwb_ref[...] = jnp.sum(a_dwb[...], axis=0, keepdims=True)

    row = pl.BlockSpec((tr, D), lambda i: (i, 0))
    vec = pl.BlockSpec((1, D), lambda i: (0, 0))
    return pl.pallas_call(
        body, name=name, grid=(nt,),
        in_specs=[row, pl.BlockSpec((CONV_HALO, D), _halo_prev(tr)), row,
                  pl.BlockSpec((CONV_HALO, D), lambda i: (0, 0)), vec, vec, vec],
        out_specs=[row, vec, vec, vec],
        out_shape=[jax.ShapeDtypeStruct((T, D), F32)] + [jax.ShapeDtypeStruct((1, D), F32)] * 3,
        scratch_shapes=[pltpu.VMEM((tr + CONV_HALO, D), F32), pltpu.VMEM((tr, D), F32)] + [pltpu.VMEM((8, D), F32)] * 3,
        compiler_params=_params("arbitrary"),
    )(v, v, ds, dwp, dwb, lg, lb)


def _conv_bwd_b(name, dv2, v, a, g, dwp, *, tr, width):
    T, D = v.shape
    nt = T // tr
    q = tr // CONV_HALO
    first = CONV_HALO - (width - 1)
    last_halo = T // CONV_HALO - 1

    def body(dv2_ref, nxt_ref, v_ref, halo_ref, a_ref, g_ref, dwp_ref,
             du_ref, ddw_ref, dbin_ref, ext_v, ext_d, dvs, a_dw, a_b):
        i = pl.program_id(0)

        @pl.when(i == 0)
        def _():
            a_dw[...] = jnp.zeros_like(a_dw)
            a_b[...] = jnp.zeros_like(a_b)

        ext_v[pl.ds(0, CONV_HALO), :] = jnp.where(i == 0, 0.0, halo_ref[...])
        ext_v[pl.ds(CONV_HALO, tr), :] = v_ref[...]
        ext_d[pl.ds(0, tr), :] = dv2_ref[...]
        ext_d[pl.ds(tr, CONV_HALO), :] = jnp.where(i == nt - 1, 0.0, nxt_ref[...])
        for r0, cols in _conv_blocks(tr, D):
            dv = None
            for o, win in _tap_windows(ext_d, r0, cols, 0, width):
                term = dwp_ref[pl.ds(width - 1 - o, 1), cols] * win
                dv = term if dv is None else dv + term
            dvs[pl.ds(r0, CONV_ROWS), cols] = dv
            d_cur = ext_d[pl.ds(r0, CONV_ROWS), cols]
            for j, win in _tap_windows(ext_v, r0, cols, first, width):
                a_dw[j, :, cols] += _fold8(d_cur * win)
        dv = dvs[...]
        av = a_ref[...].astype(F32)
        sg = _sigmoid(g_ref[...].astype(F32))
        da = dv * sg
        dg = dv * av * sg * (1.0 - sg)
        du_ref[:, pl.ds(0, D)] = da.astype(BF16)
        du_ref[:, pl.ds(D, D)] = dg.astype(BF16)
        a_b[:, pl.ds(0, D)] += _fold8(da)
        a_b[:, pl.ds(D, D)] += _fold8(dg)

        @pl.when(i == nt - 1)
        def _():
            ddw_ref[...] = jnp.sum(a_dw[...], axis=1)
            dbin_ref[...] = jnp.sum(a_b[...], axis=0, keepdims=True)

    row = pl.BlockSpec((tr, D), lambda i: (i, 0))
    return pl.pallas_call(
        body, name=name, grid=(nt,),
        in_specs=[row, pl.BlockSpec((CONV_HALO, D), lambda i: (jnp.minimum((i + 1) * q, last_halo), 0)),
                  row, pl.BlockSpec((CONV_HALO, D), _halo_prev(tr)), row, row,
                  pl.BlockSpec((CONV_HALO, D), lambda i: (0, 0))],
        out_specs=[pl.BlockSpec((tr, 2 * D), lambda i: (i, 0)),
                   pl.BlockSpec((CONV_HALO, D), lambda i: (0, 0)),
                   pl.BlockSpec((1, 2 * D), lambda i: (0, 0))],
        out_shape=[jax.ShapeDtypeStruct((T, 2 * D), BF16), jax.ShapeDtypeStruct((CONV_HALO, D), F32),
                   jax.ShapeDtypeStruct((1, 2 * D), F32)],
        scratch_shapes=[pltpu.VMEM((tr + CONV_HALO, D), F32), pltpu.VMEM((tr + CONV_HALO, D), F32),
                        pltpu.VMEM((tr, D), F32), pltpu.VMEM((CONV_HALO, 8, D), F32), pltpu.VMEM((8, 2 * D), F32)],
        compiler_params=_params("arbitrary"),
    )(dv2, dv2, v, v, a, g, dwp)


def _glu_bwd(name, dout, val, gate, *, tr):
    T, D = dout.shape

    def body(d_ref, v_ref, g_ref, dz_ref):
        d = d_ref[...]
        sg = _sigmoid(g_ref[...].astype(F32))
        dz_ref[:, pl.ds(0, D)] = (d * sg).astype(BF16)
        dz_ref[:, pl.ds(D, D)] = (d * v_ref[...].astype(F32) * sg * (1.0 - sg)).astype(BF16)

    row = pl.BlockSpec((tr, D), lambda i: (i, 0))
    return pl.pallas_call(
        body, name=name, grid=(T // tr,), in_specs=[row, row, row],
        out_specs=pl.BlockSpec((tr, 2 * D), lambda i: (i, 0)),
        out_shape=jax.ShapeDtypeStruct((T, 2 * D), BF16),
        compiler_params=_params("parallel"),
    )(dout, val, gate)


GELU_C = math.sqrt(2.0 / math.pi)
GELU_A = 0.044715


def _gelu(x):
    return 0.5 * x * (1.0 + jnp.tanh(GELU_C * (x + GELU_A * x * x * x)))


def _gelu_grad(x):
    t = jnp.tanh(GELU_C * (x + GELU_A * x * x * x))
    return 0.5 * (1.0 + t) + 0.5 * x * (1.0 - t * t) * GELU_C * (1.0 + 3.0 * GELU_A * x * x)


def _cmul(ar, ai, br, bi):
    return ar * br - ai * bi, ar * bi + ai * br


SCAN_SEQS = 8


def _interleave_rows(x, tmp_ref):
    n = x.shape[0]
    tmp_ref[...] = x
    return jnp.concatenate([tmp_ref[pl.ds(i, SCAN_SEQS, stride=n // SCAN_SEQS), :] for i in range(n // SCAN_SEQS)],
                           axis=0)


def _deinterleave_rows(x, tmp_ref):
    n = x.shape[0]
    tmp_ref[...] = x
    return jnp.concatenate([tmp_ref[pl.ds(s, n // SCAN_SEQS, stride=SCAN_SEQS), :] for s in range(SCAN_SEQS)], axis=0)


def _scan_chunk(xr, xi, tab_ref, cin_r, cin_i, *, lanes, reverse):
    n = xr.shape[0]
    L = n // SCAN_SEQS
    re, im = pl.ds(0, lanes), pl.ds(lanes, lanes)
    one, top = (L - 1, 0) if reverse else (0, L - 1)
    a_r = jnp.broadcast_to(tab_ref[pl.ds(one, 1), re], (SCAN_SEQS, lanes))
    a_i = jnp.broadcast_to(tab_ref[pl.ds(one, 1), im], (SCAN_SEQS, lanes))
    sr = si = jnp.zeros((SCAN_SEQS, lanes), F32)
    loc_r, loc_i = [None] * L, [None] * L
    for i in (reversed(range(L)) if reverse else range(L)):
        rows = slice(SCAN_SEQS * i, SCAN_SEQS * (i + 1))
        sr, si = a_r * sr - a_i * si + xr[rows], a_r * si + a_i * sr + xi[rows]
        loc_r[i], loc_i[i] = sr, si
    top_r, top_i = tab_ref[pl.ds(top, 1), re], tab_ref[pl.ds(top, 1), im]
    sub = lax.broadcasted_iota(jnp.int32, (SCAN_SEQS, lanes), 0)
    cr, ci = cin_r, cin_i
    in_r = in_i = jnp.zeros((SCAN_SEQS, lanes), F32)
    for s in (reversed(range(SCAN_SEQS)) if reverse else range(SCAN_SEQS)):
        in_r, in_i = jnp.where(sub == s, cr, in_r), jnp.where(sub == s, ci, in_i)
        dr, di = _cmul(top_r, top_i, cr, ci)
        cr, ci = dr + sr[s:s + 1], di + si[s:s + 1]
    out_r, out_i = [None] * L, [None] * L
    for i in range(L):
        dr, di = _cmul(tab_ref[pl.ds(i, 1), re], tab_ref[pl.ds(i, 1), im], in_r, in_i)
        out_r[i], out_i[i] = loc_r[i] + dr, loc_i[i] + di
    return jnp.concatenate(out_r, axis=0), jnp.concatenate(out_i, axis=0), cr, ci


def _ssm_fwd(name, h, bbd, ccd, tab, dskip, *, tc):
    T, D = h.shape
    NG, CB, L2 = bbd.shape
    lanes = L2 // 2
    nch = T // tc

    def body(h_ref, bb_ref, cc_ref, tab_ref, d_ref, y_ref, s_ref, cin_ref, carry, tmp):
        t = pl.program_id(1)

        @pl.when(t == 0)
        def _():
            carry[...] = jnp.zeros_like(carry)

        cin_ref[...] = carry[...]
        uf = _interleave_rows(h_ref[...].astype(F32), tmp)
        bu = jnp.dot(uf.astype(BF16), bb_ref[...], preferred_element_type=F32)
        sr, si, cr, ci = _scan_chunk(bu[:, :lanes], bu[:, lanes:], tab_ref,
                                     carry[pl.ds(0, 1), pl.ds(0, lanes)], carry[pl.ds(0, 1), pl.ds(lanes, lanes)],
                                     lanes=lanes, reverse=False)
        carry[pl.ds(0, 1), pl.ds(0, lanes)] = cr
        carry[pl.ds(0, 1), pl.ds(lanes, lanes)] = ci
        s = jnp.concatenate([sr, si], axis=1).astype(BF16)
        s_ref[...] = s
        yp = jnp.dot(s, cc_ref[...], preferred_element_type=F32) + d_ref[...] * uf
        y_ref[...] = _deinterleave_rows(_gelu(yp), tmp).astype(BF16)

    return pl.pallas_call(
        body, name=name, grid=(NG, nch),
        in_specs=[pl.BlockSpec((tc, CB), lambda b, t: (t, b)),
                  pl.BlockSpec((None, CB, L2), lambda b, t: (b, 0, 0)),
                  pl.BlockSpec((None, L2, CB), lambda b, t: (b, 0, 0)),
                  pl.BlockSpec((None,) + tab.shape[1:], lambda b, t: (b, 0, 0)),
                  pl.BlockSpec((1, CB), lambda b, t: (0, b))],
        out_specs=[pl.BlockSpec((tc, CB), lambda b, t: (t, b)),
                   pl.BlockSpec((tc, L2), lambda b, t: (t, b)),
                   pl.BlockSpec((None, None, 8, L2), lambda b, t: (b, t, 0, 0))],
        out_shape=[jax.ShapeDtypeStruct((T, D), BF16), jax.ShapeDtypeStruct((T, NG * L2), BF16),
                   jax.ShapeDtypeStruct((NG, nch, 8, L2), F32)],
        scratch_shapes=[pltpu.VMEM((8, L2), F32), pltpu.VMEM((tc, CB), F32)],
        compiler_params=_params("parallel", "arbitrary"),
    )(h, bbd, ccd, tab, dskip)


def _ssm_bwd(name, h, dy, states, cins, bbd, ccd, tabr, dskip, *, tc):
    T, D = h.shape
    NG, CB, L2 = bbd.shape
    lanes = L2 // 2
    nch = T // tc

    def body(h_ref, dy_ref, s_ref, cin_ref, bb_ref, cc_ref, tabr_ref, d_ref,
             dh_ref, dbb_ref, dcc_ref, da_ref, dd_ref, gcarry, a_da, a_dd, tmp):
        t = pl.program_id(1)

        @pl.when(t == 0)
        def _():
            gcarry[...] = jnp.zeros_like(gcarry)
            a_da[...] = jnp.zeros_like(a_da)
            a_dd[...] = jnp.zeros_like(a_dd)
            dbb_ref[...] = jnp.zeros_like(dbb_ref)
            dcc_ref[...] = jnp.zeros_like(dcc_ref)

        uf = _interleave_rows(h_ref[...].astype(F32), tmp)
        u = uf.astype(BF16)
        dyv = _interleave_rows(dy_ref[...], tmp)
        cin_r = cin_ref[pl.ds(0, 1), pl.ds(0, lanes)]
        cin_i = cin_ref[pl.ds(0, 1), pl.ds(lanes, lanes)]
        s = s_ref[...]
        dv = d_ref[...]
        yp = jnp.dot(s, cc_ref[...], preferred_element_type=F32) + dv * uf
        dyp = dyv * _gelu_grad(yp)
        a_dd[...] += _fold8(dyp * uf)
        dypb = dyp.astype(BF16)
        dcc_ref[...] += lax.dot_general(s, dypb, (((0,), (0,)), ((), ())), preferred_element_type=F32)
        ds = lax.dot_general(dypb, cc_ref[...], (((1,), (1,)), ((), ())), preferred_element_type=F32)
        gr, gi, cr, ci = _scan_chunk(ds[:, :lanes], ds[:, lanes:], tabr_ref,
                                     gcarry[pl.ds(0, 1), pl.ds(0, lanes)], gcarry[pl.ds(0, 1), pl.ds(lanes, lanes)],
                                     lanes=lanes, reverse=True)
        gcarry[pl.ds(0, 1), pl.ds(0, lanes)] = cr
        gcarry[pl.ds(0, 1), pl.ds(lanes, lanes)] = ci
        gb = jnp.concatenate([gr, gi], axis=1).astype(BF16)
        du = lax.dot_general(gb, bb_ref[...], (((1,), (1,)), ((), ())), preferred_element_type=F32)
        dh_ref[...] = _deinterleave_rows(du + dv * dyp, tmp)
        dbb_ref[...] += lax.dot_general(u, gb, (((0,), (0,)), ((), ())), preferred_element_type=F32)
        sr, si = s[:, :lanes].astype(F32), s[:, lanes:].astype(F32)
        first = lax.broadcasted_iota(jnp.int32, (SCAN_SEQS, lanes), 0) == 0
        head_r = jnp.where(first, cin_r, pltpu.roll(sr[tc - SCAN_SEQS:], 1, 0))
        head_i = jnp.where(first, cin_i, pltpu.roll(si[tc - SCAN_SEQS:], 1, 0))
        pr = jnp.concatenate([head_r, sr[:tc - SCAN_SEQS]], axis=0)
        pi = jnp.concatenate([head_i, si[:tc - SCAN_SEQS]], axis=0)
        a_da[:, pl.ds(0, lanes)] += _fold8(pr * gr + pi * gi)
        a_da[:, pl.ds(lanes, lanes)] += _fold8(pr * gi - pi * gr)

        @pl.when(t == nch - 1)
        def _():
            da_ref[...] = jnp.sum(a_da[...], axis=0, keepdims=True)
            dd_ref[...] = jnp.sum(a_dd[...], axis=0, keepdims=True)

    rev = lambda b, t: (nch - 1 - t, b)
    return pl.pallas_call(
        body, name=name, grid=(NG, nch),
        in_specs=[pl.BlockSpec((tc, CB), rev), pl.BlockSpec((tc, CB), rev), pl.BlockSpec((tc, L2), rev),
                  pl.BlockSpec((None, None, 8, L2), lambda b, t: (b, nch - 1 - t, 0, 0)),
                  pl.BlockSpec((None, CB, L2), lambda b, t: (b, 0, 0)),
                  pl.BlockSpec((None, L2, CB), lambda b, t: (b, 0, 0)),
                  pl.BlockSpec((None,) + tabr.shape[1:], lambda b, t: (b, 0, 0)),
                  pl.BlockSpec((1, CB), lambda b, t: (0, b))],
        out_specs=[pl.BlockSpec((tc, CB), rev),
                   pl.BlockSpec((None, CB, L2), lambda b, t: (b, 0, 0)),
                   pl.BlockSpec((None, L2, CB), lambda b, t: (b, 0, 0)),
                   pl.BlockSpec((None, 1, L2), lambda b, t: (b, 0, 0)),
                   pl.BlockSpec((1, CB), lambda b, t: (0, b))],
        out_shape=[jax.ShapeDtypeStruct((T, D), F32), jax.ShapeDtypeStruct((NG, CB, L2), F32),
                   jax.ShapeDtypeStruct((NG, L2, CB), F32), jax.ShapeDtypeStruct((NG, 1, L2), F32),
                   jax.ShapeDtypeStruct((1, D), F32)],
        scratch_shapes=[pltpu.VMEM((8, L2), F32), pltpu.VMEM((8, L2), F32), pltpu.VMEM((8, CB), F32),
                        pltpu.VMEM((tc, CB), F32)],
        compiler_params=_params("parallel", "arbitrary"),
    )(h, dy, states, cins, bbd, ccd, tabr, dskip)


def _zoh(lr, li, ldt):
    dt = jnp.exp(ldt)
    mag = jnp.exp(lr * dt)
    ar = mag * jnp.cos(li * dt)
    ai = mag * jnp.sin(li * dt)
    den = lr * lr + li * li
    nr = ar - 1.0
    kr = (nr * lr + ai * li) / den
    ki = (ai * lr - nr * li) / den
    return dt, ar, ai, kr, ki, den


def _ssm_prep(name, lr, li, ldt, br, bi):
    shp = jax.ShapeDtypeStruct(lr.shape, F32)

    def body(lr_ref, li_ref, ldt_ref, br_ref, bi_ref, ar_ref, ai_ref, kr_ref, ki_ref, bbr_ref, bbi_ref):
        _, ar, ai, kr, ki, _ = _zoh(lr_ref[...], li_ref[...], ldt_ref[...])
        ar_ref[...] = ar
        ai_ref[...] = ai
        kr_ref[...] = kr
        ki_ref[...] = ki
        bbr, bbi = _cmul(kr, ki, br_ref[...], bi_ref[...])
        bbr_ref[...] = bbr
        bbi_ref[...] = bbi

    return pl.pallas_call(body, name=name, out_shape=[shp] * 6)(lr, li, ldt, br, bi)


def _ssm_powers(name, ar, ai, rows):
    NG, _, lanes = ar.shape

    def body(ar_ref, ai_ref, tf_ref, tr_ref):
        a_r, a_i = ar_ref[...], ai_ref[...]
        pw = [(a_r, a_i)]
        for _ in range(7):
            pw.append(_cmul(pw[-1][0], pw[-1][1], a_r, a_i))
        row = lax.broadcasted_iota(jnp.int32, (8, lanes), 0)
        fr = fi = rr = ri = jnp.zeros((8, lanes), F32)
        for n in range(8):
            fr = jnp.where(row == n, pw[n][0], fr)
            fi = jnp.where(row == n, pw[n][1], fi)
            rr = jnp.where(row == 7 - n, pw[n][0], rr)
            ri = jnp.where(row == 7 - n, pw[n][1], ri)
        top_r, top_i = pw[7]
        size = 8
        while size < rows:
            hr, hi = _cmul(fr, fi, top_r, top_i)
            fr, fi = jnp.concatenate([fr, hr], axis=0), jnp.concatenate([fi, hi], axis=0)
            hr, hi = _cmul(rr, ri, top_r, top_i)
            rr, ri = jnp.concatenate([hr, rr], axis=0), jnp.concatenate([hi, ri], axis=0)
            top_r, top_i = _cmul(top_r, top_i, top_r, top_i)
            size *= 2
        tf_ref[:, pl.ds(0, lanes)] = fr
        tf_ref[:, pl.ds(lanes, lanes)] = fi
        tr_ref[:, pl.ds(0, lanes)] = rr
        tr_ref[:, pl.ds(lanes, lanes)] = -ri

    vec = pl.BlockSpec((None, 1, lanes), lambda b: (b, 0, 0))
    tab = pl.BlockSpec((None, rows, 2 * lanes), lambda b: (b, 0, 0))
    shp = jax.ShapeDtypeStruct((NG, rows, 2 * lanes), F32)
    return pl.pallas_call(body, name=name, grid=(NG,), in_specs=[vec, vec], out_specs=[tab, tab],
                          out_shape=[shp, shp], compiler_params=_params("parallel"))(ar, ai)


def _ssm_prep_bwd_b(name, kr, ki, br, bi, gbr, gbi):
    shp = jax.ShapeDtypeStruct(kr.shape, F32)

    def body(kr_ref, ki_ref, br_ref, bi_ref, gr_ref, gi_ref, dbr_ref, dbi_ref, tr_ref, ti_ref):
        gr, gi = gr_ref[...], gi_ref[...]
        dbr, dbi = _cmul(kr_ref[...], -ki_ref[...], gr, gi)
        dbr_ref[...] = dbr
        dbi_ref[...] = dbi
        t_r, t_i = _cmul(br_ref[...], -bi_ref[...], gr, gi)
        tr_ref[...] = t_r
        ti_ref[...] = t_i

    return pl.pallas_call(body, name=name, out_shape=[shp] * 4)(kr, ki, br, bi, gbr, gbi)


def _ssm_prep_bwd_a(name, lr, li, ldt, gar, gai, tkr, tki):
    G, P = lr.shape

    def body(lr_ref, li_ref, ldt_ref, gar_ref, gai_ref, tkr_ref, tki_ref, dlr_ref, dli_ref, dldt_ref):
        lr_v, li_v = lr_ref[...], li_ref[...]
        dt, ar, ai, kr, ki, den = _zoh(lr_v, li_v, ldt_ref[...])
        gkr = jnp.sum(tkr_ref[...], axis=0)
        gki = jnp.sum(tki_ref[...], axis=0)
        ir, ii = lr_v / den, -li_v / den
        t_r, t_i = _cmul(ir, -ii, gkr, gki)
        gar_t, gai_t = gar_ref[...] + t_r, gai_ref[...] + t_i
        qr, qi = _cmul(kr, ki, ir, ii)
        t_r, t_i = _cmul(-qr, qi, gkr, gki)
        u_r, u_i = _cmul(dt * ar, -dt * ai, gar_t, gai_t)
        dlr_ref[...] = u_r + t_r
        dli_ref[...] = u_i + t_i
        la_r, la_i = _cmul(lr_v, li_v, ar, ai)
        w_r, _ = _cmul(la_r, -la_i, gar_t, gai_t)
        ddt = jnp.sum(w_r, axis=1, keepdims=True)
        dldt_ref[...] = jnp.broadcast_to(ddt * dt[:, 0:1], dldt_ref.shape)

    shp = jax.ShapeDtypeStruct((G, P), F32)
    return pl.pallas_call(body, name=name, out_shape=[shp, shp, jax.ShapeDtypeStruct((G, 128), F32)])(
        lr, li, ldt, gar, gai, tkr, tki)


ROWS_CALL_TILE_ELEMS = 256 * 1024


def _rows_call(name, fn, ins, outs):
    R, Cn = ins[0].shape
    tr = _pick(R, max(16, ROWS_CALL_TILE_ELEMS // Cn), 16)
    spec = pl.BlockSpec((tr, Cn), lambda i: (i, 0))

    def body(*refs):
        res = fn(*[r[...] for r in refs[:len(ins)]])
        for o_ref, r in zip(refs[len(ins):], res):
            o_ref[...] = r.astype(o_ref.dtype)

    return pl.pallas_call(
        body, name=name, grid=(R // tr,), in_specs=[spec] * len(ins), out_specs=[spec] * len(outs),
        out_shape=[jax.ShapeDtypeStruct((R, Cn), d) for d in outs], compiler_params=_params("parallel"),
    )(*ins)


def _adamw_math(w, g, m, v):
    m = ADAM_B1 * m + (1.0 - ADAM_B1) * g
    v = ADAM_B2 * v + (1.0 - ADAM_B2) * (g * g)
    m_hat = m / (1.0 - ADAM_B1 ** ADAM_STEP)
    v_hat = v / (1.0 - ADAM_B2 ** ADAM_STEP)
    delta = -ADAM_LR * (m_hat / (jnp.sqrt(v_hat) + ADAM_EPS) + ADAM_WD * w)
    return delta, m, v


def _adamw(name, w, g, m, v):
    shape = w.shape
    cols = shape[-1]
    to2d = lambda t: t.reshape(-1, cols)
    res = _rows_call(name, _adamw_math, [to2d(w), to2d(g), to2d(m), to2d(v)], [F32, F32, F32])
    return [r.reshape(shape) for r in res]


ANY = pl.BlockSpec(memory_space=pl.ANY)


def _place():
    x, y, c = lax.axis_index("x"), lax.axis_index("y"), lax.axis_index("c")
    chips = [(1 - x, y), (x, 1 - y), (1 - x, 1 - y)]
    return x, y, c, chips


def _remote(src, dst, send_sem, recv_sem, dev):
    return pltpu.make_async_remote_copy(src_ref=src, dst_ref=dst, send_sem=send_sem, recv_sem=recv_sem,
                                        device_id=dev, device_id_type=MESH)


def _weight_gather(ws):
    n = len(ws)

    def body(*refs):
        w_refs, o_refs = refs[:n], refs[n:2 * n]
        send_sems, recv_sems = refs[2 * n:]
        x, y, c, chips = _place()
        me_k = 2 * x + y
        sibling = (x, y, 1 - c)
        sends = []
        for i in range(n):
            half = ws[i].shape[0] // 2
            mine = pl.ds(c * half, half)
            for r, (cx, cy) in enumerate(chips):
                cp = _remote(w_refs[i].at[mine], o_refs[i].at[me_k, mine],
                             send_sems.at[6 * i + r], recv_sems.at[6 * i + r], (cx, cy, c))
                cp.start()
                sends.append(cp)
        for i in range(n):
            half = ws[i].shape[0] // 2
            mine = pl.ds(c * half, half)
            for r, (cx, cy) in enumerate(chips):
                got = o_refs[i].at[2 * cx + cy, mine]
                _remote(got, got, send_sems.at[6 * i + r], recv_sems.at[6 * i + r], sibling).wait_recv()
                fwd = _remote(got, got, send_sems.at[6 * i + 3 + r], recv_sems.at[6 * i + 3 + r], sibling)
                fwd.start()
                sends.append(fwd)
        for i in range(n):
            half = ws[i].shape[0] // 2
            theirs = pl.ds((1 - c) * half, half)
            for r, (cx, cy) in enumerate(chips):
                got = o_refs[i].at[2 * cx + cy, theirs]
                _remote(got, got, send_sems.at[6 * i + 3 + r], recv_sems.at[6 * i + 3 + r], sibling).wait_recv()
        for cp in sends:
            cp.wait_send()

    return pl.pallas_call(
        body, name="weight_gather", in_specs=[ANY] * n, out_specs=[ANY] * n,
        out_shape=[jax.ShapeDtypeStruct((N_CHIPS,) + w.shape, w.dtype) for w in ws],
        scratch_shapes=[pltpu.SemaphoreType.DMA((6 * n,)), pltpu.SemaphoreType.DMA((6 * n,))],
    )(*ws)


def _tile_rows(rows, cols):
    return _pick(rows, max(16, ROWS_CALL_TILE_ELEMS // cols), 16)


def _place_own(name, own, stack, meta):
    R, Cn = own.shape
    tr = _tile_rows(R, Cn)

    def body(m_ref, own_ref, stack_ref, o_ref):
        o_ref[...] = own_ref[...]

    return pl.pallas_call(
        body, name=name, out_shape=jax.ShapeDtypeStruct(stack.shape, stack.dtype), input_output_aliases={2: 0},
        grid_spec=pltpu.PrefetchScalarGridSpec(
            num_scalar_prefetch=1, grid=(R // tr,),
            in_specs=[pl.BlockSpec((tr, Cn), lambda i, m: (i, 0)), ANY],
            out_specs=pl.BlockSpec((None, tr, Cn), lambda i, m: (m[1], i, 0))),
        compiler_params=_params("parallel"),
    )(meta, own, stack)


def _pair_exchange(gs):
    n = len(gs)

    def body(*refs):
        g_refs, r_refs = refs[:n], refs[n:2 * n]
        send_sems, recv_sems = refs[2 * n:]
        x, y, c, _ = _place()
        sibling = (x, y, 1 - c)
        cps = []
        for i in range(n):
            half = gs[i].shape[1] // 2
            cp = _remote(g_refs[i].at[:, pl.ds((1 - c) * half, half)], r_refs[i],
                         send_sems.at[i], recv_sems.at[i], sibling)
            cp.start()
            cps.append(cp)
        for cp in cps:
            cp.wait_recv()
        for cp in cps:
            cp.wait_send()

    return pl.pallas_call(
        body, name="grad_pair_exchange", in_specs=[ANY] * n, out_specs=[ANY] * n,
        out_shape=[jax.ShapeDtypeStruct((g.shape[0], g.shape[1] // 2, g.shape[2]), g.dtype) for g in gs],
        scratch_shapes=[pltpu.SemaphoreType.DMA((n,)), pltpu.SemaphoreType.DMA((n,))],
    )(*gs)


def _pair_sum(name, g, recv, meta):
    S, R, Cn = g.shape
    H = R // 2
    tr = _tile_rows(H, Cn)
    nh = H // tr

    def body(m_ref, g_ref, r_ref, p32_ref, p16_ref):
        v = g_ref[...] + r_ref[...]
        p32_ref[...] = v
        p16_ref[...] = v.astype(BF16)

    blk = pl.BlockSpec((None, tr, Cn), lambda s, i, m: (s, i, 0))
    return pl.pallas_call(
        body, name=name,
        out_shape=[jax.ShapeDtypeStruct((S, H, Cn), F32), jax.ShapeDtypeStruct((S, H, Cn), BF16)],
        grid_spec=pltpu.PrefetchScalarGridSpec(
            num_scalar_prefetch=1, grid=(S, nh),
            in_specs=[pl.BlockSpec((None, tr, Cn), lambda s, i, m: (s, m[0] * nh + i, 0)), blk],
            out_specs=[blk, blk]),
        compiler_params=_params("parallel", "parallel"),
    )(meta, g, recv)


def _chip_exchange(pbs):
    n = len(pbs)

    def body(*refs):
        pb_refs, rv_refs = refs[:n], refs[n:4 * n]
        send_sems, recv_sems = refs[4 * n:]
        x, y, c, chips = _place()
        cps = []
        for i in range(n):
            for r, (cx, cy) in enumerate(chips):
                cp = _remote(pb_refs[i].at[2 * cx + cy], rv_refs[3 * i + r],
                             send_sems.at[3 * i + r], recv_sems.at[3 * i + r], (cx, cy, c))
                cp.start()
                cps.append(cp)
        for cp in cps:
            cp.wait_recv()
        for cp in cps:
            cp.wait_send()

    res = pl.pallas_call(
        body, name="grad_chip_exchange", in_specs=[ANY] * n, out_specs=[ANY] * (3 * n),
        out_shape=[jax.ShapeDtypeStruct(pb.shape[1:], pb.dtype) for pb in pbs for _ in range(3)],
        scratch_shapes=[pltpu.SemaphoreType.DMA((3 * n,)), pltpu.SemaphoreType.DMA((3 * n,))],
    )(*pbs)
    return [res[3 * i:3 * i + 3] for i in range(n)]


def _chip_sum(name, p32, rvs, stack, layer, n_layers, meta):
    S, H, Cn = p32.shape
    tr = _tile_rows(H, Cn)
    nh = H // tr
    half = pl.BlockSpec((tr, Cn), lambda i, m: (i, 0))
    in_specs = [pl.BlockSpec((None, tr, Cn), lambda i, m: (m[1], i, 0)), half, half, half]
    args = [meta, p32, *rvs]
    aliases = {}
    if stack is not None:
        in_specs.append(ANY)
        args.append(stack)
        aliases = {len(args) - 1: 0}

    def body(m_ref, p_ref, r0_ref, r1_ref, r2_ref, *rest):
        rest[-1][...] = p_ref[...] + r0_ref[...].astype(F32) + r1_ref[...].astype(F32) + r2_ref[...].astype(F32)

    return pl.pallas_call(
        body, name=name, out_shape=jax.ShapeDtypeStruct((n_layers * 2 * H, Cn), F32), input_output_aliases=aliases,
        grid_spec=pltpu.PrefetchScalarGridSpec(
            num_scalar_prefetch=1, grid=(nh,), in_specs=in_specs,
            out_specs=pl.BlockSpec((tr, Cn), lambda i, m: (layer * 2 * nh + m[0] * nh + i, 0))),
        compiler_params=_params("parallel"),
    )(*args)


def _pair_gather(stacks, halves):
    ng = len(stacks)
    copies = [(g, l) for g in range(ng) for l in range(stacks[g].shape[0] // (2 * halves[g]))]
    n = len(copies)

    def body(*refs):
        o_refs = refs[ng:2 * ng]
        send_sems, recv_sems = refs[2 * ng:]
        x, y, c, _ = _place()
        sibling = (x, y, 1 - c)
        cps = []
        for k, (g, l) in enumerate(copies):
            H = halves[g]
            rows = o_refs[g].at[pl.ds(l * 2 * H + c * H, H)]
            cp = _remote(rows, rows, send_sems.at[k], recv_sems.at[k], sibling)
            cp.start()
            cps.append(cp)
        for k, (g, l) in enumerate(copies):
            H = halves[g]
            theirs = o_refs[g].at[pl.ds(l * 2 * H + (1 - c) * H, H)]
            _remote(theirs, theirs, send_sems.at[k], recv_sems.at[k], sibling).wait_recv()
        for cp in cps:
            cp.wait_send()

    return pl.pallas_call(
        body, name="grad_pair_gather", in_specs=[ANY] * ng, out_specs=[ANY] * ng,
        out_shape=[jax.ShapeDtypeStruct(s.shape, s.dtype) for s in stacks],
        input_output_aliases={g: g for g in range(ng)},
        scratch_shapes=[pltpu.SemaphoreType.DMA((n,)), pltpu.SemaphoreType.DMA((n,))],
    )(*stacks)


def _small_gather(name, v):
    def body(v_ref, o_ref, send_sems, recv_sems, loc_sem):
        x, y, c, chips = _place()
        me, sibling = (x, y, c), (x, y, 1 - c)

        def slot(px, py, pc):
            return o_ref.at[4 * px + 2 * py + pc]

        def copy(k, block, to, src=None):
            return _remote(slot(*block) if src is None else src, slot(*block), send_sems.at[k], recv_sems.at[k], to)

        mine = pltpu.make_async_copy(v_ref, slot(*me), loc_sem)
        mine.start()
        first = [copy(0, me, sibling, src=v_ref)]
        first += [copy(1 + j, me, (*chip, c), src=v_ref) for j, chip in enumerate(chips)]
        for cp in first:
            cp.start()
        passed = [copy(4 + j, (*chip, c), sibling) for j, chip in enumerate(chips)]
        for j, chip in enumerate(chips):
            copy(1 + j, (*chip, c), me).wait_recv()
            passed[j].start()
        copy(0, sibling, me).wait_recv()
        for j, chip in enumerate(chips):
            copy(4 + j, (*chip, 1 - c), me).wait_recv()
        for cp in first + passed:
            cp.wait_send()
        mine.wait()

    return pl.pallas_call(
        body, name=name, in_specs=[ANY], out_specs=ANY,
        out_shape=jax.ShapeDtypeStruct((N_DEV,) + v.shape, v.dtype),
        scratch_shapes=[pltpu.SemaphoreType.DMA((7,)), pltpu.SemaphoreType.DMA((7,)), pltpu.SemaphoreType.DMA],
    )(v)


def _sum_devices(name, g):
    n, R, Cn = g.shape
    tr = _pick(R, 512, 8)

    def body(g_ref, o_ref):
        acc = g_ref[0]
        for d in range(1, n):
            acc = acc + g_ref[d]
        o_ref[...] = acc

    return pl.pallas_call(
        body, name=name, grid=(R // tr,), in_specs=[pl.BlockSpec((n, tr, Cn), lambda i: (0, i, 0))],
        out_specs=pl.BlockSpec((tr, Cn), lambda i: (i, 0)), out_shape=jax.ShapeDtypeStruct((R, Cn), g.dtype),
        compiler_params=_params("parallel"),
    )(g)


WEIGHTS = ["mix_norm", "conv_w_in", "conv_b_in", "conv_dw", "conv_dw_b", "conv_ln_g", "conv_ln_b", "conv_w_out",
           "conv_b_out", "ssm_lambda_re", "ssm_lambda_im", "ssm_log_dt", "ssm_b_re", "ssm_b_im", "ssm_c_re",
           "ssm_c_im", "ssm_d", "ssm_w_glu", "mlp_norm", "mlp_w_up", "mlp_w_down", "final_norm"]
LARGE = ["conv_w_in", "conv_w_out", "ssm_w_glu", "mlp_w_up", "mlp_w_down"]
SHARDED_SMALL = ["conv_dw", "ssm_d"]
REPLICATED = [n for n in WEIGHTS if n not in LARGE and n not in SHARDED_SMALL]
PACK_QUANTUM = 8 * 128


def _pack(parts):
    rows = []
    for p in parts:
        f = p.reshape(-1)
        pad = (-f.shape[0]) % PACK_QUANTUM
        if pad:
            f = jnp.pad(f, (0, pad))
        rows.append(f.reshape(-1, 128))
    return jnp.concatenate(rows, axis=0)


def _packed_rows(shape):
    return -(-math.prod(shape) // PACK_QUANTUM) * 8


def _unpack(buf, shapes):
    out, r = [], 0
    for s in shapes:
        rows = _packed_rows(s)
        out.append(buf[r:r + rows].reshape(-1)[:math.prod(s)].reshape(s))
        r += rows
    return out


def _block_diag(t, pattern):
    return jnp.einsum(pattern, t, jnp.eye(GROUPS_PER_BLOCK, dtype=t.dtype))


def _local_step(xs, tgt, p, wg, S):
    T, D = xs.shape
    depth = p["mix_norm"].shape[0]
    width = p["conv_dw"].shape[1]
    G, P, C = D // SSM_C, SSM_P, SSM_C
    NG = G // GROUPS_PER_BLOCK
    lanes = GROUPS_PER_BLOCK * P
    F = wg["mlp_w_up"].shape[2] * S
    tm = _pick(T, 1024, 16)
    tmh = _pick(T, 512, 16)
    tre = _pick(T, 256, 16)
    trc = _pick(T, 128, CONV_HALO)
    tcs = _pick(T, 256, 8 * SCAN_SEQS)
    row = lambda v: v.reshape(1, -1)

    def nn_col(name, a, w, j, offs, extras, outs, epi, tm_):
        K, Ns = a.shape[1], w.shape[2]
        tn, tk = _pick(Ns, MM_OUT_TILE, 128), _pick(K, MM_K_TILE, 128)
        maps = [_w_col(j, K // tk, Ns // tn, o // tn) for o in offs]
        return _mm_nn(name, a, w, maps, extras, outs, epi, tm=tm_, tn=tn, tk=tk)

    def nn_row(name, a, w, j, extras, outs, epi):
        Ks, N = a.shape[1] // S, w.shape[2]
        tn, tk = _pick(N, MM_OUT_TILE, 128), _pick(Ks, MM_K_TILE, 128)
        return _mm_nn(name, a, w, [_w_row(j, Ks // tk)], extras, outs, epi, tm=tm, tn=tn, tk=tk)

    def nt_col(name, a, w, j, R, extras, outs, epi):
        Cs = w.shape[2]
        tr, tc = _pick(R, MM_OUT_TILE, 128), _pick(Cs, MM_K_TILE, 128)
        return _mm_nt(name, a, w, _nt_col(j, R // tr, Cs // tc), extras, outs, epi, tm=tm, tr=tr, tc=tc)

    def nt_row(name, a, w, j, R, extras, outs, epi):
        Rs = R // S
        tr, tc = _pick(Rs, MM_OUT_TILE, 128), _pick(w.shape[2], MM_K_TILE, 128)
        return _mm_nt(name, a, w, _nt_row(j, Rs // tr), extras, outs, epi, tm=tm, tr=tr, tc=tc)

    def tn(name, a, b, sharding):
        R, N = a.shape[1], b.shape[1]
        if sharding == "col":
            tr, tc = _pick(R, MM_OUT_TILE, 128), _pick(N // S, MM_OUT_TILE, 128)
        else:
            tr, tc = _pick(R // S, MM_OUT_TILE, 128), _pick(N, MM_OUT_TILE, 128)
        return _mm_tn(name, a, b, sharding, S, tr=tr, tc=tc, tt=_pick(T, MM_K_TILE, 128))

    ssm = []
    for j in range(p["ssm_lambda_re"].shape[0]):
        rep = lambda t: jnp.repeat(t, C, axis=0)
        lam_r, lam_i = p["ssm_lambda_re"][j], p["ssm_lambda_im"][j]
        ldt = jnp.broadcast_to(p["ssm_log_dt"][j][:, None], (G, P))
        b_r = p["ssm_b_re"][j].transpose(0, 2, 1).reshape(G * C, P)
        b_i = p["ssm_b_im"][j].transpose(0, 2, 1).reshape(G * C, P)
        ar, ai, kr, ki, bbr, bbi = _ssm_prep(f"ssm_prep_{j}", rep(lam_r), rep(lam_i), rep(ldt), b_r, b_i)
        tabf, tabr = _ssm_powers(f"ssm_powers_{j}", ar[::C].reshape(NG, 1, lanes), ai[::C].reshape(NG, 1, lanes),
                                 tcs // SCAN_SEQS)
        bd = lambda t: _block_diag(t.reshape(NG, GROUPS_PER_BLOCK, C, P), "bgcp,gh->bgchp").reshape(NG, 128, lanes)
        bbd = jnp.concatenate([bd(bbr), bd(bbi)], axis=2).astype(BF16)
        cd = lambda t: _block_diag(t.reshape(NG, GROUPS_PER_BLOCK, C, P), "bgcp,gh->bhpgc").reshape(NG, lanes, 128)
        ccd = jnp.concatenate([cd(p["ssm_c_re"][j]), -cd(p["ssm_c_im"][j])], axis=1).astype(BF16)
        ssm.append(dict(lam_r=lam_r, lam_i=lam_i, ldt=ldt, b_r=b_r, b_i=b_i, kr=kr, ki=ki, tabf=tabf, tabr=tabr,
                        bbd=bbd, ccd=ccd, dskip=row(p["ssm_d"][j])))

    dwp = [jnp.pad(p["conv_dw"][j], ((0, CONV_HALO - width), (0, 0))) for j in range(p["conv_dw"].shape[0])]

    saved = []
    x = xs
    for i in range(depth):
        j = i // 2
        s = dict(x_in=x)
        h = _rms_fwd(f"mix_norm_fwd_{i}", x, row(p["mix_norm"][i]), tr=tre)
        s["h"] = h
        if i % 2 == 0:
            def epi_in(accs, ex):
                a_, g_ = accs[0] + ex[0], accs[1] + ex[1]
                return [a_, g_, a_ * _sigmoid(g_)]
            b_in = row(p["conv_b_in"][j])
            a_, g_, v = nn_col(f"conv_in_{j}", h, wg["conv_w_in"], j, [0, D], [(b_in, "n", 0), (b_in, "n", D)],
                               [((T, D), BF16), ((T, D), BF16), ((T, D), F32)], epi_in, tmh)
            sl = _conv_fwd(f"conv_fwd_{j}", v, dwp[j], row(p["conv_dw_b"][j]), row(p["conv_ln_g"][j]),
                           row(p["conv_ln_b"][j]), tr=trc, width=width)
            x = nn_row(f"conv_out_{j}", sl, wg["conv_w_out"], j, [(row(p["conv_b_out"][j]), "n", 0), (x, "mn", 0)],
                       [((T, D), F32)], lambda accs, ex: [accs[0] + ex[0] + ex[1]])[0]
            s.update(a=a_, g=g_, v=v, s=sl)
        else:
            q = ssm[j]
            y, states, cins = _ssm_fwd(f"ssm_fwd_{j}", h, q["bbd"], q["ccd"], q["tabf"], q["dskip"], tc=tcs)
            def epi_glu(accs, ex):
                return [accs[0], accs[1], accs[0] * _sigmoid(accs[1]) + ex[0]]
            val, gate, x = nn_col(f"ssm_glu_{j}", y, wg["ssm_w_glu"], j, [0, D], [(x, "mn", 0)],
                                  [((T, D), BF16), ((T, D), BF16), ((T, D), F32)], epi_glu, tmh)
            s.update(y=y, states=states, cins=cins, val=val, gate=gate)
        s["x_mid"] = x
        h2 = _rms_fwd(f"mlp_norm_fwd_{i}", x, row(p["mlp_norm"][i]), tr=tre)
        def epi_up(accs, ex):
            r_ = jnp.maximum(accs[0], 0.0)
            return [r_, r_ * r_]
        r, rsq = nn_col(f"mlp_up_{i}", h2, wg["mlp_w_up"], i, [0], [], [((T, F), BF16), ((T, F), BF16)], epi_up, tm)
        x = nn_row(f"mlp_down_{i}", rsq, wg["mlp_w_down"], i, [(x, "mn", 0)], [((T, D), F32)],
                   lambda accs, ex: [accs[0] + ex[0]])[0]
        s.update(h2=h2, r=r, rsq=rsq)
        saved.append(s)

    dx, dxb, loss8, dgf = _final_loss("final_loss", x, row(p["final_norm"]), tgt, tr=tre)

    n_conv, n_ssm = p["conv_dw"].shape[0], p["ssm_d"].shape[0]
    gs = {n: [None] * p[n].shape[0] for n in WEIGHTS if n not in LARGE and n != "final_norm"}
    gs.update(conv_w_in=[None] * n_conv, conv_w_out=[None] * n_conv, ssm_w_glu=[None] * n_ssm,
              mlp_w_up=[None] * depth, mlp_w_down=[None] * depth)
    gs["final_norm"] = dgf.reshape(-1)
    for i in reversed(range(depth)):
        j = i // 2
        s = saved[i]
        dz = nt_row(f"mlp_down_dx_{i}", dxb, wg["mlp_w_down"], i, F, [s["r"]], [((T, F), BF16)],
                    lambda acc, ex: [acc * (2.0 * ex[0].astype(F32))])[0]
        gs["mlp_w_down"][i] = tn(f"mlp_down_dw_{i}", s["rsq"], dxb, "row")
        gs["mlp_w_up"][i] = tn(f"mlp_up_dw_{i}", s["h2"], dz, "col")
        dh2 = nt_col(f"mlp_up_dx_{i}", dz, wg["mlp_w_up"], i, D, [], [((T, D), F32)], lambda acc, ex: [acc])[0]
        dx, dxb, dg, cs = _rms_bwd(f"mlp_norm_bwd_{i}", s["x_mid"], row(p["mlp_norm"][i]), dh2, dx, tr=tre)
        gs["mlp_norm"][i] = dg.reshape(-1)
        if i % 2 == 0:
            gs["conv_b_out"][j] = cs.reshape(-1)
            gs["conv_w_out"][j] = tn(f"conv_out_dw_{j}", s["s"], dxb, "row")
            dsl = nt_row(f"conv_out_dx_{j}", dxb, wg["conv_w_out"], j, D, [], [((T, D), F32)],
                         lambda acc, ex: [acc])[0]
            dv2, dlg, dlb, ddwb = _conv_bwd_a(f"conv_bwd_a_{j}", s["v"], dsl, dwp[j], row(p["conv_dw_b"][j]),
                                              row(p["conv_ln_g"][j]), row(p["conv_ln_b"][j]), tr=trc, width=width)
            du, ddw, dbin = _conv_bwd_b(f"conv_bwd_b_{j}", dv2, s["v"], s["a"], s["g"], dwp[j], tr=trc, width=width)
            gs["conv_ln_g"][j], gs["conv_ln_b"][j] = dlg.reshape(-1), dlb.reshape(-1)
            gs["conv_dw_b"][j], gs["conv_dw"][j], gs["conv_b_in"][j] = ddwb.reshape(-1), ddw[:width], dbin.reshape(-1)
            gs["conv_w_in"][j] = tn(f"conv_in_dw_{j}", s["h"], du, "col")
            dh = nt_col(f"conv_in_dx_{j}", du, wg["conv_w_in"], j, D, [], [((T, D), F32)], lambda acc, ex: [acc])[0]
        else:
            q = ssm[j]
            dz2 = _glu_bwd(f"ssm_glu_bwd_{j}", dx, s["val"], s["gate"], tr=tre)
            gs["ssm_w_glu"][j] = tn(f"ssm_glu_dw_{j}", s["y"], dz2, "col")
            dy = nt_col(f"ssm_glu_dx_{j}", dz2, wg["ssm_w_glu"], j, D, [], [((T, D), F32)], lambda acc, ex: [acc])[0]
            dh, dbbd, dccd, da, dd = _ssm_bwd(f"ssm_bwd_{j}", s["h"], dy, s["states"], s["cins"], q["bbd"], q["ccd"],
                                              q["tabr"], q["dskip"], tc=tcs)
            gbb = _block_diag(dbbd.reshape(NG, GROUPS_PER_BLOCK, C, 2, GROUPS_PER_BLOCK, P), "bgcrhp,gh->rbgcp")
            dbr, dbi, tkr, tki = _ssm_prep_bwd_b(f"ssm_prep_bwd_b_{j}", q["kr"], q["ki"], q["b_r"], q["b_i"],
                                                 gbb[0].reshape(G * C, P), gbb[1].reshape(G * C, P))
            unrow = lambda t: t.reshape(G, C, P).transpose(0, 2, 1)
            gs["ssm_b_re"][j], gs["ssm_b_im"][j] = unrow(dbr), unrow(dbi)
            per_c = lambda t: t.reshape(G, C, P).transpose(1, 0, 2)
            dlr, dli, dldt = _ssm_prep_bwd_a(f"ssm_prep_bwd_a_{j}", q["lam_r"], q["lam_i"], q["ldt"],
                                             da[:, 0, :lanes].reshape(G, P), da[:, 0, lanes:].reshape(G, P),
                                             per_c(tkr), per_c(tki))
            gs["ssm_lambda_re"][j], gs["ssm_lambda_im"][j], gs["ssm_log_dt"][j] = dlr, dli, dldt[:, 0]
            gcc = _block_diag(dccd.reshape(NG, 2, GROUPS_PER_BLOCK, P, GROUPS_PER_BLOCK, C), "brhpgc,gh->rbgcp")
            gs["ssm_c_re"][j], gs["ssm_c_im"][j] = gcc[0].reshape(G, C, P), -gcc[1].reshape(G, C, P)
            gs["ssm_d"][j] = dd.reshape(-1)
        dx, dxb, dg, _ = _rms_bwd(f"mix_norm_bwd_{i}", s["x_in"], row(p["mix_norm"][i]), dh, dx, tr=tre)
        gs["mix_norm"][i] = dg.reshape(-1)

    small = {n: (gs[n] if n == "final_norm" else jnp.stack(gs[n])) for n in WEIGHTS if n not in LARGE}
    large = {n: gs[n] for n in LARGE}
    return loss8[0, 0], dx, small, large


def kernel(x, mix_norm, conv_w_in, conv_b_in, conv_dw, conv_dw_b, conv_ln_g, conv_ln_b, conv_w_out, conv_b_out, ssm_lambda_re, ssm_lambda_im, ssm_log_dt, ssm_b_re, ssm_b_im, ssm_c_re, ssm_c_im, ssm_d, ssm_w_glu, mlp_norm, mlp_w_up, mlp_w_down, final_norm, loss_target, m_mix_norm, m_conv_w_in, m_conv_b_in, m_conv_dw, m_conv_dw_b, m_conv_ln_g, m_conv_ln_b, m_conv_w_out, m_conv_b_out, m_ssm_lambda_re, m_ssm_lambda_im, m_ssm_log_dt, m_ssm_b_re, m_ssm_b_im, m_ssm_c_re, m_ssm_c_im, m_ssm_d, m_ssm_w_glu, m_mlp_norm, m_mlp_w_up, m_mlp_w_down, m_final_norm, v_mix_norm, v_conv_w_in, v_conv_b_in, v_conv_dw, v_conv_dw_b, v_conv_ln_g, v_conv_ln_b, v_conv_w_out, v_conv_b_out, v_ssm_lambda_re, v_ssm_lambda_im, v_ssm_log_dt, v_ssm_b_re, v_ssm_b_im, v_ssm_c_re, v_ssm_c_im, v_ssm_d, v_ssm_w_glu, v_mlp_norm, v_mlp_w_up, v_mlp_w_down, v_final_norm):
    a = dict(locals())
    S = N_CHIPS
    w = {n: a[n] for n in WEIGHTS}
    k_chip = 2 * lax.axis_index("x") + lax.axis_index("y")

    meta = jnp.stack([lax.axis_index("c"), k_chip]).astype(jnp.int32)
    flat = lambda t: t.reshape(-1, t.shape[-1])
    own = [flat(w[n]).astype(BF16) for n in LARGE]
    gathered = _weight_gather(own)
    wg = {n: _place_own(f"place_own_{n}", o, g, meta) for n, o, g in zip(LARGE, own, gathered)}
    sh_shapes = [w[n].shape for n in SHARDED_SMALL]
    sh_all = _small_gather("small_weight_gather", _pack([w[n] for n in SHARDED_SMALL]))
    per_chip = [_unpack(sh_all[2 * k], sh_shapes) for k in range(S)]
    p = {n: w[n] for n in REPLICATED}
    for idx, n in enumerate(SHARDED_SMALL):
        p[n] = jnp.concatenate([per_chip[k][idx] for k in range(S)], axis=-1)

    loss_local, dx, small, large = _local_step(x[0], loss_target[0], p, wg, S)
    loss = lax.psum(loss_local, ("x", "y", "c"))

    order = [(n, j) for n in LARGE for j in range(len(large[n]))]
    arrs = [large[n][j] for n, j in order]
    theirs = _pair_exchange(arrs)
    sums = [_pair_sum(f"grad_pair_sum_{idx}", g, t_, meta) for idx, (g, t_) in enumerate(zip(arrs, theirs))]
    rvs = _chip_exchange([s16 for _, s16 in sums])
    stack = {n: None for n in LARGE}
    for idx, (n, j) in enumerate(order):
        stack[n] = _chip_sum(f"grad_chip_sum_{idx}", sums[idx][0], rvs[idx], stack[n], j, len(large[n]), meta)
    stacks = _pair_gather([stack[n] for n in LARGE], [large[n][0].shape[1] // 2 for n in LARGE])
    grads = {n: st.reshape(w[n].shape) for n, st in zip(LARGE, stacks)}

    small_names = REPLICATED + SHARDED_SMALL
    g_all = _small_gather("small_grad_gather", _pack([small[n] for n in small_names]))
    g_sum = _sum_devices("small_grad_sum", g_all)
    g_parts = dict(zip(small_names, _unpack(g_sum, [small[n].shape for n in small_names])))
    for n in REPLICATED:
        grads[n] = g_parts[n]
    for n in SHARDED_SMALL:
        cols = w[n].shape[-1]
        grads[n] = lax.dynamic_slice_in_dim(g_parts[n], k_chip * cols, cols, axis=g_parts[n].ndim - 1)

    delta, new_m, new_v = {}, {}, {}
    for n in LARGE + SHARDED_SMALL:
        delta[n], new_m[n], new_v[n] = _adamw(f"adamw_{n}", w[n], grads[n], a["m_" + n], a["v_" + n])
    rep_shapes = [w[n].shape for n in REPLICATED]
    rep_rows = sum(_packed_rows(s) for s in rep_shapes)
    res = _adamw("adamw_replicated", _pack([w[n] for n in REPLICATED]), g_sum[:rep_rows],
                 _pack([a["m_" + n] for n in REPLICATED]), _pack([a["v_" + n] for n in REPLICATED]))
    for dst, buf in zip((delta, new_m, new_v), res):
        dst.update(zip(REPLICATED, _unpack(buf, rep_shapes)))

    return (loss, dx[None], *[grads[n] for n in WEIGHTS], *[delta[n] for n in WEIGHTS],
            *[new_m[n] for n in WEIGHTS], *[new_v[n] for n in WEIGHTS])
```

```python
import functools
import math

import jax
import jax.numpy as jnp
from jax import lax
from jax.experimental import pallas as pl
from jax.experimental.pallas import tpu as pltpu

F32 = jnp.float32
BF16 = jnp.bfloat16
MESH = pl.DeviceIdType.MESH

EPS = 1e-6
ADAM_LR = 0.001
ADAM_B1 = 0.9
ADAM_B2 = 0.999
ADAM_EPS = 1e-08
ADAM_WD = 0.01
ADAM_STEP = 10

N_CHIPS = 4
N_DEV = 8
SSM_C = 16
SSM_P = 64
GROUPS_PER_BLOCK = 8
CONV_HALO = 32
VMEM_LIMIT = 56 * 1024 * 1024
MM_OUT_TILE = 1024
MM_K_TILE = 2048


def _pick(n, pref, align):
    t = min(n, pref)
    t -= t % align
    while t >= align:
        if n % t == 0:
            return t
        t -= align
    return n


def _params(*sem):
    return pltpu.CompilerParams(dimension_semantics=sem, vmem_limit_bytes=VMEM_LIMIT)


def _sigmoid(x):
    return 1.0 / (1.0 + jnp.exp(-x))


def _fold8(z):
    r, n = z.shape
    return jnp.sum(z.reshape(r // 8, 8, n), axis=0)


def _w_col(j, kt, nps, off):
    def idx(m, n, k):
        return ((n + off) // nps, j * kt + k, (n + off) % nps)
    return idx


def _w_row(j, kps):
    def idx(m, n, k):
        return (k // kps, j * kps + k % kps, n)
    return idx


GATHER_FORWARD_AT = 0.7


def _mm_nn(name, a, w, w_maps, extras, outs, epilogue, *, tm, tn, tk, gather=()):
    M, K = a.shape
    N = outs[0][0][1]
    grid = (M // tm, N // tn, K // tk)
    nk, nv, ne, no, ng = grid[2], len(w_maps), len(extras), len(outs), len(gather)
    steps = grid[0] * grid[1] * grid[2]
    in_specs = [pl.BlockSpec((tm, tk), lambda m, n, k: (m, k))]
    args = [a]
    for wm in w_maps:
        in_specs.append(pl.BlockSpec((None, tk, tn), wm))
        args.append(w)
    for arr, kind, off in extras:
        if kind == "mn":
            in_specs.append(pl.BlockSpec((tm, tn), lambda m, n, k: (m, n)))
        else:
            in_specs.append(pl.BlockSpec((1, tn), functools.partial(lambda m, n, k, o: (0, n + o), o=off // tn)))
        args.append(arr)
    in_specs += [ANY] * ng
    args += list(gather)
    out_specs = [pl.BlockSpec((tm, tn), lambda m, n, k: (m, n)) for _ in outs] + [ANY] * ng
    out_shape = [jax.ShapeDtypeStruct(s, d) for s, d in outs]
    out_shape += [jax.ShapeDtypeStruct((N_CHIPS,) + g.shape, g.dtype) for g in gather]
    n_acc = nv if nk > 1 else 0

    def body(*refs):
        a_ref = refs[0]
        w_refs = refs[1:1 + nv]
        e_refs = refs[1 + nv:1 + nv + ne]
        n_in = 1 + nv + ne + ng
        o_refs = refs[n_in:n_in + no]
        acc_refs = refs[n_in + no + ng:n_in + no + ng + n_acc]
        k = pl.program_id(2)
        if ng:
            send, forward, drain = _gather_phases([g.shape for g in gather], refs[n_in - ng:n_in],
                                                  refs[n_in + no:n_in + no + ng], *refs[n_in + no + ng + n_acc:])
            step = (pl.program_id(0) * grid[1] + pl.program_id(1)) * grid[2] + k
            pl.when(step == 0)(send)

        def write(accs):
            for o_ref, r in zip(o_refs, epilogue(accs, [e[...] for e in e_refs])):
                o_ref[...] = r.astype(o_ref.dtype)

        av = a_ref[...].astype(BF16)
        if nk == 1:
            write([jnp.dot(av, w_ref[...], preferred_element_type=F32) for w_ref in w_refs])
        else:
            @pl.when(k == 0)
            def _():
                for acc in acc_refs:
                    acc[...] = jnp.zeros_like(acc)

            for acc, w_ref in zip(acc_refs, w_refs):
                acc[...] += jnp.dot(av, w_ref[...], preferred_element_type=F32)
            pl.when(k == nk - 1)(lambda: write([acc[...] for acc in acc_refs]))
        if ng:
            pl.when(step == min(int(GATHER_FORWARD_AT * steps), steps - 1))(forward)
            pl.when(step == steps - 1)(drain)

    sems = ("arbitrary",) * 3 if ng else ("parallel", "parallel", "arbitrary")
    return pl.pallas_call(
        body, name=name, grid=grid, in_specs=in_specs, out_specs=out_specs, out_shape=out_shape,
        scratch_shapes=[pltpu.VMEM((tm, tn), F32) for _ in range(n_acc)] + (_gather_sems(ng) if ng else []),
        compiler_params=_params(*sems),
    )(*args)


def _mm_nt(name, a, w, w_map, extras, outs, epilogue, *, tm, tr, tc):
    M, N = a.shape
    R = outs[0][0][1]
    grid = (M // tm, R // tr, N // tc)
    nc, ne, no = grid[2], len(extras), len(outs)
    in_specs = [pl.BlockSpec((tm, tc), lambda m, r, c: (m, c)), pl.BlockSpec((None, tr, tc), w_map)]
    args = [a, w]
    for arr in extras:
        in_specs.append(pl.BlockSpec((tm, tr), lambda m, r, c: (m, r)))
        args.append(arr)
    out_specs = [pl.BlockSpec((tm, tr), lambda m, r, c: (m, r)) for _ in outs]
    out_shape = [jax.ShapeDtypeStruct(s, d) for s, d in outs]

    def body(*refs):
        a_ref, w_ref = refs[0], refs[1]
        e_refs = refs[2:2 + ne]
        o_refs = refs[2 + ne:2 + ne + no]
        def partial_product():
            return lax.dot_general(a_ref[...].astype(BF16), w_ref[...], (((1,), (1,)), ((), ())),
                                   preferred_element_type=F32)

        if nc == 1:
            for o_ref, r in zip(o_refs, epilogue(partial_product(), [e[...] for e in e_refs])):
                o_ref[...] = r.astype(o_ref.dtype)
            return
        acc = refs[2 + ne + no]
        c = pl.program_id(2)

        @pl.when(c == 0)
        def _():
            acc[...] = jnp.zeros_like(acc)

        acc[...] += partial_product()

        @pl.when(c == nc - 1)
        def _():
            res = epilogue(acc[...], [e[...] for e in e_refs])
            for o_ref, r in zip(o_refs, res):
                o_ref[...] = r.astype(o_ref.dtype)

    return pl.pallas_call(
        body, name=name, grid=grid, in_specs=in_specs, out_specs=out_specs, out_shape=out_shape,
        scratch_shapes=[pltpu.VMEM((tm, tr), F32)] if nc > 1 else [],
        compiler_params=_params("parallel", "parallel", "arbitrary"),
    )(*args)


def _nt_col(j, rt, cps):
    def idx(m, r, c):
        return (c // cps, j * rt + r, c % cps)
    return idx


def _nt_row(j, rps):
    def idx(m, r, c):
        return (r // rps, j * rps + r % rps, c)
    return idx


def _mm_tn(name, a, b, out_sharding, n_shards, *, tr, tc, tt):
    T, R = a.shape
    N = b.shape[1]
    S = n_shards
    grid = (R // tr, N // tc, T // tt)
    if out_sharding == "col":
        nps = (N // S) // tc
        out_shape = jax.ShapeDtypeStruct((S, R, N // S), F32)
        out_spec = pl.BlockSpec((None, tr, tc), lambda r, n, t: (n // nps, r, n % nps))
    else:
        rps = (R // S) // tr
        out_shape = jax.ShapeDtypeStruct((S, R // S, N), F32)
        out_spec = pl.BlockSpec((None, tr, tc), lambda r, n, t: (r // rps, r % rps, n))

    def body(a_ref, b_ref, o_ref):
        t = pl.program_id(2)

        @pl.when(t == 0)
        def _():
            o_ref[...] = jnp.zeros_like(o_ref)

        o_ref[...] += lax.dot_general(a_ref[...].astype(BF16), b_ref[...].astype(BF16), (((0,), (0,)), ((), ())),
                                      preferred_element_type=F32)

    return pl.pallas_call(
        body, name=name, grid=grid,
        in_specs=[pl.BlockSpec((tt, tr), lambda r, n, t: (t, r)), pl.BlockSpec((tt, tc), lambda r, n, t: (t, n))],
        out_specs=out_spec, out_shape=out_shape,
        compiler_params=_params("parallel", "parallel", "arbitrary"),
    )(a, b)


def _rms_fwd(name, x, g, *, tr):
    T, D = x.shape

    def body(x_ref, g_ref, h_ref):
        xv = x_ref[...]
        r = lax.rsqrt(jnp.mean(xv * xv, axis=-1, keepdims=True) + EPS)
        h_ref[...] = (xv * r * g_ref[...]).astype(BF16)

    return pl.pallas_call(
        body, name=name, grid=(T // tr,),
        in_specs=[pl.BlockSpec((tr, D), lambda i: (i, 0)), pl.BlockSpec((1, D), lambda i: (0, 0))],
        out_specs=pl.BlockSpec((tr, D), lambda i: (i, 0)),
        out_shape=jax.ShapeDtypeStruct((T, D), BF16),
        compiler_params=_params("parallel"),
    )(x, g)


def _rms_bwd_rows(xv, gv, dh):
    r = lax.rsqrt(jnp.mean(xv * xv, axis=-1, keepdims=True) + EPS)
    xh = xv * r
    gdy = dh * gv
    dx = r * (gdy - xh * jnp.mean(gdy * xh, axis=-1, keepdims=True))
    return dx, dh * xh


def _rms_bwd(name, x, g, dh, dx_in, *, tr):
    T, D = x.shape
    nt = T // tr

    def body(x_ref, g_ref, dh_ref, dxi_ref, dx_ref, dxb_ref, dg_ref, cs_ref, dg_acc, cs_acc):
        i = pl.program_id(0)

        @pl.when(i == 0)
        def _():
            dg_acc[...] = jnp.zeros_like(dg_acc)
            cs_acc[...] = jnp.zeros_like(cs_acc)

        dx, dgx = _rms_bwd_rows(x_ref[...], g_ref[...], dh_ref[...].astype(F32))
        dxo = dxi_ref[...] + dx
        dx_ref[...] = dxo
        dxb_ref[...] = dxo.astype(BF16)
        dg_acc[...] += _fold8(dgx)
        cs_acc[...] += _fold8(dxo)

        @pl.when(i == nt - 1)
        def _():
            dg_ref[...] = jnp.sum(dg_acc[...], axis=0, keepdims=True)
            cs_ref[...] = jnp.sum(cs_acc[...], axis=0, keepdims=True)

    row = pl.BlockSpec((tr, D), lambda i: (i, 0))
    vec = pl.BlockSpec((1, D), lambda i: (0, 0))
    return pl.pallas_call(
        body, name=name, grid=(nt,),
        in_specs=[row, vec, row, row], out_specs=[row, row, vec, vec],
        out_shape=[jax.ShapeDtypeStruct((T, D), F32), jax.ShapeDtypeStruct((T, D), BF16),
                   jax.ShapeDtypeStruct((1, D), F32), jax.ShapeDtypeStruct((1, D), F32)],
        scratch_shapes=[pltpu.VMEM((8, D), F32), pltpu.VMEM((8, D), F32)],
        compiler_params=_params("arbitrary"),
    )(x, g, dh, dx_in)


def _final_loss(name, x, g, target, *, tr):
    T, D = x.shape
    nt = T // tr

    def body(x_ref, g_ref, t_ref, dx_ref, dxb_ref, loss_ref, dg_ref, l_acc, dg_acc):
        i = pl.program_id(0)

        @pl.when(i == 0)
        def _():
            l_acc[...] = jnp.zeros_like(l_acc)
            dg_acc[...] = jnp.zeros_like(dg_acc)

        xv = x_ref[...]
        gv = g_ref[...]
        r = lax.rsqrt(jnp.mean(xv * xv, axis=-1, keepdims=True) + EPS)
        err = xv * r * gv - t_ref[...]
        l_acc[...] += _fold8(err * err)
        dx, dgx = _rms_bwd_rows(xv, gv, err * (1.0 / D))
        dx_ref[...] = dx
        dxb_ref[...] = dx.astype(BF16)
        dg_acc[...] += _fold8(dgx)

        @pl.when(i == nt - 1)
        def _():
            tot = jnp.sum(jnp.sum(l_acc[...], axis=0, keepdims=True), axis=1, keepdims=True)
            loss_ref[...] = jnp.broadcast_to(tot * (0.5 / D), loss_ref.shape)
            dg_ref[...] = jnp.sum(dg_acc[...], axis=0, keepdims=True)

    row = pl.BlockSpec((tr, D), lambda i: (i, 0))
    vec = pl.BlockSpec((1, D), lambda i: (0, 0))
    return pl.pallas_call(
        body, name=name, grid=(nt,),
        in_specs=[row, vec, row],
        out_specs=[row, row, pl.BlockSpec((8, 128), lambda i: (0, 0)), vec],
        out_shape=[jax.ShapeDtypeStruct((T, D), F32), jax.ShapeDtypeStruct((T, D), BF16),
                   jax.ShapeDtypeStruct((8, 128), F32), jax.ShapeDtypeStruct((1, D), F32)],
        scratch_shapes=[pltpu.VMEM((8, D), F32), pltpu.VMEM((8, D), F32)],
        compiler_params=_params("arbitrary"),
    )(x, g, target)


CONV_ROWS = 64
CONV_LANES = 128


def _tap_windows(ext_ref, r0, cols, first, width):
    last = first + width - 1
    for r in range(8):
        qs = [q for q in range(last // 8 + 1) if first <= 8 * q + r <= last]
        if not qs:
            continue
        n = CONV_ROWS + 8 * qs[-1] + (8 if r else 0)
        win = ext_ref[pl.ds(r0, n), cols]
        if r:
            win = pltpu.roll(win, n - r, 0)
        for q in qs:
            yield 8 * q + r - first, win[8 * q:8 * q + CONV_ROWS]


def _conv_blocks(tr, D):
    for cb in range(D // CONV_LANES):
        for rb in range(tr // CONV_ROWS):
            yield rb * CONV_ROWS, pl.ds(cb * CONV_LANES, CONV_LANES)


def _conv_taps(ext_ref, dwp_ref, out_ref, tr, width, first):
    for r0, cols in _conv_blocks(tr, out_ref.shape[1]):
        acc = None
        for j, win in _tap_windows(ext_ref, r0, cols, first, width):
            term = dwp_ref[pl.ds(j, 1), cols] * win
            acc = term if acc is None else acc + term
        out_ref[pl.ds(r0, CONV_ROWS), cols] = acc


def _ln_rows(v2, lg, lb):
    mu = jnp.mean(v2, axis=-1, keepdims=True)
    xc = v2 - mu
    rs = lax.rsqrt(jnp.mean(xc * xc, axis=-1, keepdims=True) + EPS)
    xh = xc * rs
    return xh, rs, xh * lg + lb


def _halo_prev(tr):
    q = tr // CONV_HALO
    return lambda i: (jnp.maximum(i * q - 1, 0), 0)


def _conv_fwd(name, v, dwp, dwb, lg, lb, *, tr, width):
    T, D = v.shape
    first = CONV_HALO - (width - 1)

    def body(v_ref, halo_ref, dwp_ref, dwb_ref, lg_ref, lb_ref, s_ref, ext, conv):
        i = pl.program_id(0)
        ext[pl.ds(0, CONV_HALO), :] = jnp.where(i == 0, 0.0, halo_ref[...])
        ext[pl.ds(CONV_HALO, tr), :] = v_ref[...]
        _conv_taps(ext, dwp_ref, conv, tr, width, first)
        _, _, ln = _ln_rows(conv[...] + dwb_ref[...], lg_ref[...], lb_ref[...])
        s_ref[...] = (ln * _sigmoid(ln)).astype(BF16)

    row = pl.BlockSpec((tr, D), lambda i: (i, 0))
    vec = pl.BlockSpec((1, D), lambda i: (0, 0))
    return pl.pallas_call(
        body, name=name, grid=(T // tr,),
        in_specs=[row, pl.BlockSpec((CONV_HALO, D), _halo_prev(tr)),
                  pl.BlockSpec((CONV_HALO, D), lambda i: (0, 0)), vec, vec, vec],
        out_specs=row, out_shape=jax.ShapeDtypeStruct((T, D), BF16),
        scratch_shapes=[pltpu.VMEM((tr + CONV_HALO, D), F32), pltpu.VMEM((tr, D), F32)],
        compiler_params=_params("parallel"),
    )(v, v, dwp, dwb, lg, lb)


def _conv_bwd_a(name, v, ds, dwp, dwb, lg, lb, *, tr, width):
    T, D = v.shape
    nt = T // tr
    first = CONV_HALO - (width - 1)

    def body(v_ref, halo_ref, ds_ref, dwp_ref, dwb_ref, lg_ref, lb_ref,
             dv2_ref, dlg_ref, dlb_ref, ddwb_ref, ext, conv, a_lg, a_lb, a_dwb):
        i = pl.program_id(0)

        @pl.when(i == 0)
        def _():
            a_lg[...] = jnp.zeros_like(a_lg)
            a_lb[...] = jnp.zeros_like(a_lb)
            a_dwb[...] = jnp.zeros_like(a_dwb)

        ext[pl.ds(0, CONV_HALO), :] = jnp.where(i == 0, 0.0, halo_ref[...])
        ext[pl.ds(CONV_HALO, tr), :] = v_ref[...]
        _conv_taps(ext, dwp_ref, conv, tr, width, first)
        v2 = conv[...] + dwb_ref[...]
        lgv = lg_ref[...]
        xh, rs, ln = _ln_rows(v2, lgv, lb_ref[...])
        sg = _sigmoid(ln)
        dln = ds_ref[...] * (sg * (1.0 + ln * (1.0 - sg)))
        a_lg[...] += _fold8(dln * xh)
        a_lb[...] += _fold8(dln)
        dxh = dln * lgv
        dv2 = rs * (dxh - jnp.mean(dxh, axis=-1, keepdims=True)
                    - xh * jnp.mean(dxh * xh, axis=-1, keepdims=True))
        dv2_ref[...] = dv2
        a_dwb[...] += _fold8(dv2)

        @pl.when(i == nt - 1)
        def _():
            dlg_ref[...] = jnp.sum(a_lg[...], axis=0, keepdims=True)
            dlb_ref[...] = jnp.sum(a_lb[...], axis=0, keepdims=True)
            ddwb_ref[...] = jnp.sum(a_dwb[...], axis=0, keepdims=True)

    row = pl.BlockSpec((tr, D), lambda i: (i, 0))
    vec = pl.BlockSpec((1, D), lambda i: (0, 0))
    return pl.pallas_call(
        body, name=name, grid=(nt,),
        in_specs=[row, pl.BlockSpec((CONV_HALO, D), _halo_prev(tr)), row,
                  pl.BlockSpec((CONV_HALO, D), lambda i: (0, 0)), vec, vec, vec],
        out_specs=[row, vec, vec, vec],
        out_shape=[jax.ShapeDtypeStruct((T, D), F32)] + [jax.ShapeDtypeStruct((1, D), F32)] * 3,
        scratch_shapes=[pltpu.VMEM((tr + CONV_HALO, D), F32), pltpu.VMEM((tr, D), F32)] + [pltpu.VMEM((8, D), F32)] * 3,
        compiler_params=_params("arbitrary"),
    )(v, v, ds, dwp, dwb, lg, lb)


def _conv_bwd_b(name, dv2, v, a, g, dwp, *, tr, width):
    T, D = v.shape
    nt = T // tr
    q = tr // CONV_HALO
    first = CONV_HALO - (width - 1)
    last_halo = T // CONV_HALO - 1

    def body(dv2_ref, nxt_ref, v_ref, halo_ref, a_ref, g_ref, dwp_ref,
             du_ref, ddw_ref, dbin_ref, ext_v, ext_d, dvs, a_dw, a_b):
        i = pl.program_id(0)

        @pl.when(i == 0)
        def _():
            a_dw[...] = jnp.zeros_like(a_dw)
            a_b[...] = jnp.zeros_like(a_b)

        ext_v[pl.ds(0, CONV_HALO), :] = jnp.where(i == 0, 0.0, halo_ref[...])
        ext_v[pl.ds(CONV_HALO, tr), :] = v_ref[...]
        ext_d[pl.ds(0, tr), :] = dv2_ref[...]
        ext_d[pl.ds(tr, CONV_HALO), :] = jnp.where(i == nt - 1, 0.0, nxt_ref[...])
        for r0, cols in _conv_blocks(tr, D):
            dv = None
            for o, win in _tap_windows(ext_d, r0, cols, 0, width):
                term = dwp_ref[pl.ds(width - 1 - o, 1), cols] * win
                dv = term if dv is None else dv + term
            dvs[pl.ds(r0, CONV_ROWS), cols] = dv
            d_cur = ext_d[pl.ds(r0, CONV_ROWS), cols]
            for j, win in _tap_windows(ext_v, r0, cols, first, width):
                a_dw[j, :, cols] += _fold8(d_cur * win)
        dv = dvs[...]
        av = a_ref[...].astype(F32)
        sg = _sigmoid(g_ref[...].astype(F32))
        da = dv * sg
        dg = dv * av * sg * (1.0 - sg)
        du_ref[:, pl.ds(0, D)] = da.astype(BF16)
        du_ref[:, pl.ds(D, D)] = dg.astype(BF16)
        a_b[:, pl.ds(0, D)] += _fold8(da)
        a_b[:, pl.ds(D, D)] += _fold8(dg)

        @pl.when(i == nt - 1)
        def _():
            ddw_ref[...] = jnp.sum(a_dw[...], axis=1)
            dbin_ref[...] = jnp.sum(a_b[...], axis=0, keepdims=True)

    row = pl.BlockSpec((tr, D), lambda i: (i, 0))
    return pl.pallas_call(
        body, name=name, grid=(nt,),
        in_specs=[row, pl.BlockSpec((CONV_HALO, D), lambda i: (jnp.minimum((i + 1) * q, last_halo), 0)),
                  row, pl.BlockSpec((CONV_HALO, D), _halo_prev(tr)), row, row,
                  pl.BlockSpec((CONV_HALO, D), lambda i: (0, 0))],
        out_specs=[pl.BlockSpec((tr, 2 * D), lambda i: (i, 0)),
                   pl.BlockSpec((CONV_HALO, D), lambda i: (0, 0)),
                   pl.BlockSpec((1, 2 * D), lambda i: (0, 0))],
        out_shape=[jax.ShapeDtypeStruct((T, 2 * D), BF16), jax.ShapeDtypeStruct((CONV_HALO, D), F32),
                   jax.ShapeDtypeStruct((1, 2 * D), F32)],
        scratch_shapes=[pltpu.VMEM((tr + CONV_HALO, D), F32), pltpu.VMEM((tr + CONV_HALO, D), F32),
                        pltpu.VMEM((tr, D), F32), pltpu.VMEM((CONV_HALO, 8, D), F32), pltpu.VMEM((8, 2 * D), F32)],
        compiler_params=_params("arbitrary"),
    )(dv2, dv2, v, v, a, g, dwp)


def _glu_bwd(name, dout, val, gate, *, tr):
    T, D = dout.shape

    def body(d_ref, v_ref, g_ref, dz_ref):
        d = d_ref[...]
        sg = _sigmoid(g_ref[...].astype(F32))
        dz_ref[:, pl.ds(0, D)] = (d * sg).astype(BF16)
        dz_ref[:, pl.ds(D, D)] = (d * v_ref[...].astype(F32) * sg * (1.0 - sg)).astype(BF16)

    row = pl.BlockSpec((tr, D), lambda i: (i, 0))
    return pl.pallas_call(
        body, name=name, grid=(T // tr,), in_specs=[row, row, row],
        out_specs=pl.BlockSpec((tr, 2 * D), lambda i: (i, 0)),
        out_shape=jax.ShapeDtypeStruct((T, 2 * D), BF16),
        compiler_params=_params("parallel"),
    )(dout, val, gate)


GELU_C = math.sqrt(2.0 / math.pi)
GELU_A = 0.044715


def _gelu(x):
    return 0.5 * x * (1.0 + jnp.tanh(GELU_C * (x + GELU_A * x * x * x)))


def _gelu_grad(x):
    t = jnp.tanh(GELU_C * (x + GELU_A * x * x * x))
    return 0.5 * (1.0 + t) + 0.5 * x * (1.0 - t * t) * GELU_C * (1.0 + 3.0 * GELU_A * x * x)


def _cmul(ar, ai, br, bi):
    return ar * br - ai * bi, ar * bi + ai * br


SCAN_SEQS = 8


def _interleave_rows(x, tmp_ref):
    n = x.shape[0]
    tmp_ref[...] = x
    return jnp.concatenate([tmp_ref[pl.ds(i, SCAN_SEQS, stride=n // SCAN_SEQS), :] for i in range(n // SCAN_SEQS)],
                           axis=0)


def _deinterleave_rows(x, tmp_ref):
    n = x.shape[0]
    tmp_ref[...] = x
    return jnp.concatenate([tmp_ref[pl.ds(s, n // SCAN_SEQS, stride=SCAN_SEQS), :] for s in range(SCAN_SEQS)], axis=0)


def _scan_chunk(xr, xi, tab_ref, cin_r, cin_i, *, lanes, reverse):
    n = xr.shape[0]
    L = n // SCAN_SEQS
    re, im = pl.ds(0, lanes), pl.ds(lanes, lanes)
    one, top = (L - 1, 0) if reverse else (0, L - 1)
    a_r = jnp.broadcast_to(tab_ref[pl.ds(one, 1), re], (SCAN_SEQS, lanes))
    a_i = jnp.broadcast_to(tab_ref[pl.ds(one, 1), im], (SCAN_SEQS, lanes))
    sr = si = jnp.zeros((SCAN_SEQS, lanes), F32)
    loc_r, loc_i = [None] * L, [None] * L
    for i in (reversed(range(L)) if reverse else range(L)):
        rows = slice(SCAN_SEQS * i, SCAN_SEQS * (i + 1))
        sr, si = a_r * sr - a_i * si + xr[rows], a_r * si + a_i * sr + xi[rows]
        loc_r[i], loc_i[i] = sr, si
    top_r, top_i = tab_ref[pl.ds(top, 1), re], tab_ref[pl.ds(top, 1), im]
    sub = lax.broadcasted_iota(jnp.int32, (SCAN_SEQS, lanes), 0)
    cr, ci = cin_r, cin_i
    in_r = in_i = jnp.zeros((SCAN_SEQS, lanes), F32)
    for s in (reversed(range(SCAN_SEQS)) if reverse else range(SCAN_SEQS)):
        in_r, in_i = jnp.where(sub == s, cr, in_r), jnp.where(sub == s, ci, in_i)
        dr, di = _cmul(top_r, top_i, cr, ci)
        cr, ci = dr + sr[s:s + 1], di + si[s:s + 1]
    out_r, out_i = [None] * L, [None] * L
    for i in range(L):
        dr, di = _cmul(tab_ref[pl.ds(i, 1), re], tab_ref[pl.ds(i, 1), im], in_r, in_i)
        out_r[i], out_i[i] = loc_r[i] + dr, loc_i[i] + di
    return jnp.concatenate(out_r, axis=0), jnp.concatenate(out_i, axis=0), cr, ci


def _ssm_fwd(name, h, bbd, ccd, tab, dskip, *, tc):
    T, D = h.shape
    NG, CB, L2 = bbd.shape
    lanes = L2 // 2
    nch = T // tc

    def body(h_ref, bb_ref, cc_ref, tab_ref, d_ref, y_ref, s_ref, cin_ref, carry, tmp):
        t = pl.program_id(1)

        @pl.when(t == 0)
        def _():
            carry[...] = jnp.zeros_like(carry)

        cin_ref[...] = carry[...]
        uf = _interleave_rows(h_ref[...].astype(F32), tmp)
        bu = jnp.dot(uf.astype(BF16), bb_ref[...], preferred_element_type=F32)
        sr, si, cr, ci = _scan_chunk(bu[:, :lanes], bu[:, lanes:], tab_ref,
                                     carry[pl.ds(0, 1), pl.ds(0, lanes)], carry[pl.ds(0, 1), pl.ds(lanes, lanes)],
                                     lanes=lanes, reverse=False)
        carry[pl.ds(0, 1), pl.ds(0, lanes)] = cr
        carry[pl.ds(0, 1), pl.ds(lanes, lanes)] = ci
        s = jnp.concatenate([sr, si], axis=1).astype(BF16)
        s_ref[...] = s
        yp = jnp.dot(s, cc_ref[...], preferred_element_type=F32) + d_ref[...] * uf
        y_ref[...] = _deinterleave_rows(_gelu(yp), tmp).astype(BF16)

    return pl.pallas_call(
        body, name=name, grid=(NG, nch),
        in_specs=[pl.BlockSpec((tc, CB), lambda b, t: (t, b)),
                  pl.BlockSpec((None, CB, L2), lambda b, t: (b, 0, 0)),
                  pl.BlockSpec((None, L2, CB), lambda b, t: (b, 0, 0)),
                  pl.BlockSpec((None,) + tab.shape[1:], lambda b, t: (b, 0, 0)),
                  pl.BlockSpec((1, CB), lambda b, t: (0, b))],
        out_specs=[pl.BlockSpec((tc, CB), lambda b, t: (t, b)),
                   pl.BlockSpec((tc, L2), lambda b, t: (t, b)),
                   pl.BlockSpec((None, None, 8, L2), lambda b, t: (b, t, 0, 0))],
        out_shape=[jax.ShapeDtypeStruct((T, D), BF16), jax.ShapeDtypeStruct((T, NG * L2), BF16),
                   jax.ShapeDtypeStruct((NG, nch, 8, L2), F32)],
        scratch_shapes=[pltpu.VMEM((8, L2), F32), pltpu.VMEM((tc, CB), F32)],
        compiler_params=_params("parallel", "arbitrary"),
    )(h, bbd, ccd, tab, dskip)


def _ssm_bwd(name, h, dy, states, cins, bbd, ccd, tabr, dskip, *, tc):
    T, D = h.shape
    NG, CB, L2 = bbd.shape
    lanes = L2 // 2
    nch = T // tc

    def body(h_ref, dy_ref, s_ref, cin_ref, bb_ref, cc_ref, tabr_ref, d_ref,
             dh_ref, dbb_ref, dcc_ref, da_ref, dd_ref, gcarry, a_da, a_dd, tmp):
        t = pl.program_id(1)

        @pl.when(t == 0)
        def _():
            gcarry[...] = jnp.zeros_like(gcarry)
            a_da[...] = jnp.zeros_like(a_da)
            a_dd[...] = jnp.zeros_like(a_dd)
            dbb_ref[...] = jnp.zeros_like(dbb_ref)
            dcc_ref[...] = jnp.zeros_like(dcc_ref)

        uf = _interleave_rows(h_ref[...].astype(F32), tmp)
        u = uf.astype(BF16)
        dyv = _interleave_rows(dy_ref[...], tmp)
        cin_r = cin_ref[pl.ds(0, 1), pl.ds(0, lanes)]
        cin_i = cin_ref[pl.ds(0, 1), pl.ds(lanes, lanes)]
        s = s_ref[...]
        dv = d_ref[...]
        yp = jnp.dot(s, cc_ref[...], preferred_element_type=F32) + dv * uf
        dyp = dyv * _gelu_grad(yp)
        a_dd[...] += _fold8(dyp * uf)
        dypb = dyp.astype(BF16)
        dcc_ref[...] += lax.dot_general(s, dypb, (((0,), (0,)), ((), ())), preferred_element_type=F32)
        ds = lax.dot_general(dypb, cc_ref[...], (((1,), (1,)), ((), ())), preferred_element_type=F32)
        gr, gi, cr, ci = _scan_chunk(ds[:, :lanes], ds[:, lanes:], tabr_ref,
                                     gcarry[pl.ds(0, 1), pl.ds(0, lanes)], gcarry[pl.ds(0, 1), pl.ds(lanes, lanes)],
                                     lanes=lanes, reverse=True)
        gcarry[pl.ds(0, 1), pl.ds(0, lanes)] = cr
        gcarry[pl.ds(0, 1), pl.ds(lanes, lanes)] = ci
        gb = jnp.concatenate([gr, gi], axis=1).astype(BF16)
        du = lax.dot_general(gb, bb_ref[...], (((1,), (1,)), ((), ())), preferred_element_type=F32)
        dh_ref[...] = _deinterleave_rows(du + dv * dyp, tmp)
        dbb_ref[...] += lax.dot_general(u, gb, (((0,), (0,)), ((), ())), preferred_element_type=F32)
        sr, si = s[:, :lanes].astype(F32), s[:, lanes:].astype(F32)
        first = lax.broadcasted_iota(jnp.int32, (SCAN_SEQS, lanes), 0) == 0
        head_r = jnp.where(first, cin_r, pltpu.roll(sr[tc - SCAN_SEQS:], 1, 0))
        head_i = jnp.where(first, cin_i, pltpu.roll(si[tc - SCAN_SEQS:], 1, 0))
        pr = jnp.concatenate([head_r, sr[:tc - SCAN_SEQS]], axis=0)
        pi = jnp.concatenate([head_i, si[:tc - SCAN_SEQS]], axis=0)
        a_da[:, pl.ds(0, lanes)] += _fold8(pr * gr + pi * gi)
        a_da[:, pl.ds(lanes, lanes)] += _fold8(pr * gi - pi * gr)

        @pl.when(t == nch - 1)
        def _():
            da_ref[...] = jnp.sum(a_da[...], axis=0, keepdims=True)
            dd_ref[...] = jnp.sum(a_dd[...], axis=0, keepdims=True)

    rev = lambda b, t: (nch - 1 - t, b)
    return pl.pallas_call(
        body, name=name, grid=(NG, nch),
        in_specs=[pl.BlockSpec((tc, CB), rev), pl.BlockSpec((tc, CB), rev), pl.BlockSpec((tc, L2), rev),
                  pl.BlockSpec((None, None, 8, L2), lambda b, t: (b, nch - 1 - t, 0, 0)),
                  pl.BlockSpec((None, CB, L2), lambda b, t: (b, 0, 0)),
                  pl.BlockSpec((None, L2, CB), lambda b, t: (b, 0, 0)),
                  pl.BlockSpec((None,) + tabr.shape[1:], lambda b, t: (b, 0, 0)),
                  pl.BlockSpec((1, CB), lambda b, t: (0, b))],
        out_specs=[pl.BlockSpec((tc, CB), rev),
                   pl.BlockSpec((None, CB, L2), lambda b, t: (b, 0, 0)),
                   pl.BlockSpec((None, L2, CB), lambda b, t: (b, 0, 0)),
                   pl.BlockSpec((None, 1, L2), lambda b, t: (b, 0, 0)),
                   pl.BlockSpec((1, CB), lambda b, t: (0, b))],
        out_shape=[jax.ShapeDtypeStruct((T, D), F32), jax.ShapeDtypeStruct((NG, CB, L2), F32),
                   jax.ShapeDtypeStruct((NG, L2, CB), F32), jax.ShapeDtypeStruct((NG, 1, L2), F32),
                   jax.ShapeDtypeStruct((1, D), F32)],
        scratch_shapes=[pltpu.VMEM((8, L2), F32), pltpu.VMEM((8, L2), F32), pltpu.VMEM((8, CB), F32),
                        pltpu.VMEM((tc, CB), F32)],
        compiler_params=_params("parallel", "arbitrary"),
    )(h, dy, states, cins, bbd, ccd, tabr, dskip)


def _zoh(lr, li, ldt):
    dt = jnp.exp(ldt)
    mag = jnp.exp(lr * dt)
    ar = mag * jnp.cos(li * dt)
    ai = mag * jnp.sin(li * dt)
    den = lr * lr + li * li
    nr = ar - 1.0
    kr = (nr * lr + ai * li) / den
    ki = (ai * lr - nr * li) / den
    return dt, ar, ai, kr, ki, den


def _ssm_prep(name, lr, li, ldt, br, bi):
    shp = jax.ShapeDtypeStruct(lr.shape, F32)

    def body(lr_ref, li_ref, ldt_ref, br_ref, bi_ref, ar_ref, ai_ref, kr_ref, ki_ref, bbr_ref, bbi_ref):
        _, ar, ai, kr, ki, _ = _zoh(lr_ref[...], li_ref[...], ldt_ref[...])
        ar_ref[...] = ar
        ai_ref[...] = ai
        kr_ref[...] = kr
        ki_ref[...] = ki
        bbr, bbi = _cmul(kr, ki, br_ref[...], bi_ref[...])
        bbr_ref[...] = bbr
        bbi_ref[...] = bbi

    return pl.pallas_call(body, name=name, out_shape=[shp] * 6)(lr, li, ldt, br, bi)


def _ssm_powers(name, ar, ai, rows):
    NG, _, lanes = ar.shape

    def body(ar_ref, ai_ref, tf_ref, tr_ref):
        a_r, a_i = ar_ref[...], ai_ref[...]
        pw = [(a_r, a_i)]
        for _ in range(7):
            pw.append(_cmul(pw[-1][0], pw[-1][1], a_r, a_i))
        row = lax.broadcasted_iota(jnp.int32, (8, lanes), 0)
        fr = fi = rr = ri = jnp.zeros((8, lanes), F32)
        for n in range(8):
            fr = jnp.where(row == n, pw[n][0], fr)
            fi = jnp.where(row == n, pw[n][1], fi)
            rr = jnp.where(row == 7 - n, pw[n][0], rr)
            ri = jnp.where(row == 7 - n, pw[n][1], ri)
        top_r, top_i = pw[7]
        size = 8
        while size < rows:
            hr, hi = _cmul(fr, fi, top_r, top_i)
            fr, fi = jnp.concatenate([fr, hr], axis=0), jnp.concatenate([fi, hi], axis=0)
            hr, hi = _cmul(rr, ri, top_r, top_i)
            rr, ri = jnp.concatenate([hr, rr], axis=0), jnp.concatenate([hi, ri], axis=0)
            top_r, top_i = _cmul(top_r, top_i, top_r, top_i)
            size *= 2
        tf_ref[:, pl.ds(0, lanes)] = fr
        tf_ref[:, pl.ds(lanes, lanes)] = fi
        tr_ref[:, pl.ds(0, lanes)] = rr
        tr_ref[:, pl.ds(lanes, lanes)] = -ri

    vec = pl.BlockSpec((None, 1, lanes), lambda b: (b, 0, 0))
    tab = pl.BlockSpec((None, rows, 2 * lanes), lambda b: (b, 0, 0))
    shp = jax.ShapeDtypeStruct((NG, rows, 2 * lanes), F32)
    return pl.pallas_call(body, name=name, grid=(NG,), in_specs=[vec, vec], out_specs=[tab, tab],
                          out_shape=[shp, shp], compiler_params=_params("parallel"))(ar, ai)


def _ssm_prep_bwd_b(name, kr, ki, br, bi, gbr, gbi):
    shp = jax.ShapeDtypeStruct(kr.shape, F32)

    def body(kr_ref, ki_ref, br_ref, bi_ref, gr_ref, gi_ref, dbr_ref, dbi_ref, tr_ref, ti_ref):
        gr, gi = gr_ref[...], gi_ref[...]
        dbr, dbi = _cmul(kr_ref[...], -ki_ref[...], gr, gi)
        dbr_ref[...] = dbr
        dbi_ref[...] = dbi
        t_r, t_i = _cmul(br_ref[...], -bi_ref[...], gr, gi)
        tr_ref[...] = t_r
        ti_ref[...] = t_i

    return pl.pallas_call(body, name=name, out_shape=[shp] * 4)(kr, ki, br, bi, gbr, gbi)


def _ssm_prep_bwd_a(name, lr, li, ldt, gar, gai, tkr, tki):
    G, P = lr.shape

    def body(lr_ref, li_ref, ldt_ref, gar_ref, gai_ref, tkr_ref, tki_ref, dlr_ref, dli_ref, dldt_ref):
        lr_v, li_v = lr_ref[...], li_ref[...]
        dt, ar, ai, kr, ki, den = _zoh(lr_v, li_v, ldt_ref[...])
        gkr = jnp.sum(tkr_ref[...], axis=0)
        gki = jnp.sum(tki_ref[...], axis=0)
        ir, ii = lr_v / den, -li_v / den
        t_r, t_i = _cmul(ir, -ii, gkr, gki)
        gar_t, gai_t = gar_ref[...] + t_r, gai_ref[...] + t_i
        qr, qi = _cmul(kr, ki, ir, ii)
        t_r, t_i = _cmul(-qr, qi, gkr, gki)
        u_r, u_i = _cmul(dt * ar, -dt * ai, gar_t, gai_t)
        dlr_ref[...] = u_r + t_r
        dli_ref[...] = u_i + t_i
        la_r, la_i = _cmul(lr_v, li_v, ar, ai)
        w_r, _ = _cmul(la_r, -la_i, gar_t, gai_t)
        ddt = jnp.sum(w_r, axis=1, keepdims=True)
        dldt_ref[...] = jnp.broadcast_to(ddt * dt[:, 0:1], dldt_ref.shape)

    shp = jax.ShapeDtypeStruct((G, P), F32)
    return pl.pallas_call(body, name=name, out_shape=[shp, shp, jax.ShapeDtypeStruct((G, 128), F32)])(
        lr, li, ldt, gar, gai, tkr, tki)


ROWS_CALL_TILE_ELEMS = 256 * 1024


def _rows_call(name, fn, ins, outs):
    R, Cn = ins[0].shape
    tr = _pick(R, max(16, ROWS_CALL_TILE_ELEMS // Cn), 16)
    spec = pl.BlockSpec((tr, Cn), lambda i: (i, 0))

    def body(*refs):
        res = fn(*[r[...] for r in refs[:len(ins)]])
        for o_ref, r in zip(refs[len(ins):], res):
            o_ref[...] = r.astype(o_ref.dtype)

    return pl.pallas_call(
        body, name=name, grid=(R // tr,), in_specs=[spec] * len(ins), out_specs=[spec] * len(outs),
        out_shape=[jax.ShapeDtypeStruct((R, Cn), d) for d in outs], compiler_params=_params("parallel"),
    )(*ins)


def _adamw_math(w, g, m, v):
    m = ADAM_B1 * m + (1.0 - ADAM_B1) * g
    v = ADAM_B2 * v + (1.0 - ADAM_B2) * (g * g)
    m_hat = m / (1.0 - ADAM_B1 ** ADAM_STEP)
    v_hat = v / (1.0 - ADAM_B2 ** ADAM_STEP)
    delta = -ADAM_LR * (m_hat / (jnp.sqrt(v_hat) + ADAM_EPS) + ADAM_WD * w)
    return delta, m, v


def _adamw(name, w, g, m, v):
    shape = w.shape
    cols = shape[-1]
    to2d = lambda t: t.reshape(-1, cols)
    res = _rows_call(name, _adamw_math, [to2d(w), to2d(g), to2d(m), to2d(v)], [F32, F32, F32])
    return [r.reshape(shape) for r in res]


ANY = pl.BlockSpec(memory_space=pl.ANY)


def _place():
    x, y, c = lax.axis_index("x"), lax.axis_index("y"), lax.axis_index("c")
    chips = [(1 - x, y), (x, 1 - y), (1 - x, 1 - y)]
    return x, y, c, chips


def _remote(src, dst, send_sem, recv_sem, dev):
    return pltpu.make_async_remote_copy(src_ref=src, dst_ref=dst, send_sem=send_sem, recv_sem=recv_sem,
                                        device_id=dev, device_id_type=MESH)


def _gather_phases(shapes, w_refs, o_refs, send_sems, recv_sems):
    x, y, c, chips = _place()
    me_k = 2 * x + y
    sibling = (x, y, 1 - c)

    def ici(i, r):
        half = shapes[i][0] // 2
        mine = pl.ds(c * half, half)
        cx, cy = chips[r]
        return _remote(w_refs[i].at[mine], o_refs[i].at[me_k, mine],
                       send_sems.at[6 * i + r], recv_sems.at[6 * i + r], (cx, cy, c))

    def d2d(i, r, core):
        half = shapes[i][0] // 2
        cx, cy = chips[r]
        rows = o_refs[i].at[2 * cx + cy, pl.ds(core * half, half)]
        return rows, _remote(rows, rows, send_sems.at[6 * i + 3 + r], recv_sems.at[6 * i + 3 + r], sibling)

    pairs = [(i, r) for i in range(len(shapes)) for r in range(3)]

    def send():
        for i, r in pairs:
            ici(i, r).start()

    def forward():
        for i, r in pairs:
            got, fwd = d2d(i, r, c)
            _remote(got, got, send_sems.at[6 * i + r], recv_sems.at[6 * i + r], sibling).wait_recv()
            fwd.start()

    def drain():
        for i, r in pairs:
            d2d(i, r, 1 - c)[1].wait_recv()
        for i, r in pairs:
            ici(i, r).wait_send()
            d2d(i, r, c)[1].wait_send()

    return send, forward, drain


def _gather_sems(n):
    return [pltpu.SemaphoreType.DMA((6 * n,)), pltpu.SemaphoreType.DMA((6 * n,))]


def _weight_gather(name, ws):
    n = len(ws)

    def body(*refs):
        send, forward, drain = _gather_phases([w.shape for w in ws], refs[:n], refs[n:2 * n], *refs[2 * n:])
        send()
        forward()
        drain()

    return pl.pallas_call(
        body, name=name, in_specs=[ANY] * n, out_specs=[ANY] * n,
        out_shape=[jax.ShapeDtypeStruct((N_CHIPS,) + w.shape, w.dtype) for w in ws],
        scratch_shapes=_gather_sems(n),
    )(*ws)


def _tile_rows(rows, cols):
    return _pick(rows, max(16, ROWS_CALL_TILE_ELEMS // cols), 16)


def _place_own(name, own, stack, meta):
    R, Cn = own.shape
    tr = _tile_rows(R, Cn)

    def body(m_ref, own_ref, stack_ref, o_ref):
        o_ref[...] = own_ref[...]

    return pl.pallas_call(
        body, name=name, out_shape=jax.ShapeDtypeStruct(stack.shape, stack.dtype), input_output_aliases={2: 0},
        grid_spec=pltpu.PrefetchScalarGridSpec(
            num_scalar_prefetch=1, grid=(R // tr,),
            in_specs=[pl.BlockSpec((tr, Cn), lambda i, m: (i, 0)), ANY],
            out_specs=pl.BlockSpec((None, tr, Cn), lambda i, m: (m[1], i, 0))),
        compiler_params=_params("parallel"),
    )(meta, own, stack)


def _pair_exchange(gs):
    n = len(gs)

    def body(*refs):
        g_refs, r_refs = refs[:n], refs[n:2 * n]
        send_sems, recv_sems = refs[2 * n:]
        x, y, c, _ = _place()
        sibling = (x, y, 1 - c)
        cps = []
        for i in range(n):
            half = gs[i].shape[1] // 2
            cp = _remote(g_refs[i].at[:, pl.ds((1 - c) * half, half)], r_refs[i],
                         send_sems.at[i], recv_sems.at[i], sibling)
            cp.start()
            cps.append(cp)
        for cp in cps:
            cp.wait_recv()
        for cp in cps:
            cp.wait_send()

    return pl.pallas_call(
        body, name="grad_pair_exchange", in_specs=[ANY] * n, out_specs=[ANY] * n,
        out_shape=[jax.ShapeDtypeStruct((g.shape[0], g.shape[1] // 2, g.shape[2]), g.dtype) for g in gs],
        scratch_shapes=[pltpu.SemaphoreType.DMA((n,)), pltpu.SemaphoreType.DMA((n,))],
    )(*gs)


def _pair_sum(name, g, recv, meta):
    S, R, Cn = g.shape
    H = R // 2
    tr = _tile_rows(H, Cn)
    nh = H // tr

    def body(m_ref, g_ref, r_ref, p32_ref, p16_ref):
        v = g_ref[...] + r_ref[...]
        p32_ref[...] = v
        p16_ref[...] = v.astype(BF16)

    blk = pl.BlockSpec((None, tr, Cn), lambda s, i, m: (s, i, 0))
    return pl.pallas_call(
        body, name=name,
        out_shape=[jax.ShapeDtypeStruct((S, H, Cn), F32), jax.ShapeDtypeStruct((S, H, Cn), BF16)],
        grid_spec=pltpu.PrefetchScalarGridSpec(
            num_scalar_prefetch=1, grid=(S, nh),
            in_specs=[pl.BlockSpec((None, tr, Cn), lambda s, i, m: (s, m[0] * nh + i, 0)), blk],
            out_specs=[blk, blk]),
        compiler_params=_params("parallel", "parallel"),
    )(meta, g, recv)


def _chip_exchange(pbs):
    n = len(pbs)

    def body(*refs):
        pb_refs, rv_refs = refs[:n], refs[n:4 * n]
        send_sems, recv_sems = refs[4 * n:]
        x, y, c, chips = _place()
        cps = []
        for i in range(n):
            for r, (cx, cy) in enumerate(chips):
                cp = _remote(pb_refs[i].at[2 * cx + cy], rv_refs[3 * i + r],
                             send_sems.at[3 * i + r], recv_sems.at[3 * i + r], (cx, cy, c))
                cp.start()
                cps.append(cp)
        for cp in cps:
            cp.wait_recv()
        for cp in cps:
            cp.wait_send()

    res = pl.pallas_call(
        body, name="grad_chip_exchange", in_specs=[ANY] * n, out_specs=[ANY] * (3 * n),
        out_shape=[jax.ShapeDtypeStruct(pb.shape[1:], pb.dtype) for pb in pbs for _ in range(3)],
        scratch_shapes=[pltpu.SemaphoreType.DMA((3 * n,)), pltpu.SemaphoreType.DMA((3 * n,))],
    )(*pbs)
    return [res[3 * i:3 * i + 3] for i in range(n)]


def _chip_sum(name, p32, rvs, stack, layer, n_layers, meta):
    S, H, Cn = p32.shape
    tr = _tile_rows(H, Cn)
    nh = H // tr
    half = pl.BlockSpec((tr, Cn), lambda i, m: (i, 0))
    in_specs = [pl.BlockSpec((None, tr, Cn), lambda i, m: (m[1], i, 0)), half, half, half]
    args = [meta, p32, *rvs]
    aliases = {}
    if stack is not None:
        in_specs.append(ANY)
        args.append(stack)
        aliases = {len(args) - 1: 0}

    def body(m_ref, p_ref, r0_ref, r1_ref, r2_ref, *rest):
        rest[-1][...] = p_ref[...] + r0_ref[...].astype(F32) + r1_ref[...].astype(F32) + r2_ref[...].astype(F32)

    return pl.pallas_call(
        body, name=name, out_shape=jax.ShapeDtypeStruct((n_layers * 2 * H, Cn), F32), input_output_aliases=aliases,
        grid_spec=pltpu.PrefetchScalarGridSpec(
            num_scalar_prefetch=1, grid=(nh,), in_specs=in_specs,
            out_specs=pl.BlockSpec((tr, Cn), lambda i, m: (layer * 2 * nh + m[0] * nh + i, 0))),
        compiler_params=_params("parallel"),
    )(*args)


def _pair_gather(stacks, halves):
    ng = len(stacks)
    copies = [(g, l) for g in range(ng) for l in range(stacks[g].shape[0] // (2 * halves[g]))]
    n = len(copies)

    def body(*refs):
        o_refs = refs[ng:2 * ng]
        send_sems, recv_sems = refs[2 * ng:]
        x, y, c, _ = _place()
        sibling = (x, y, 1 - c)
        cps = []
        for k, (g, l) in enumerate(copies):
            H = halves[g]
            rows = o_refs[g].at[pl.ds(l * 2 * H + c * H, H)]
            cp = _remote(rows, rows, send_sems.at[k], recv_sems.at[k], sibling)
            cp.start()
            cps.append(cp)
        for k, (g, l) in enumerate(copies):
            H = halves[g]
            theirs = o_refs[g].at[pl.ds(l * 2 * H + (1 - c) * H, H)]
            _remote(theirs, theirs, send_sems.at[k], recv_sems.at[k], sibling).wait_recv()
        for cp in cps:
            cp.wait_send()

    return pl.pallas_call(
        body, name="grad_pair_gather", in_specs=[ANY] * ng, out_specs=[ANY] * ng,
        out_shape=[jax.ShapeDtypeStruct(s.shape, s.dtype) for s in stacks],
        input_output_aliases={g: g for g in range(ng)},
        scratch_shapes=[pltpu.SemaphoreType.DMA((n,)), pltpu.SemaphoreType.DMA((n,))],
    )(*stacks)


def _small_gather(name, v):
    def body(v_ref, o_ref, send_sems, recv_sems, loc_sem):
        x, y, c, chips = _place()
        me, sibling = (x, y, c), (x, y, 1 - c)

        def slot(px, py, pc):
            return o_ref.at[4 * px + 2 * py + pc]

        def copy(k, block, to, src=None):
            return _remote(slot(*block) if src is None else src, slot(*block), send_sems.at[k], recv_sems.at[k], to)

        mine = pltpu.make_async_copy(v_ref, slot(*me), loc_sem)
        mine.start()
        first = [copy(0, me, sibling, src=v_ref)]
        first += [copy(1 + j, me, (*chip, c), src=v_ref) for j, chip in enumerate(chips)]
        for cp in first:
            cp.start()
        passed = [copy(4 + j, (*chip, c), sibling) for j, chip in enumerate(chips)]
        for j, chip in enumerate(chips):
            copy(1 + j, (*chip, c), me).wait_recv()
            passed[j].start()
        copy(0, sibling, me).wait_recv()
        for j, chip in enumerate(chips):
            copy(4 + j, (*chip, 1 - c), me).wait_recv()
        for cp in first + passed:
            cp.wait_send()
        mine.wait()

    return pl.pallas_call(
        body, name=name, in_specs=[ANY], out_specs=ANY,
        out_shape=jax.ShapeDtypeStruct((N_DEV,) + v.shape, v.dtype),
        scratch_shapes=[pltpu.SemaphoreType.DMA((7,)), pltpu.SemaphoreType.DMA((7,)), pltpu.SemaphoreType.DMA],
    )(v)


def _sum_devices(name, g):
    n, R, Cn = g.shape
    tr = _pick(R, 512, 8)

    def body(g_ref, o_ref):
        acc = g_ref[0]
        for d in range(1, n):
            acc = acc + g_ref[d]
        o_ref[...] = acc

    return pl.pallas_call(
        body, name=name, grid=(R // tr,), in_specs=[pl.BlockSpec((n, tr, Cn), lambda i: (0, i, 0))],
        out_specs=pl.BlockSpec((tr, Cn), lambda i: (i, 0)), out_shape=jax.ShapeDtypeStruct((R, Cn), g.dtype),
        compiler_params=_params("parallel"),
    )(g)


WEIGHTS = ["mix_norm", "conv_w_in", "conv_b_in", "conv_dw", "conv_dw_b", "conv_ln_g", "conv_ln_b", "conv_w_out",
           "conv_b_out", "ssm_lambda_re", "ssm_lambda_im", "ssm_log_dt", "ssm_b_re", "ssm_b_im", "ssm_c_re",
           "ssm_c_im", "ssm_d", "ssm_w_glu", "mlp_norm", "mlp_w_up", "mlp_w_down", "final_norm"]
LARGE = ["conv_w_in", "conv_w_out", "ssm_w_glu", "mlp_w_up", "mlp_w_down"]
SHARDED_SMALL = ["conv_dw", "ssm_d"]
REPLICATED = [n for n in WEIGHTS if n not in LARGE and n not in SHARDED_SMALL]
PACK_QUANTUM = 8 * 128


def _pack(parts):
    rows = []
    for p in parts:
        f = p.reshape(-1)
        pad = (-f.shape[0]) % PACK_QUANTUM
        if pad:
            f = jnp.pad(f, (0, pad))
        rows.append(f.reshape(-1, 128))
    return jnp.concatenate(rows, axis=0)


def _packed_rows(shape):
    return -(-math.prod(shape) // PACK_QUANTUM) * 8


def _unpack(buf, shapes):
    out, r = [], 0
    for s in shapes:
        rows = _packed_rows(s)
        out.append(buf[r:r + rows].reshape(-1)[:math.prod(s)].reshape(s))
        r += rows
    return out


def _block_diag(t, pattern):
    return jnp.einsum(pattern, t, jnp.eye(GROUPS_PER_BLOCK, dtype=t.dtype))


def _local_step(xs, tgt, p, wg, S, own=None, meta=None):
    T, D = xs.shape
    depth = p["mix_norm"].shape[0]
    width = p["conv_dw"].shape[1]
    G, P, C = D // SSM_C, SSM_P, SSM_C
    NG = G // GROUPS_PER_BLOCK
    lanes = GROUPS_PER_BLOCK * P
    F = (own if own is not None else wg)["mlp_w_up"][0].shape[-1] * S
    tm = _pick(T, 1024, 16)
    tmh = _pick(T, 512, 16)
    tre = _pick(T, 256, 16)
    trc = _pick(T, 128, CONV_HALO)
    tcs = _pick(T, 256, 8 * SCAN_SEQS)
    row = lambda v: v.reshape(1, -1)

    def missing(units):
        return [u for u in units if wg[u[0]][u[1]] is None]

    def store(units, stacks):
        for (n, l), st in zip(units, stacks):
            wg[n][l] = _place_own(f"place_own_{n}_{l}", own[n][l], st, meta)

    def with_fetch(fn, units, n_out):
        units = missing(units)
        res = fn(gather=[own[n][l] for n, l in units])
        store(units, res[n_out:])
        return res[:n_out]

    def nn_col(name, a, wname, j, offs, extras, outs, epi, tm_, fetch=()):
        w = wg[wname][j]
        K, Ns = a.shape[1], w.shape[2]
        tn, tk = _pick(Ns, MM_OUT_TILE, 128), _pick(K, MM_K_TILE, 128)
        maps = [_w_col(0, K // tk, Ns // tn, o // tn) for o in offs]
        return with_fetch(functools.partial(_mm_nn, name, a, w, maps, extras, outs, epi, tm=tm_, tn=tn, tk=tk),
                          fetch, len(outs))

    def nn_row(name, a, wname, j, extras, outs, epi, fetch=()):
        w = wg[wname][j]
        Ks, N = a.shape[1] // S, w.shape[2]
        tn, tk = _pick(N, MM_OUT_TILE, 128), _pick(Ks, MM_K_TILE, 128)
        return with_fetch(functools.partial(_mm_nn, name, a, w, [_w_row(0, Ks // tk)], extras, outs, epi,
                                            tm=tm, tn=tn, tk=tk), fetch, len(outs))

    def nt_col(name, a, wname, j, R, extras, outs, epi):
        w = wg[wname][j]
        Cs = w.shape[2]
        tr, tc = _pick(R, MM_OUT_TILE, 128), _pick(Cs, MM_K_TILE, 128)
        return _mm_nt(name, a, w, _nt_col(0, R // tr, Cs // tc), extras, outs, epi, tm=tm, tr=tr, tc=tc)

    def nt_row(name, a, wname, j, R, extras, outs, epi):
        w = wg[wname][j]
        Rs = R // S
        tr, tc = _pick(Rs, MM_OUT_TILE, 128), _pick(w.shape[2], MM_K_TILE, 128)
        return _mm_nt(name, a, w, _nt_row(0, Rs // tr), extras, outs, epi, tm=tm, tr=tr, tc=tc)

    def tn(name, a, b, sharding):
        R, N = a.shape[1], b.shape[1]
        if sharding == "col":
            tr, tc = _pick(R, MM_OUT_TILE, 128), _pick(N // S, MM_OUT_TILE, 128)
        else:
            tr, tc = _pick(R // S, MM_OUT_TILE, 128), _pick(N, MM_OUT_TILE, 128)
        return _mm_tn(name, a, b, sharding, S, tr=tr, tc=tc, tt=_pick(T, MM_K_TILE, 128))

    ssm = []
    for j in range(p["ssm_lambda_re"].shape[0]):
        rep = lambda t: jnp.repeat(t, C, axis=0)
        lam_r, lam_i = p["ssm_lambda_re"][j], p["ssm_lambda_im"][j]
        ldt = jnp.broadcast_to(p["ssm_log_dt"][j][:, None], (G, P))
        b_r = p["ssm_b_re"][j].transpose(0, 2, 1).reshape(G * C, P)
        b_i = p["ssm_b_im"][j].transpose(0, 2, 1).reshape(G * C, P)
        ar, ai, kr, ki, bbr, bbi = _ssm_prep(f"ssm_prep_{j}", rep(lam_r), rep(lam_i), rep(ldt), b_r, b_i)
        tabf, tabr = _ssm_powers(f"ssm_powers_{j}", ar[::C].reshape(NG, 1, lanes), ai[::C].reshape(NG, 1, lanes),
                                 tcs // SCAN_SEQS)
        bd = lambda t: _block_diag(t.reshape(NG, GROUPS_PER_BLOCK, C, P), "bgcp,gh->bgchp").reshape(NG, 128, lanes)
        bbd = jnp.concatenate([bd(bbr), bd(bbi)], axis=2).astype(BF16)
        cd = lambda t: _block_diag(t.reshape(NG, GROUPS_PER_BLOCK, C, P), "bgcp,gh->bhpgc").reshape(NG, lanes, 128)
        ccd = jnp.concatenate([cd(p["ssm_c_re"][j]), -cd(p["ssm_c_im"][j])], axis=1).astype(BF16)
        ssm.append(dict(lam_r=lam_r, lam_i=lam_i, ldt=ldt, b_r=b_r, b_i=b_i, kr=kr, ki=ki, tabf=tabf, tabr=tabr,
                        bbd=bbd, ccd=ccd, dskip=row(p["ssm_d"][j])))

    dwp = [jnp.pad(p["conv_dw"][j], ((0, CONV_HALO - width), (0, 0))) for j in range(p["conv_dw"].shape[0])]

    def first_weight(i):
        if i >= depth:
            return []
        return [("conv_w_in" if i % 2 == 0 else "ssm_w_glu", i // 2)]

    units = missing(first_weight(0))
    if units:
        store(units, _weight_gather("weight_gather", [own[n][l] for n, l in units]))

    saved = []
    x = xs
    for i in range(depth):
        j = i // 2
        s = dict(x_in=x)
        h = _rms_fwd(f"mix_norm_fwd_{i}", x, row(p["mix_norm"][i]), tr=tre)
        s["h"] = h
        if i % 2 == 0:
            def epi_in(accs, ex):
                a_, g_ = accs[0] + ex[0], accs[1] + ex[1]
                return [a_, g_, a_ * _sigmoid(g_)]
            b_in = row(p["conv_b_in"][j])
            a_, g_, v = nn_col(f"conv_in_{j}", h, "conv_w_in", j, [0, D], [(b_in, "n", 0), (b_in, "n", D)],
                               [((T, D), BF16), ((T, D), BF16), ((T, D), F32)], epi_in, tmh,
                               fetch=[("conv_w_out", j), ("mlp_w_up", i)])
            sl = _conv_fwd(f"conv_fwd_{j}", v, dwp[j], row(p["conv_dw_b"][j]), row(p["conv_ln_g"][j]),
                           row(p["conv_ln_b"][j]), tr=trc, width=width)
            x = nn_row(f"conv_out_{j}", sl, "conv_w_out", j, [(row(p["conv_b_out"][j]), "n", 0), (x, "mn", 0)],
                       [((T, D), F32)], lambda accs, ex: [accs[0] + ex[0] + ex[1]])[0]
            s.update(a=a_, g=g_, v=v, s=sl)
        else:
            q = ssm[j]
            y, states, cins = _ssm_fwd(f"ssm_fwd_{j}", h, q["bbd"], q["ccd"], q["tabf"], q["dskip"], tc=tcs)
            def epi_glu(accs, ex):
                return [accs[0], accs[1], accs[0] * _sigmoid(accs[1]) + ex[0]]
            val, gate, x = nn_col(f"ssm_glu_{j}", y, "ssm_w_glu", j, [0, D], [(x, "mn", 0)],
                                  [((T, D), BF16), ((T, D), BF16), ((T, D), F32)], epi_glu, tmh,
                                  fetch=[("mlp_w_up", i)])
            s.update(y=y, states=states, cins=cins, val=val, gate=gate)
        s["x_mid"] = x
        h2 = _rms_fwd(f"mlp_norm_fwd_{i}", x, row(p["mlp_norm"][i]), tr=tre)
        def epi_up(accs, ex):
            r_ = jnp.maximum(accs[0], 0.0)
            return [r_, r_ * r_]
        r, rsq = nn_col(f"mlp_up_{i}", h2, "mlp_w_up", i, [0], [], [((T, F), BF16), ((T, F), BF16)], epi_up, tm,
                        fetch=[("mlp_w_down", i)])
        x = nn_row(f"mlp_down_{i}", rsq, "mlp_w_down", i, [(x, "mn", 0)], [((T, D), F32)],
                   lambda accs, ex: [accs[0] + ex[0]], fetch=first_weight(i + 1))[0]
        s.update(h2=h2, r=r, rsq=rsq)
        saved.append(s)

    dx, dxb, loss8, dgf = _final_loss("final_loss", x, row(p["final_norm"]), tgt, tr=tre)

    n_conv, n_ssm = p["conv_dw"].shape[0], p["ssm_d"].shape[0]
    gs = {n: [None] * p[n].shape[0] for n in WEIGHTS if n not in LARGE and n != "final_norm"}
    gs.update(conv_w_in=[None] * n_conv, conv_w_out=[None] * n_conv, ssm_w_glu=[None] * n_ssm,
              mlp_w_up=[None] * depth, mlp_w_down=[None] * depth)
    gs["final_norm"] = dgf.reshape(-1)
    for i in reversed(range(depth)):
        j = i // 2
        s = saved[i]
        dz = nt_row(f"mlp_down_dx_{i}", dxb, "mlp_w_down", i, F, [s["r"]], [((T, F), BF16)],
                    lambda acc, ex: [acc * (2.0 * ex[0].astype(F32))])[0]
        gs["mlp_w_down"][i] = tn(f"mlp_down_dw_{i}", s["rsq"], dxb, "row")
        gs["mlp_w_up"][i] = tn(f"mlp_up_dw_{i}", s["h2"], dz, "col")
        dh2 = nt_col(f"mlp_up_dx_{i}", dz, "mlp_w_up", i, D, [], [((T, D), F32)], lambda acc, ex: [acc])[0]
        dx, dxb, dg, cs = _rms_bwd(f"mlp_norm_bwd_{i}", s["x_mid"], row(p["mlp_norm"][i]), dh2, dx, tr=tre)
        gs["mlp_norm"][i] = dg.reshape(-1)
        if i % 2 == 0:
            gs["conv_b_out"][j] = cs.reshape(-1)
            gs["conv_w_out"][j] = tn(f"conv_out_dw_{j}", s["s"], dxb, "row")
            dsl = nt_row(f"conv_out_dx_{j}", dxb, "conv_w_out", j, D, [], [((T, D), F32)],
                         lambda acc, ex: [acc])[0]
            dv2, dlg, dlb, ddwb = _conv_bwd_a(f"conv_bwd_a_{j}", s["v"], dsl, dwp[j], row(p["conv_dw_b"][j]),
                                              row(p["conv_ln_g"][j]), row(p["conv_ln_b"][j]), tr=trc, width=width)
            du, ddw, dbin = _conv_bwd_b(f"conv_bwd_b_{j}", dv2, s["v"], s["a"], s["g"], dwp[j], tr=trc, width=width)
            gs["conv_ln_g"][j], gs["conv_ln_b"][j] = dlg.reshape(-1), dlb.reshape(-1)
            gs["conv_dw_b"][j], gs["conv_dw"][j], gs["conv_b_in"][j] = ddwb.reshape(-1), ddw[:width], dbin.reshape(-1)
            gs["conv_w_in"][j] = tn(f"conv_in_dw_{j}", s["h"], du, "col")
            dh = nt_col(f"conv_in_dx_{j}", du, "conv_w_in", j, D, [], [((T, D), F32)], lambda acc, ex: [acc])[0]
        else:
            q = ssm[j]
            dz2 = _glu_bwd(f"ssm_glu_bwd_{j}", dx, s["val"], s["gate"], tr=tre)
            gs["ssm_w_glu"][j] = tn(f"ssm_glu_dw_{j}", s["y"], dz2, "col")
            dy = nt_col(f"ssm_glu_dx_{j}", dz2, "ssm_w_glu", j, D, [], [((T, D), F32)], lambda acc, ex: [acc])[0]
            dh, dbbd, dccd, da, dd = _ssm_bwd(f"ssm_bwd_{j}", s["h"], dy, s["states"], s["cins"], q["bbd"], q["ccd"],
                                              q["tabr"], q["dskip"], tc=tcs)
            gbb = _block_diag(dbbd.reshape(NG, GROUPS_PER_BLOCK, C, 2, GROUPS_PER_BLOCK, P), "bgcrhp,gh->rbgcp")
            dbr, dbi, tkr, tki = _ssm_prep_bwd_b(f"ssm_prep_bwd_b_{j}", q["kr"], q["ki"], q["b_r"], q["b_i"],
                                                 gbb[0].reshape(G * C, P), gbb[1].reshape(G * C, P))
            unrow = lambda t: t.reshape(G, C, P).transpose(0, 2, 1)
            gs["ssm_b_re"][j], gs["ssm_b_im"][j] = unrow(dbr), unrow(dbi)
            per_c = lambda t: t.reshape(G, C, P).transpose(1, 0, 2)
            dlr, dli, dldt = _ssm_prep_bwd_a(f"ssm_prep_bwd_a_{j}", q["lam_r"], q["lam_i"], q["ldt"],
                                             da[:, 0, :lanes].reshape(G, P), da[:, 0, lanes:].reshape(G, P),
                                             per_c(tkr), per_c(tki))
            gs["ssm_lambda_re"][j], gs["ssm_lambda_im"][j], gs["ssm_log_dt"][j] = dlr, dli, dldt[:, 0]
            gcc = _block_diag(dccd.reshape(NG, 2, GROUPS_PER_BLOCK, P, GROUPS_PER_BLOCK, C), "brhpgc,gh->rbgcp")
            gs["ssm_c_re"][j], gs["ssm_c_im"][j] = gcc[0].reshape(G, C, P), -gcc[1].reshape(G, C, P)
            gs["ssm_d"][j] = dd.reshape(-1)
        dx, dxb, dg, _ = _rms_bwd(f"mix_norm_bwd_{i}", s["x_in"], row(p["mix_norm"][i]), dh, dx, tr=tre)
        gs["mix_norm"][i] = dg.reshape(-1)

    small = {n: (gs[n] if n == "final_norm" else jnp.stack(gs[n])) for n in WEIGHTS if n not in LARGE}
    large = {n: gs[n] for n in LARGE}
    return loss8[0, 0], dx, small, large


def kernel(x, mix_norm, conv_w_in, conv_b_in, conv_dw, conv_dw_b, conv_ln_g, conv_ln_b, conv_w_out, conv_b_out, ssm_lambda_re, ssm_lambda_im, ssm_log_dt, ssm_b_re, ssm_b_im, ssm_c_re, ssm_c_im, ssm_d, ssm_w_glu, mlp_norm, mlp_w_up, mlp_w_down, final_norm, loss_target, m_mix_norm, m_conv_w_in, m_conv_b_in, m_conv_dw, m_conv_dw_b, m_conv_ln_g, m_conv_ln_b, m_conv_w_out, m_conv_b_out, m_ssm_lambda_re, m_ssm_lambda_im, m_ssm_log_dt, m_ssm_b_re, m_ssm_b_im, m_ssm_c_re, m_ssm_c_im, m_ssm_d, m_ssm_w_glu, m_mlp_norm, m_mlp_w_up, m_mlp_w_down, m_final_norm, v_mix_norm, v_conv_w_in, v_conv_b_in, v_conv_dw, v_conv_dw_b, v_conv_ln_g, v_conv_ln_b, v_conv_w_out, v_conv_b_out, v_ssm_lambda_re, v_ssm_lambda_im, v_ssm_log_dt, v_ssm_b_re, v_ssm_b_im, v_ssm_c_re, v_ssm_c_im, v_ssm_d, v_ssm_w_glu, v_mlp_norm, v_mlp_w_up, v_mlp_w_down, v_final_norm):
    a = dict(locals())
    S = N_CHIPS
    w = {n: a[n] for n in WEIGHTS}
    k_chip = 2 * lax.axis_index("x") + lax.axis_index("y")

    meta = jnp.stack([lax.axis_index("c"), k_chip]).astype(jnp.int32)
    own = {n: [w[n][l].astype(BF16) for l in range(w[n].shape[0])] for n in LARGE}
    wg = {n: [None] * len(own[n]) for n in LARGE}
    sh_shapes = [w[n].shape for n in SHARDED_SMALL]
    sh_all = _small_gather("small_weight_gather", _pack([w[n] for n in SHARDED_SMALL]))
    per_chip = [_unpack(sh_all[2 * k], sh_shapes) for k in range(S)]
    p = {n: w[n] for n in REPLICATED}
    for idx, n in enumerate(SHARDED_SMALL):
        p[n] = jnp.concatenate([per_chip[k][idx] for k in range(S)], axis=-1)

    loss_local, dx, small, large = _local_step(x[0], loss_target[0], p, wg, S, own, meta)
    loss = lax.psum(loss_local, ("x", "y", "c"))

    order = [(n, j) for n in LARGE for j in range(len(large[n]))]
    arrs = [large[n][j] for n, j in order]
    theirs = _pair_exchange(arrs)
    sums = [_pair_sum(f"grad_pair_sum_{idx}", g, t_, meta) for idx, (g, t_) in enumerate(zip(arrs, theirs))]
    rvs = _chip_exchange([s16 for _, s16 in sums])
    stack = {n: None for n in LARGE}
    for idx, (n, j) in enumerate(order):
        stack[n] = _chip_sum(f"grad_chip_sum_{idx}", sums[idx][0], rvs[idx], stack[n], j, len(large[n]), meta)
    stacks = _pair_gather([stack[n] for n in LARGE], [large[n][0].shape[1] // 2 for n in LARGE])
    grads = {n: st.reshape(w[n].shape) for n, st in zip(LARGE, stacks)}

    small_names = REPLICATED + SHARDED_SMALL
    g_all = _small_gather("small_grad_gather", _pack([small[n] for n in small_names]))
    g_sum = _sum_devices("small_grad_sum", g_all)
    g_parts = dict(zip(small_names, _unpack(g_sum, [small[n].shape for n in small_names])))
    for n in REPLICATED:
        grads[n] = g_parts[n]
    for n in SHARDED_SMALL:
        cols = w[n].shape[-1]
        grads[n] = lax.dynamic_slice_in_dim(g_parts[n], k_chip * cols, cols, axis=g_parts[n].ndim - 1)

    delta, new_m, new_v = {}, {}, {}
    for n in LARGE + SHARDED_SMALL:
        delta[n], new_m[n], new_v[n] = _adamw(f"adamw_{n}", w[n], grads[n], a["m_" + n], a["v_" + n])
    rep_shapes = [w[n].shape for n in REPLICATED]
    rep_rows = sum(_packed_rows(s) for s in rep_shapes)
    res = _adamw("adamw_replicated", _pack([w[n] for n in REPLICATED]), g_sum[:rep_rows],
                 _pack([a["m_" + n] for n in REPLICATED]), _pack([a["v_" + n] for n in REPLICATED]))
    for dst, buf in zip((delta, new_m, new_v), res):
        dst.update(zip(REPLICATED, _unpack(buf, rep_shapes)))

    return (loss, dx[None], *[grads[n] for n in WEIGHTS], *[delta[n] for n in WEIGHTS],
            *[new_m[n] for n in WEIGHTS], *[new_v[n] for n in WEIGHTS])
```

```python
import functools
import math

import jax
import jax.numpy as jnp
from jax import lax
from jax.experimental import pallas as pl
from jax.experimental.pallas import tpu as pltpu

F32 = jnp.float32
BF16 = jnp.bfloat16
MESH = pl.DeviceIdType.MESH

EPS = 1e-6
ADAM_LR = 0.001
ADAM_B1 = 0.9
ADAM_B2 = 0.999
ADAM_EPS = 1e-08
ADAM_WD = 0.01
ADAM_STEP = 10

N_CHIPS = 4
N_DEV = 8
SSM_C = 16
SSM_P = 64
GROUPS_PER_BLOCK = 8
CONV_HALO = 32
VMEM_LIMIT = 56 * 1024 * 1024
MM_OUT_TILE = 1024
MM_K_TILE = 2048


def _pick(n, pref, align):
    t = min(n, pref)
    t -= t % align
    while t >= align:
        if n % t == 0:
            return t
        t -= align
    return n


def _params(*sem):
    return pltpu.CompilerParams(dimension_semantics=sem, vmem_limit_bytes=VMEM_LIMIT)


def _sigmoid(x):
    return 1.0 / (1.0 + jnp.exp(-x))


def _fold8(z):
    r, n = z.shape
    return jnp.sum(z.reshape(r // 8, 8, n), axis=0)


def _w_col(j, kt, nps, off):
    def idx(m, n, k):
        return ((n + off) // nps, j * kt + k, (n + off) % nps)
    return idx


def _w_row(j, kps):
    def idx(m, n, k):
        return (k // kps, j * kps + k % kps, n)
    return idx


JOB_MIDDLE_AT = 0.7


def _grid_call(name, grid, in_specs, args, out_specs, out_shape, scratch, compute, semantics, jobs=()):
    n_in, n_out, n_scr = len(args), len(out_shape), len(scratch)
    j_ins = [t for jb in jobs for t in jb.ins]
    j_outs = [t for jb in jobs for t in jb.outs]
    steps = math.prod(grid)

    def body(*refs):
        o0 = n_in + len(j_ins)
        s0 = o0 + n_out + len(j_outs)
        if jobs:
            start, middle, finish = _jobs_bind(jobs, refs[n_in:o0], refs[o0 + n_out:s0], refs[s0 + n_scr:])
            step = 0
            for d, size in enumerate(grid):
                step = step * size + pl.program_id(d)
            pl.when(step == 0)(start)
        compute(refs[:n_in], refs[o0:o0 + n_out], refs[s0:s0 + n_scr])
        if jobs:
            pl.when(step == min(int(JOB_MIDDLE_AT * steps), steps - 1))(middle)
            pl.when(step == steps - 1)(finish)

    return pl.pallas_call(
        body, name=name, grid=grid, in_specs=list(in_specs) + [ANY] * len(j_ins),
        out_specs=list(out_specs) + [ANY] * len(j_outs), out_shape=list(out_shape) + j_outs,
        scratch_shapes=list(scratch) + _jobs_sems(jobs),
        compiler_params=_params(*(("arbitrary",) * len(grid) if jobs else semantics)),
    )(*args, *j_ins)


MM_SEMANTICS = ("parallel", "parallel", "arbitrary")


def _mm_nn(name, a, w, w_maps, extras, outs, epilogue, *, tm, tn, tk, jobs=()):
    M, K = a.shape
    N = outs[0][0][1]
    grid = (M // tm, N // tn, K // tk)
    nk, nv = grid[2], len(w_maps)
    in_specs = [pl.BlockSpec((tm, tk), lambda m, n, k: (m, k))]
    args = [a]
    for wm in w_maps:
        in_specs.append(pl.BlockSpec((None, tk, tn), wm))
        args.append(w)
    for arr, kind, off in extras:
        if kind == "mn":
            in_specs.append(pl.BlockSpec((tm, tn), lambda m, n, k: (m, n)))
        else:
            in_specs.append(pl.BlockSpec((1, tn), functools.partial(lambda m, n, k, o: (0, n + o), o=off // tn)))
        args.append(arr)
    out_specs = [pl.BlockSpec((tm, tn), lambda m, n, k: (m, n)) for _ in outs]
    out_shape = [jax.ShapeDtypeStruct(s, d) for s, d in outs]

    def compute(ins, o_refs, acc_refs):
        a_ref, w_refs, e_refs = ins[0], ins[1:1 + nv], ins[1 + nv:]
        k = pl.program_id(2)

        def write(accs):
            for o_ref, r in zip(o_refs, epilogue(accs, [e[...] for e in e_refs])):
                o_ref[...] = r.astype(o_ref.dtype)

        av = a_ref[...].astype(BF16)
        if nk == 1:
            write([jnp.dot(av, w_ref[...], preferred_element_type=F32) for w_ref in w_refs])
            return

        @pl.when(k == 0)
        def _():
            for acc in acc_refs:
                acc[...] = jnp.zeros_like(acc)

        for acc, w_ref in zip(acc_refs, w_refs):
            acc[...] += jnp.dot(av, w_ref[...], preferred_element_type=F32)
        pl.when(k == nk - 1)(lambda: write([acc[...] for acc in acc_refs]))

    scratch = [pltpu.VMEM((tm, tn), F32) for _ in range(nv if nk > 1 else 0)]
    return _grid_call(name, grid, in_specs, args, out_specs, out_shape, scratch, compute, MM_SEMANTICS, jobs)


def _mm_nt(name, a, w, w_map, extras, outs, epilogue, *, tm, tr, tc, jobs=()):
    M, N = a.shape
    R = outs[0][0][1]
    grid = (M // tm, R // tr, N // tc)
    nc = grid[2]
    in_specs = [pl.BlockSpec((tm, tc), lambda m, r, c: (m, c)), pl.BlockSpec((None, tr, tc), w_map)]
    args = [a, w]
    for arr in extras:
        in_specs.append(pl.BlockSpec((tm, tr), lambda m, r, c: (m, r)))
        args.append(arr)
    out_specs = [pl.BlockSpec((tm, tr), lambda m, r, c: (m, r)) for _ in outs]
    out_shape = [jax.ShapeDtypeStruct(s, d) for s, d in outs]

    def compute(ins, o_refs, acc_refs):
        a_ref, w_ref, e_refs = ins[0], ins[1], ins[2:]

        def partial_product():
            return lax.dot_general(a_ref[...].astype(BF16), w_ref[...], (((1,), (1,)), ((), ())),
                                   preferred_element_type=F32)

        def write(acc):
            for o_ref, r in zip(o_refs, epilogue(acc, [e[...] for e in e_refs])):
                o_ref[...] = r.astype(o_ref.dtype)

        if nc == 1:
            write(partial_product())
            return
        acc = acc_refs[0]
        c = pl.program_id(2)

        @pl.when(c == 0)
        def _():
            acc[...] = jnp.zeros_like(acc)

        acc[...] += partial_product()
        pl.when(c == nc - 1)(lambda: write(acc[...]))

    scratch = [pltpu.VMEM((tm, tr), F32)] if nc > 1 else []
    return _grid_call(name, grid, in_specs, args, out_specs, out_shape, scratch, compute, MM_SEMANTICS, jobs)


def _nt_col(j, rt, cps):
    def idx(m, r, c):
        return (c // cps, j * rt + r, c % cps)
    return idx


def _nt_row(j, rps):
    def idx(m, r, c):
        return (r // rps, j * rps + r % rps, c)
    return idx


def _mm_tn(name, a, b, out_sharding, n_shards, *, tr, tc, tt, jobs=()):
    T, R = a.shape
    N = b.shape[1]
    S = n_shards
    grid = (R // tr, N // tc, T // tt)
    if out_sharding == "col":
        nps = (N // S) // tc
        out_shape = jax.ShapeDtypeStruct((S, R, N // S), F32)
        out_spec = pl.BlockSpec((None, tr, tc), lambda r, n, t: (n // nps, r, n % nps))
    else:
        rps = (R // S) // tr
        out_shape = jax.ShapeDtypeStruct((S, R // S, N), F32)
        out_spec = pl.BlockSpec((None, tr, tc), lambda r, n, t: (r // rps, r % rps, n))

    def compute(ins, outs, _):
        a_ref, b_ref, o_ref = ins[0], ins[1], outs[0]
        t = pl.program_id(2)

        @pl.when(t == 0)
        def _():
            o_ref[...] = jnp.zeros_like(o_ref)

        o_ref[...] += lax.dot_general(a_ref[...].astype(BF16), b_ref[...].astype(BF16), (((0,), (0,)), ((), ())),
                                      preferred_element_type=F32)

    in_specs = [pl.BlockSpec((tt, tr), lambda r, n, t: (t, r)), pl.BlockSpec((tt, tc), lambda r, n, t: (t, n))]
    return _grid_call(name, grid, in_specs, [a, b], [out_spec], [out_shape], [], compute, MM_SEMANTICS, jobs)


def _rms_fwd(name, x, g, *, tr):
    T, D = x.shape

    def body(x_ref, g_ref, h_ref):
        xv = x_ref[...]
        r = lax.rsqrt(jnp.mean(xv * xv, axis=-1, keepdims=True) + EPS)
        h_ref[...] = (xv * r * g_ref[...]).astype(BF16)

    return pl.pallas_call(
        body, name=name, grid=(T // tr,),
        in_specs=[pl.BlockSpec((tr, D), lambda i: (i, 0)), pl.BlockSpec((1, D), lambda i: (0, 0))],
        out_specs=pl.BlockSpec((tr, D), lambda i: (i, 0)),
        out_shape=jax.ShapeDtypeStruct((T, D), BF16),
        compiler_params=_params("parallel"),
    )(x, g)


def _rms_bwd_rows(xv, gv, dh):
    r = lax.rsqrt(jnp.mean(xv * xv, axis=-1, keepdims=True) + EPS)
    xh = xv * r
    gdy = dh * gv
    dx = r * (gdy - xh * jnp.mean(gdy * xh, axis=-1, keepdims=True))
    return dx, dh * xh


def _rms_bwd(name, x, g, dh, dx_in, *, tr):
    T, D = x.shape
    nt = T // tr

    def body(x_ref, g_ref, dh_ref, dxi_ref, dx_ref, dxb_ref, dg_ref, cs_ref, dg_acc, cs_acc):
        i = pl.program_id(0)

        @pl.when(i == 0)
        def _():
            dg_acc[...] = jnp.zeros_like(dg_acc)
            cs_acc[...] = jnp.zeros_like(cs_acc)

        dx, dgx = _rms_bwd_rows(x_ref[...], g_ref[...], dh_ref[...].astype(F32))
        dxo = dxi_ref[...] + dx
        dx_ref[...] = dxo
        dxb_ref[...] = dxo.astype(BF16)
        dg_acc[...] += _fold8(dgx)
        cs_acc[...] += _fold8(dxo)

        @pl.when(i == nt - 1)
        def _():
            dg_ref[...] = jnp.sum(dg_acc[...], axis=0, keepdims=True)
            cs_ref[...] = jnp.sum(cs_acc[...], axis=0, keepdims=True)

    row = pl.BlockSpec((tr, D), lambda i: (i, 0))
    vec = pl.BlockSpec((1, D), lambda i: (0, 0))
    return pl.pallas_call(
        body, name=name, grid=(nt,),
        in_specs=[row, vec, row, row], out_specs=[row, row, vec, vec],
        out_shape=[jax.ShapeDtypeStruct((T, D), F32), jax.ShapeDtypeStruct((T, D), BF16),
                   jax.ShapeDtypeStruct((1, D), F32), jax.ShapeDtypeStruct((1, D), F32)],
        scratch_shapes=[pltpu.VMEM((8, D), F32), pltpu.VMEM((8, D), F32)],
        compiler_params=_params("arbitrary"),
    )(x, g, dh, dx_in)


def _final_loss(name, x, g, target, *, tr):
    T, D = x.shape
    nt = T // tr

    def body(x_ref, g_ref, t_ref, dx_ref, dxb_ref, loss_ref, dg_ref, l_acc, dg_acc):
        i = pl.program_id(0)

        @pl.when(i == 0)
        def _():
            l_acc[...] = jnp.zeros_like(l_acc)
            dg_acc[...] = jnp.zeros_like(dg_acc)

        xv = x_ref[...]
        gv = g_ref[...]
        r = lax.rsqrt(jnp.mean(xv * xv, axis=-1, keepdims=True) + EPS)
        err = xv * r * gv - t_ref[...]
        l_acc[...] += _fold8(err * err)
        dx, dgx = _rms_bwd_rows(xv, gv, err * (1.0 / D))
        dx_ref[...] = dx
        dxb_ref[...] = dx.astype(BF16)
        dg_acc[...] += _fold8(dgx)

        @pl.when(i == nt - 1)
        def _():
            tot = jnp.sum(jnp.sum(l_acc[...], axis=0, keepdims=True), axis=1, keepdims=True)
            loss_ref[...] = jnp.broadcast_to(tot * (0.5 / D), loss_ref.shape)
            dg_ref[...] = jnp.sum(dg_acc[...], axis=0, keepdims=True)

    row = pl.BlockSpec((tr, D), lambda i: (i, 0))
    vec = pl.BlockSpec((1, D), lambda i: (0, 0))
    return pl.pallas_call(
        body, name=name, grid=(nt,),
        in_specs=[row, vec, row],
        out_specs=[row, row, pl.BlockSpec((8, 128), lambda i: (0, 0)), vec],
        out_shape=[jax.ShapeDtypeStruct((T, D), F32), jax.ShapeDtypeStruct((T, D), BF16),
                   jax.ShapeDtypeStruct((8, 128), F32), jax.ShapeDtypeStruct((1, D), F32)],
        scratch_shapes=[pltpu.VMEM((8, D), F32), pltpu.VMEM((8, D), F32)],
        compiler_params=_params("arbitrary"),
    )(x, g, target)


CONV_ROWS = 64
CONV_LANES = 128


def _tap_windows(ext_ref, r0, cols, first, width):
    last = first + width - 1
    for r in range(8):
        qs = [q for q in range(last // 8 + 1) if first <= 8 * q + r <= last]
        if not qs:
            continue
        n = CONV_ROWS + 8 * qs[-1] + (8 if r else 0)
        win = ext_ref[pl.ds(r0, n), cols]
        if r:
            win = pltpu.roll(win, n - r, 0)
        for q in qs:
            yield 8 * q + r - first, win[8 * q:8 * q + CONV_ROWS]


def _conv_blocks(tr, D):
    for cb in range(D // CONV_LANES):
        for rb in range(tr // CONV_ROWS):
            yield rb * CONV_ROWS, pl.ds(cb * CONV_LANES, CONV_LANES)


def _conv_taps(ext_ref, dwp_ref, out_ref, tr, width, first):
    for r0, cols in _conv_blocks(tr, out_ref.shape[1]):
        acc = None
        for j, win in _tap_windows(ext_ref, r0, cols, first, width):
            term = dwp_ref[pl.ds(j, 1), cols] * win
            acc = term if acc is None else acc + term
        out_ref[pl.ds(r0, CONV_ROWS), cols] = acc


def _ln_rows(v2, lg, lb):
    mu = jnp.mean(v2, axis=-1, keepdims=True)
    xc = v2 - mu
    rs = lax.rsqrt(jnp.mean(xc * xc, axis=-1, keepdims=True) + EPS)
    xh = xc * rs
    return xh, rs, xh * lg + lb


def _halo_prev(tr):
    q = tr // CONV_HALO
    return lambda i: (jnp.maximum(i * q - 1, 0), 0)


def _conv_fwd(name, v, dwp, dwb, lg, lb, *, tr, width):
    T, D = v.shape
    first = CONV_HALO - (width - 1)

    def body(v_ref, halo_ref, dwp_ref, dwb_ref, lg_ref, lb_ref, s_ref, ext, conv):
        i = pl.program_id(0)
        ext[pl.ds(0, CONV_HALO), :] = jnp.where(i == 0, 0.0, halo_ref[...])
        ext[pl.ds(CONV_HALO, tr), :] = v_ref[...]
        _conv_taps(ext, dwp_ref, conv, tr, width, first)
        _, _, ln = _ln_rows(conv[...] + dwb_ref[...], lg_ref[...], lb_ref[...])
        s_ref[...] = (ln * _sigmoid(ln)).astype(BF16)

    row = pl.BlockSpec((tr, D), lambda i: (i, 0))
    vec = pl.BlockSpec((1, D), lambda i: (0, 0))
    return pl.pallas_call(
        body, name=name, grid=(T // tr,),
        in_specs=[row, pl.BlockSpec((CONV_HALO, D), _halo_prev(tr)),
                  pl.BlockSpec((CONV_HALO, D), lambda i: (0, 0)), vec, vec, vec],
        out_specs=row, out_shape=jax.ShapeDtypeStruct((T, D), BF16),
        scratch_shapes=[pltpu.VMEM((tr + CONV_HALO, D), F32), pltpu.VMEM((tr, D), F32)],
        compiler_params=_params("parallel"),
    )(v, v, dwp, dwb, lg, lb)


def _conv_bwd_a(name, v, ds, dwp, dwb, lg, lb, *, tr, width):
    T, D = v.shape
    nt = T // tr
    first = CONV_HALO - (width - 1)

    def body(v_ref, halo_ref, ds_ref, dwp_ref, dwb_ref, lg_ref, lb_ref,
             dv2_ref, dlg_ref, dlb_ref, ddwb_ref, ext, conv, a_lg, a_lb, a_dwb):
        i = pl.program_id(0)

        @pl.when(i == 0)
        def _():
            a_lg[...] = jnp.zeros_like(a_lg)
            a_lb[...] = jnp.zeros_like(a_lb)
            a_dwb[...] = jnp.zeros_like(a_dwb)

        ext[pl.ds(0, CONV_HALO), :] = jnp.where(i == 0, 0.0, halo_ref[...])
        ext[pl.ds(CONV_HALO, tr), :] = v_ref[...]
        _conv_taps(ext, dwp_ref, conv, tr, width, first)
        v2 = conv[...] + dwb_ref[...]
        lgv = lg_ref[...]
        xh, rs, ln = _ln_rows(v2, lgv, lb_ref[...])
        sg = _sigmoid(ln)
        dln = ds_ref[...] * (sg * (1.0 + ln * (1.0 - sg)))
        a_lg[...] += _fold8(dln * xh)
        a_lb[...] += _fold8(dln)
        dxh = dln * lgv
        dv2 = rs * (dxh - jnp.mean(dxh, axis=-1, keepdims=True)
                    - xh * jnp.mean(dxh * xh, axis=-1, keepdims=True))
        dv2_ref[...] = dv2
        a_dwb[...] += _fold8(dv2)

        @pl.when(i == nt - 1)
        def _():
            dlg_ref[...] = jnp.sum(a_lg[...], axis=0, keepdims=True)
            dlb_ref[...] = jnp.sum(a_lb[...], axis=0, keepdims=True)
            ddwb_ref[...] = jnp.sum(a_dwb[...], axis=0, keepdims=True)

    row = pl.BlockSpec((tr, D), lambda i: (i, 0))
    vec = pl.BlockSpec((1, D), lambda i: (0, 0))
    return pl.pallas_call(
        body, name=name, grid=(nt,),
        in_specs=[row, pl.BlockSpec((CONV_HALO, D), _halo_prev(tr)), row,
                  pl.BlockSpec((CONV_HALO, D), lambda i: (0, 0)), vec, vec, vec],
        out_specs=[row, vec, vec, vec],
        out_shape=[jax.ShapeDtypeStruct((T, D), F32)] + [jax.ShapeDtypeStruct((1, D), F32)] * 3,
        scratch_shapes=[pltpu.VMEM((tr + CONV_HALO, D), F32), pltpu.VMEM((tr, D), F32)] + [pltpu.VMEM((8, D), F32)] * 3,
        compiler_params=_params("arbitrary"),
    )(v, v, ds, dwp, dwb, lg, lb)


def _conv_bwd_b(name, dv2, v, a, g, dwp, *, tr, width):
    T, D = v.shape
    nt = T // tr
    q = tr // CONV_HALO
    first = CONV_HALO - (width - 1)
    last_halo = T // CONV_HALO - 1

    def body(dv2_ref, nxt_ref, v_ref, halo_ref, a_ref, g_ref, dwp_ref,
             du_ref, ddw_ref, dbin_ref, ext_v, ext_d, dvs, a_dw, a_b):
        i = pl.program_id(0)

        @pl.when(i == 0)
        def _():
            a_dw[...] = jnp.zeros_like(a_dw)
            a_b[...] = jnp.zeros_like(a_b)

        ext_v[pl.ds(0, CONV_HALO), :] = jnp.where(i == 0, 0.0, halo_ref[...])
        ext_v[pl.ds(CONV_HALO, tr), :] = v_ref[...]
        ext_d[pl.ds(0, tr), :] = dv2_ref[...]
        ext_d[pl.ds(tr, CONV_HALO), :] = jnp.where(i == nt - 1, 0.0, nxt_ref[...])
        for r0, cols in _conv_blocks(tr, D):
            dv = None
            for o, win in _tap_windows(ext_d, r0, cols, 0, width):
                term = dwp_ref[pl.ds(width - 1 - o, 1), cols] * win
                dv = term if dv is None else dv + term
            dvs[pl.ds(r0, CONV_ROWS), cols] = dv
            d_cur = ext_d[pl.ds(r0, CONV_ROWS), cols]
            for j, win in _tap_windows(ext_v, r0, cols, first, width):
                a_dw[j, :, cols] += _fold8(d_cur * win)
        dv = dvs[...]
        av = a_ref[...].astype(F32)
        sg = _sigmoid(g_ref[...].astype(F32))
        da = dv * sg
        dg = dv * av * sg * (1.0 - sg)
        du_ref[:, pl.ds(0, D)] = da.astype(BF16)
        du_ref[:, pl.ds(D, D)] = dg.astype(BF16)
        a_b[:, pl.ds(0, D)] += _fold8(da)
        a_b[:, pl.ds(D, D)] += _fold8(dg)

        @pl.when(i == nt - 1)
        def _():
            ddw_ref[...] = jnp.sum(a_dw[...], axis=1)
            dbin_ref[...] = jnp.sum(a_b[...], axis=0, keepdims=True)

    row = pl.BlockSpec((tr, D), lambda i: (i, 0))
    return pl.pallas_call(
        body, name=name, grid=(nt,),
        in_specs=[row, pl.BlockSpec((CONV_HALO, D), lambda i: (jnp.minimum((i + 1) * q, last_halo), 0)),
                  row, pl.BlockSpec((CONV_HALO, D), _halo_prev(tr)), row, row,
                  pl.BlockSpec((CONV_HALO, D), lambda i: (0, 0))],
        out_specs=[pl.BlockSpec((tr, 2 * D), lambda i: (i, 0)),
                   pl.BlockSpec((CONV_HALO, D), lambda i: (0, 0)),
                   pl.BlockSpec((1, 2 * D), lambda i: (0, 0))],
        out_shape=[jax.ShapeDtypeStruct((T, 2 * D), BF16), jax.ShapeDtypeStruct((CONV_HALO, D), F32),
                   jax.ShapeDtypeStruct((1, 2 * D), F32)],
        scratch_shapes=[pltpu.VMEM((tr + CONV_HALO, D), F32), pltpu.VMEM((tr + CONV_HALO, D), F32),
                        pltpu.VMEM((tr, D), F32), pltpu.VMEM((CONV_HALO, 8, D), F32), pltpu.VMEM((8, 2 * D), F32)],
        compiler_params=_params("arbitrary"),
    )(dv2, dv2, v, v, a, g, dwp)


def _glu_bwd(name, dout, val, gate, *, tr):
    T, D = dout.shape

    def body(d_ref, v_ref, g_ref, dz_ref):
        d = d_ref[...]
        sg = _sigmoid(g_ref[...].astype(F32))
        dz_ref[:, pl.ds(0, D)] = (d * sg).astype(BF16)
        dz_ref[:, pl.ds(D, D)] = (d * v_ref[...].astype(F32) * sg * (1.0 - sg)).astype(BF16)

    row = pl.BlockSpec((tr, D), lambda i: (i, 0))
    return pl.pallas_call(
        body, name=name, grid=(T // tr,), in_specs=[row, row, row],
        out_specs=pl.BlockSpec((tr, 2 * D), lambda i: (i, 0)),
        out_shape=jax.ShapeDtypeStruct((T, 2 * D), BF16),
        compiler_params=_params("parallel"),
    )(dout, val, gate)


GELU_C = math.sqrt(2.0 / math.pi)
GELU_A = 0.044715


def _gelu(x):
    return 0.5 * x * (1.0 + jnp.tanh(GELU_C * (x + GELU_A * x * x * x)))


def _gelu_grad(x):
    t = jnp.tanh(GELU_C * (x + GELU_A * x * x * x))
    return 0.5 * (1.0 + t) + 0.5 * x * (1.0 - t * t) * GELU_C * (1.0 + 3.0 * GELU_A * x * x)


def _cmul(ar, ai, br, bi):
    return ar * br - ai * bi, ar * bi + ai * br


SCAN_SEQS = 8


def _interleave_rows(x, tmp_ref):
    n = x.shape[0]
    tmp_ref[...] = x
    return jnp.concatenate([tmp_ref[pl.ds(i, SCAN_SEQS, stride=n // SCAN_SEQS), :] for i in range(n // SCAN_SEQS)],
                           axis=0)


def _deinterleave_rows(x, tmp_ref):
    n = x.shape[0]
    tmp_ref[...] = x
    return jnp.concatenate([tmp_ref[pl.ds(s, n // SCAN_SEQS, stride=SCAN_SEQS), :] for s in range(SCAN_SEQS)], axis=0)


def _scan_chunk(xr, xi, tab_ref, cin_r, cin_i, *, lanes, reverse):
    n = xr.shape[0]
    L = n // SCAN_SEQS
    re, im = pl.ds(0, lanes), pl.ds(lanes, lanes)
    one, top = (L - 1, 0) if reverse else (0, L - 1)
    a_r = jnp.broadcast_to(tab_ref[pl.ds(one, 1), re], (SCAN_SEQS, lanes))
    a_i = jnp.broadcast_to(tab_ref[pl.ds(one, 1), im], (SCAN_SEQS, lanes))
    sr = si = jnp.zeros((SCAN_SEQS, lanes), F32)
    loc_r, loc_i = [None] * L, [None] * L
    for i in (reversed(range(L)) if reverse else range(L)):
        rows = slice(SCAN_SEQS * i, SCAN_SEQS * (i + 1))
        sr, si = a_r * sr - a_i * si + xr[rows], a_r * si + a_i * sr + xi[rows]
        loc_r[i], loc_i[i] = sr, si
    top_r, top_i = tab_ref[pl.ds(top, 1), re], tab_ref[pl.ds(top, 1), im]
    sub = lax.broadcasted_iota(jnp.int32, (SCAN_SEQS, lanes), 0)
    cr, ci = cin_r, cin_i
    in_r = in_i = jnp.zeros((SCAN_SEQS, lanes), F32)
    for s in (reversed(range(SCAN_SEQS)) if reverse else range(SCAN_SEQS)):
        in_r, in_i = jnp.where(sub == s, cr, in_r), jnp.where(sub == s, ci, in_i)
        dr, di = _cmul(top_r, top_i, cr, ci)
        cr, ci = dr + sr[s:s + 1], di + si[s:s + 1]
    out_r, out_i = [None] * L, [None] * L
    for i in range(L):
        dr, di = _cmul(tab_ref[pl.ds(i, 1), re], tab_ref[pl.ds(i, 1), im], in_r, in_i)
        out_r[i], out_i[i] = loc_r[i] + dr, loc_i[i] + di
    return jnp.concatenate(out_r, axis=0), jnp.concatenate(out_i, axis=0), cr, ci


def _ssm_fwd(name, h, bbd, ccd, tab, dskip, *, tc):
    T, D = h.shape
    NG, CB, L2 = bbd.shape
    lanes = L2 // 2
    nch = T // tc

    def body(h_ref, bb_ref, cc_ref, tab_ref, d_ref, y_ref, s_ref, cin_ref, carry, tmp):
        t = pl.program_id(1)

        @pl.when(t == 0)
        def _():
            carry[...] = jnp.zeros_like(carry)

        cin_ref[...] = carry[...]
        uf = _interleave_rows(h_ref[...].astype(F32), tmp)
        bu = jnp.dot(uf.astype(BF16), bb_ref[...], preferred_element_type=F32)
        sr, si, cr, ci = _scan_chunk(bu[:, :lanes], bu[:, lanes:], tab_ref,
                                     carry[pl.ds(0, 1), pl.ds(0, lanes)], carry[pl.ds(0, 1), pl.ds(lanes, lanes)],
                                     lanes=lanes, reverse=False)
        carry[pl.ds(0, 1), pl.ds(0, lanes)] = cr
        carry[pl.ds(0, 1), pl.ds(lanes, lanes)] = ci
        s = jnp.concatenate([sr, si], axis=1).astype(BF16)
        s_ref[...] = s
        yp = jnp.dot(s, cc_ref[...], preferred_element_type=F32) + d_ref[...] * uf
        y_ref[...] = _deinterleave_rows(_gelu(yp), tmp).astype(BF16)

    return pl.pallas_call(
        body, name=name, grid=(NG, nch),
        in_specs=[pl.BlockSpec((tc, CB), lambda b, t: (t, b)),
                  pl.BlockSpec((None, CB, L2), lambda b, t: (b, 0, 0)),
                  pl.BlockSpec((None, L2, CB), lambda b, t: (b, 0, 0)),
                  pl.BlockSpec((None,) + tab.shape[1:], lambda b, t: (b, 0, 0)),
                  pl.BlockSpec((1, CB), lambda b, t: (0, b))],
        out_specs=[pl.BlockSpec((tc, CB), lambda b, t: (t, b)),
                   pl.BlockSpec((tc, L2), lambda b, t: (t, b)),
                   pl.BlockSpec((None, None, 8, L2), lambda b, t: (b, t, 0, 0))],
        out_shape=[jax.ShapeDtypeStruct((T, D), BF16), jax.ShapeDtypeStruct((T, NG * L2), BF16),
                   jax.ShapeDtypeStruct((NG, nch, 8, L2), F32)],
        scratch_shapes=[pltpu.VMEM((8, L2), F32), pltpu.VMEM((tc, CB), F32)],
        compiler_params=_params("parallel", "arbitrary"),
    )(h, bbd, ccd, tab, dskip)


def _ssm_bwd(name, h, dy, states, cins, bbd, ccd, tabr, dskip, *, tc):
    T, D = h.shape
    NG, CB, L2 = bbd.shape
    lanes = L2 // 2
    nch = T // tc

    def body(h_ref, dy_ref, s_ref, cin_ref, bb_ref, cc_ref, tabr_ref, d_ref,
             dh_ref, dbb_ref, dcc_ref, da_ref, dd_ref, gcarry, a_da, a_dd, tmp):
        t = pl.program_id(1)

        @pl.when(t == 0)
        def _():
            gcarry[...] = jnp.zeros_like(gcarry)
            a_da[...] = jnp.zeros_like(a_da)
            a_dd[...] = jnp.zeros_like(a_dd)
            dbb_ref[...] = jnp.zeros_like(dbb_ref)
            dcc_ref[...] = jnp.zeros_like(dcc_ref)

        uf = _interleave_rows(h_ref[...].astype(F32), tmp)
        u = uf.astype(BF16)
        dyv = _interleave_rows(dy_ref[...], tmp)
        cin_r = cin_ref[pl.ds(0, 1), pl.ds(0, lanes)]
        cin_i = cin_ref[pl.ds(0, 1), pl.ds(lanes, lanes)]
        s = s_ref[...]
        dv = d_ref[...]
        yp = jnp.dot(s, cc_ref[...], preferred_element_type=F32) + dv * uf
        dyp = dyv * _gelu_grad(yp)
        a_dd[...] += _fold8(dyp * uf)
        dypb = dyp.astype(BF16)
        dcc_ref[...] += lax.dot_general(s, dypb, (((0,), (0,)), ((), ())), preferred_element_type=F32)
        ds = lax.dot_general(dypb, cc_ref[...], (((1,), (1,)), ((), ())), preferred_element_type=F32)
        gr, gi, cr, ci = _scan_chunk(ds[:, :lanes], ds[:, lanes:], tabr_ref,
                                     gcarry[pl.ds(0, 1), pl.ds(0, lanes)], gcarry[pl.ds(0, 1), pl.ds(lanes, lanes)],
                                     lanes=lanes, reverse=True)
        gcarry[pl.ds(0, 1), pl.ds(0, lanes)] = cr
        gcarry[pl.ds(0, 1), pl.ds(lanes, lanes)] = ci
        gb = jnp.concatenate([gr, gi], axis=1).astype(BF16)
        du = lax.dot_general(gb, bb_ref[...], (((1,), (1,)), ((), ())), preferred_element_type=F32)
        dh_ref[...] = _deinterleave_rows(du + dv * dyp, tmp)
        dbb_ref[...] += lax.dot_general(u, gb, (((0,), (0,)), ((), ())), preferred_element_type=F32)
        sr, si = s[:, :lanes].astype(F32), s[:, lanes:].astype(F32)
        first = lax.broadcasted_iota(jnp.int32, (SCAN_SEQS, lanes), 0) == 0
        head_r = jnp.where(first, cin_r, pltpu.roll(sr[tc - SCAN_SEQS:], 1, 0))
        head_i = jnp.where(first, cin_i, pltpu.roll(si[tc - SCAN_SEQS:], 1, 0))
        pr = jnp.concatenate([head_r, sr[:tc - SCAN_SEQS]], axis=0)
        pi = jnp.concatenate([head_i, si[:tc - SCAN_SEQS]], axis=0)
        a_da[:, pl.ds(0, lanes)] += _fold8(pr * gr + pi * gi)
        a_da[:, pl.ds(lanes, lanes)] += _fold8(pr * gi - pi * gr)

        @pl.when(t == nch - 1)
        def _():
            da_ref[...] = jnp.sum(a_da[...], axis=0, keepdims=True)
            dd_ref[...] = jnp.sum(a_dd[...], axis=0, keepdims=True)

    rev = lambda b, t: (nch - 1 - t, b)
    return pl.pallas_call(
        body, name=name, grid=(NG, nch),
        in_specs=[pl.BlockSpec((tc, CB), rev), pl.BlockSpec((tc, CB), rev), pl.BlockSpec((tc, L2), rev),
                  pl.BlockSpec((None, None, 8, L2), lambda b, t: (b, nch - 1 - t, 0, 0)),
                  pl.BlockSpec((None, CB, L2), lambda b, t: (b, 0, 0)),
                  pl.BlockSpec((None, L2, CB), lambda b, t: (b, 0, 0)),
                  pl.BlockSpec((None,) + tabr.shape[1:], lambda b, t: (b, 0, 0)),
                  pl.BlockSpec((1, CB), lambda b, t: (0, b))],
        out_specs=[pl.BlockSpec((tc, CB), rev),
                   pl.BlockSpec((None, CB, L2), lambda b, t: (b, 0, 0)),
                   pl.BlockSpec((None, L2, CB), lambda b, t: (b, 0, 0)),
                   pl.BlockSpec((None, 1, L2), lambda b, t: (b, 0, 0)),
                   pl.BlockSpec((1, CB), lambda b, t: (0, b))],
        out_shape=[jax.ShapeDtypeStruct((T, D), F32), jax.ShapeDtypeStruct((NG, CB, L2), F32),
                   jax.ShapeDtypeStruct((NG, L2, CB), F32), jax.ShapeDtypeStruct((NG, 1, L2), F32),
                   jax.ShapeDtypeStruct((1, D), F32)],
        scratch_shapes=[pltpu.VMEM((8, L2), F32), pltpu.VMEM((8, L2), F32), pltpu.VMEM((8, CB), F32),
                        pltpu.VMEM((tc, CB), F32)],
        compiler_params=_params("parallel", "arbitrary"),
    )(h, dy, states, cins, bbd, ccd, tabr, dskip)


def _zoh(lr, li, ldt):
    dt = jnp.exp(ldt)
    mag = jnp.exp(lr * dt)
    ar = mag * jnp.cos(li * dt)
    ai = mag * jnp.sin(li * dt)
    den = lr * lr + li * li
    nr = ar - 1.0
    kr = (nr * lr + ai * li) / den
    ki = (ai * lr - nr * li) / den
    return dt, ar, ai, kr, ki, den


def _ssm_prep(name, lr, li, ldt, br, bi):
    shp = jax.ShapeDtypeStruct(lr.shape, F32)

    def body(lr_ref, li_ref, ldt_ref, br_ref, bi_ref, ar_ref, ai_ref, kr_ref, ki_ref, bbr_ref, bbi_ref):
        _, ar, ai, kr, ki, _ = _zoh(lr_ref[...], li_ref[...], ldt_ref[...])
        ar_ref[...] = ar
        ai_ref[...] = ai
        kr_ref[...] = kr
        ki_ref[...] = ki
        bbr, bbi = _cmul(kr, ki, br_ref[...], bi_ref[...])
        bbr_ref[...] = bbr
        bbi_ref[...] = bbi

    return pl.pallas_call(body, name=name, out_shape=[shp] * 6)(lr, li, ldt, br, bi)


def _ssm_powers(name, ar, ai, rows):
    NG, _, lanes = ar.shape

    def body(ar_ref, ai_ref, tf_ref, tr_ref):
        a_r, a_i = ar_ref[...], ai_ref[...]
        pw = [(a_r, a_i)]
        for _ in range(7):
            pw.append(_cmul(pw[-1][0], pw[-1][1], a_r, a_i))
        row = lax.broadcasted_iota(jnp.int32, (8, lanes), 0)
        fr = fi = rr = ri = jnp.zeros((8, lanes), F32)
        for n in range(8):
            fr = jnp.where(row == n, pw[n][0], fr)
            fi = jnp.where(row == n, pw[n][1], fi)
            rr = jnp.where(row == 7 - n, pw[n][0], rr)
            ri = jnp.where(row == 7 - n, pw[n][1], ri)
        top_r, top_i = pw[7]
        size = 8
        while size < rows:
            hr, hi = _cmul(fr, fi, top_r, top_i)
            fr, fi = jnp.concatenate([fr, hr], axis=0), jnp.concatenate([fi, hi], axis=0)
            hr, hi = _cmul(rr, ri, top_r, top_i)
            rr, ri = jnp.concatenate([hr, rr], axis=0), jnp.concatenate([hi, ri], axis=0)
            top_r, top_i = _cmul(top_r, top_i, top_r, top_i)
            size *= 2
        tf_ref[:, pl.ds(0, lanes)] = fr
        tf_ref[:, pl.ds(lanes, lanes)] = fi
        tr_ref[:, pl.ds(0, lanes)] = rr
        tr_ref[:, pl.ds(lanes, lanes)] = -ri

    vec = pl.BlockSpec((None, 1, lanes), lambda b: (b, 0, 0))
    tab = pl.BlockSpec((None, rows, 2 * lanes), lambda b: (b, 0, 0))
    shp = jax.ShapeDtypeStruct((NG, rows, 2 * lanes), F32)
    return pl.pallas_call(body, name=name, grid=(NG,), in_specs=[vec, vec], out_specs=[tab, tab],
                          out_shape=[shp, shp], compiler_params=_params("parallel"))(ar, ai)


def _ssm_prep_bwd_b(name, kr, ki, br, bi, gbr, gbi):
    shp = jax.ShapeDtypeStruct(kr.shape, F32)

    def body(kr_ref, ki_ref, br_ref, bi_ref, gr_ref, gi_ref, dbr_ref, dbi_ref, tr_ref, ti_ref):
        gr, gi = gr_ref[...], gi_ref[...]
        dbr, dbi = _cmul(kr_ref[...], -ki_ref[...], gr, gi)
        dbr_ref[...] = dbr
        dbi_ref[...] = dbi
        t_r, t_i = _cmul(br_ref[...], -bi_ref[...], gr, gi)
        tr_ref[...] = t_r
        ti_ref[...] = t_i

    return pl.pallas_call(body, name=name, out_shape=[shp] * 4)(kr, ki, br, bi, gbr, gbi)


def _ssm_prep_bwd_a(name, lr, li, ldt, gar, gai, tkr, tki):
    G, P = lr.shape

    def body(lr_ref, li_ref, ldt_ref, gar_ref, gai_ref, tkr_ref, tki_ref, dlr_ref, dli_ref, dldt_ref):
        lr_v, li_v = lr_ref[...], li_ref[...]
        dt, ar, ai, kr, ki, den = _zoh(lr_v, li_v, ldt_ref[...])
        gkr = jnp.sum(tkr_ref[...], axis=0)
        gki = jnp.sum(tki_ref[...], axis=0)
        ir, ii = lr_v / den, -li_v / den
        t_r, t_i = _cmul(ir, -ii, gkr, gki)
        gar_t, gai_t = gar_ref[...] + t_r, gai_ref[...] + t_i
        qr, qi = _cmul(kr, ki, ir, ii)
        t_r, t_i = _cmul(-qr, qi, gkr, gki)
        u_r, u_i = _cmul(dt * ar, -dt * ai, gar_t, gai_t)
        dlr_ref[...] = u_r + t_r
        dli_ref[...] = u_i + t_i
        la_r, la_i = _cmul(lr_v, li_v, ar, ai)
        w_r, _ = _cmul(la_r, -la_i, gar_t, gai_t)
        ddt = jnp.sum(w_r, axis=1, keepdims=True)
        dldt_ref[...] = jnp.broadcast_to(ddt * dt[:, 0:1], dldt_ref.shape)

    shp = jax.ShapeDtypeStruct((G, P), F32)
    return pl.pallas_call(body, name=name, out_shape=[shp, shp, jax.ShapeDtypeStruct((G, 128), F32)])(
        lr, li, ldt, gar, gai, tkr, tki)


ROWS_CALL_TILE_ELEMS = 256 * 1024


def _rows_call(name, fn, ins, outs):
    R, Cn = ins[0].shape
    tr = _pick(R, max(16, ROWS_CALL_TILE_ELEMS // Cn), 16)
    spec = pl.BlockSpec((tr, Cn), lambda i: (i, 0))

    def body(*refs):
        res = fn(*[r[...] for r in refs[:len(ins)]])
        for o_ref, r in zip(refs[len(ins):], res):
            o_ref[...] = r.astype(o_ref.dtype)

    return pl.pallas_call(
        body, name=name, grid=(R // tr,), in_specs=[spec] * len(ins), out_specs=[spec] * len(outs),
        out_shape=[jax.ShapeDtypeStruct((R, Cn), d) for d in outs], compiler_params=_params("parallel"),
    )(*ins)


def _adamw_math(w, g, m, v):
    m = ADAM_B1 * m + (1.0 - ADAM_B1) * g
    v = ADAM_B2 * v + (1.0 - ADAM_B2) * (g * g)
    m_hat = m / (1.0 - ADAM_B1 ** ADAM_STEP)
    v_hat = v / (1.0 - ADAM_B2 ** ADAM_STEP)
    delta = -ADAM_LR * (m_hat / (jnp.sqrt(v_hat) + ADAM_EPS) + ADAM_WD * w)
    return delta, m, v


def _adamw(name, w, g, m, v):
    shape = w.shape
    cols = shape[-1]
    to2d = lambda t: t.reshape(-1, cols)
    res = _rows_call(name, _adamw_math, [to2d(w), to2d(g), to2d(m), to2d(v)], [F32, F32, F32])
    return [r.reshape(shape) for r in res]


ANY = pl.BlockSpec(memory_space=pl.ANY)


def _place():
    x, y, c = lax.axis_index("x"), lax.axis_index("y"), lax.axis_index("c")
    chips = [(1 - x, y), (x, 1 - y), (1 - x, 1 - y)]
    return x, y, c, chips


def _remote(src, dst, send_sem, recv_sem, dev):
    return pltpu.make_async_remote_copy(src_ref=src, dst_ref=dst, send_sem=send_sem, recv_sem=recv_sem,
                                        device_id=dev, device_id_type=MESH)


def _gather_phases(shapes, w_refs, o_refs, send_sems, recv_sems):
    x, y, c, chips = _place()
    me_k = 2 * x + y
    sibling = (x, y, 1 - c)

    def ici(i, r):
        half = shapes[i][0] // 2
        mine = pl.ds(c * half, half)
        cx, cy = chips[r]
        return _remote(w_refs[i].at[mine], o_refs[i].at[me_k, mine],
                       send_sems.at[6 * i + r], recv_sems.at[6 * i + r], (cx, cy, c))

    def d2d(i, r, core):
        half = shapes[i][0] // 2
        cx, cy = chips[r]
        rows = o_refs[i].at[2 * cx + cy, pl.ds(core * half, half)]
        return rows, _remote(rows, rows, send_sems.at[6 * i + 3 + r], recv_sems.at[6 * i + 3 + r], sibling)

    pairs = [(i, r) for i in range(len(shapes)) for r in range(3)]

    def send():
        for i, r in pairs:
            ici(i, r).start()

    def forward():
        for i, r in pairs:
            got, fwd = d2d(i, r, c)
            _remote(got, got, send_sems.at[6 * i + r], recv_sems.at[6 * i + r], sibling).wait_recv()
            fwd.start()

    def drain():
        for i, r in pairs:
            d2d(i, r, 1 - c)[1].wait_recv()
        for i, r in pairs:
            ici(i, r).wait_send()
            d2d(i, r, c)[1].wait_send()

    return send, forward, drain


class _Job:
    def __init__(self, ins, outs, n_sems, bind):
        self.ins, self.outs, self.n_sems, self.bind = list(ins), list(outs), n_sems, bind


def _jobs_sems(jobs):
    return [pltpu.SemaphoreType.DMA((jb.n_sems,)) for jb in jobs for _ in range(2)]


def _jobs_bind(jobs, in_refs, out_refs, sem_refs):
    bound, i, o = [], 0, 0
    for k, jb in enumerate(jobs):
        bound.append(jb.bind(in_refs[i:i + len(jb.ins)], out_refs[o:o + len(jb.outs)], sem_refs[2 * k], sem_refs[2 * k + 1]))
        i, o = i + len(jb.ins), o + len(jb.outs)

    def phase(p):
        def run():
            for b in bound:
                b[p]()
        return run

    return phase(0), phase(1), phase(2)


def _run_jobs(name, jobs):
    ins = [a for jb in jobs for a in jb.ins]
    outs = [o for jb in jobs for o in jb.outs]

    def body(*refs):
        start, middle, finish = _jobs_bind(jobs, refs[:len(ins)], refs[len(ins):len(ins) + len(outs)],
                                           refs[len(ins) + len(outs):])
        start()
        middle()
        finish()

    return pl.pallas_call(body, name=name, in_specs=[ANY] * len(ins), out_specs=[ANY] * len(outs), out_shape=outs,
                          scratch_shapes=_jobs_sems(jobs))(*ins)


def _gather_job(ws):
    shapes = [w.shape for w in ws]
    return _Job(ws, [jax.ShapeDtypeStruct((N_CHIPS,) + w.shape, w.dtype) for w in ws], 6 * len(ws),
                functools.partial(_gather_phases, shapes))


def _nothing():
    pass


def _pair_exchange_job(gs):
    def bind(g_refs, r_refs, send_sems, recv_sems):
        x, y, c, _ = _place()

        def copies():
            for i, g in enumerate(gs):
                half = g.shape[1] // 2
                yield _remote(g_refs[i].at[:, pl.ds((1 - c) * half, half)], r_refs[i],
                              send_sems.at[i], recv_sems.at[i], (x, y, 1 - c))

        def start():
            for cp in copies():
                cp.start()

        def finish():
            for cp in copies():
                cp.wait_recv()
            for cp in copies():
                cp.wait_send()

        return start, _nothing, finish

    return _Job(gs, [jax.ShapeDtypeStruct((g.shape[0], g.shape[1] // 2, g.shape[2]), g.dtype) for g in gs], len(gs), bind)


def _chip_exchange_job(pbs):
    def bind(pb_refs, rv_refs, send_sems, recv_sems):
        x, y, c, chips = _place()

        def copies():
            for i in range(len(pbs)):
                for r, (cx, cy) in enumerate(chips):
                    yield _remote(pb_refs[i].at[2 * cx + cy], rv_refs[3 * i + r],
                                  send_sems.at[3 * i + r], recv_sems.at[3 * i + r], (cx, cy, c))

        def start():
            for cp in copies():
                cp.start()

        def finish():
            for cp in copies():
                cp.wait_recv()
            for cp in copies():
                cp.wait_send()

        return start, _nothing, finish

    return _Job(pbs, [jax.ShapeDtypeStruct(pb.shape[1:], pb.dtype) for pb in pbs for _ in range(3)], 3 * len(pbs), bind)


def _tile_rows(rows, cols):
    return _pick(rows, max(16, ROWS_CALL_TILE_ELEMS // cols), 16)


def _place_own(name, own, stack, meta):
    R, Cn = own.shape
    tr = _tile_rows(R, Cn)

    def body(m_ref, own_ref, stack_ref, o_ref):
        o_ref[...] = own_ref[...]

    return pl.pallas_call(
        body, name=name, out_shape=jax.ShapeDtypeStruct(stack.shape, stack.dtype), input_output_aliases={2: 0},
        grid_spec=pltpu.PrefetchScalarGridSpec(
            num_scalar_prefetch=1, grid=(R // tr,),
            in_specs=[pl.BlockSpec((tr, Cn), lambda i, m: (i, 0)), ANY],
            out_specs=pl.BlockSpec((None, tr, Cn), lambda i, m: (m[1], i, 0))),
        compiler_params=_params("parallel"),
    )(meta, own, stack)


def _pair_sum(name, g, recv, meta):
    S, R, Cn = g.shape
    H = R // 2
    tr = _tile_rows(H, Cn)
    nh = H // tr

    def body(m_ref, g_ref, r_ref, p32_ref, p16_ref):
        v = g_ref[...] + r_ref[...]
        p32_ref[...] = v
        p16_ref[...] = v.astype(BF16)

    blk = pl.BlockSpec((None, tr, Cn), lambda s, i, m: (s, i, 0))
    return pl.pallas_call(
        body, name=name,
        out_shape=[jax.ShapeDtypeStruct((S, H, Cn), F32), jax.ShapeDtypeStruct((S, H, Cn), BF16)],
        grid_spec=pltpu.PrefetchScalarGridSpec(
            num_scalar_prefetch=1, grid=(S, nh),
            in_specs=[pl.BlockSpec((None, tr, Cn), lambda s, i, m: (s, m[0] * nh + i, 0)), blk],
            out_specs=[blk, blk]),
        compiler_params=_params("parallel", "parallel"),
    )(meta, g, recv)


def _chip_sum(name, p32, rvs, stack, layer, n_layers, meta):
    S, H, Cn = p32.shape
    tr = _tile_rows(H, Cn)
    nh = H // tr
    half = pl.BlockSpec((tr, Cn), lambda i, m: (i, 0))
    in_specs = [pl.BlockSpec((None, tr, Cn), lambda i, m: (m[1], i, 0)), half, half, half]
    args = [meta, p32, *rvs]
    aliases = {}
    if stack is not None:
        in_specs.append(ANY)
        args.append(stack)
        aliases = {len(args) - 1: 0}

    def body(m_ref, p_ref, r0_ref, r1_ref, r2_ref, *rest):
        rest[-1][...] = p_ref[...] + r0_ref[...].astype(F32) + r1_ref[...].astype(F32) + r2_ref[...].astype(F32)

    return pl.pallas_call(
        body, name=name, out_shape=jax.ShapeDtypeStruct((n_layers * 2 * H, Cn), F32), input_output_aliases=aliases,
        grid_spec=pltpu.PrefetchScalarGridSpec(
            num_scalar_prefetch=1, grid=(nh,), in_specs=in_specs,
            out_specs=pl.BlockSpec((tr, Cn), lambda i, m: (layer * 2 * nh + m[0] * nh + i, 0))),
        compiler_params=_params("parallel"),
    )(*args)


def _pair_gather(stacks, halves):
    ng = len(stacks)
    copies = [(g, l) for g in range(ng) for l in range(stacks[g].shape[0] // (2 * halves[g]))]
    n = len(copies)

    def body(*refs):
        o_refs = refs[ng:2 * ng]
        send_sems, recv_sems = refs[2 * ng:]
        x, y, c, _ = _place()
        sibling = (x, y, 1 - c)
        cps = []
        for k, (g, l) in enumerate(copies):
            H = halves[g]
            rows = o_refs[g].at[pl.ds(l * 2 * H + c * H, H)]
            cp = _remote(rows, rows, send_sems.at[k], recv_sems.at[k], sibling)
            cp.start()
            cps.append(cp)
        for k, (g, l) in enumerate(copies):
            H = halves[g]
            theirs = o_refs[g].at[pl.ds(l * 2 * H + (1 - c) * H, H)]
            _remote(theirs, theirs, send_sems.at[k], recv_sems.at[k], sibling).wait_recv()
        for cp in cps:
            cp.wait_send()

    return pl.pallas_call(
        body, name="grad_pair_gather", in_specs=[ANY] * ng, out_specs=[ANY] * ng,
        out_shape=[jax.ShapeDtypeStruct(s.shape, s.dtype) for s in stacks],
        input_output_aliases={g: g for g in range(ng)},
        scratch_shapes=[pltpu.SemaphoreType.DMA((n,)), pltpu.SemaphoreType.DMA((n,))],
    )(*stacks)


GRAD_BYTES = 704e6
ICI_SECONDS_PER_BYTE = 1.9e-3 / GRAD_BYTES
D2D_SECONDS_PER_BYTE = 0.53e-3 / GRAD_BYTES
MXU_FLOPS = 7.5e14
CARRIER_FILL = 0.8


class _GradReducer:
    def __init__(self, meta, layers):
        self.meta, self.layers = meta, dict(layers)
        self.fresh, self.summed = [], []
        self.stack = {n: None for n in layers}
        self.half = {}

    def add(self, name, layer, g):
        self.fresh.append((name, layer, g))

    def take(self, seconds):
        budget, used = CARRIER_FILL * seconds, 0.0
        jobs, plan = [], []
        while self.summed:
            cost = 2 * 4 * self.summed[0][2].size * ICI_SECONDS_PER_BYTE
            if used + cost > budget:
                break
            used += cost
            item = self.summed.pop(0)
            jobs.append(_chip_exchange_job([item[3]]))
            plan.append(("chip", item))
        if self.fresh:
            fresh, self.fresh = self.fresh, []
            jobs.append(_pair_exchange_job([g for _, _, g in fresh]))
            plan.append(("pair", fresh))
        return jobs, functools.partial(self._done, plan)

    def _done(self, plan, outs):
        i = 0
        for kind, item in plan:
            if kind == "chip":
                name, layer, p32, _ = item
                self.stack[name] = _chip_sum(f"grad_chip_sum_{name}_{layer}", p32, outs[i:i + 3], self.stack[name],
                                             layer, self.layers[name], self.meta)
                i += 3
            else:
                for (name, layer, g), recv in zip(item, outs[i:i + len(item)]):
                    p32, p16 = _pair_sum(f"grad_pair_sum_{name}_{layer}", g, recv, self.meta)
                    self.half[name] = p32.shape[1]
                    self.summed.append((name, layer, p32, p16))
                i += len(item)

    def finish(self):
        k = 0
        while self.fresh or self.summed:
            jobs, done = self.take(float("inf"))
            done(_run_jobs(f"grad_exchange_tail_{k}", jobs))
            k += 1
        names = list(self.layers)
        return dict(zip(names, _pair_gather([self.stack[n] for n in names], [self.half[n] for n in names])))


def _small_gather(name, v):
    def body(v_ref, o_ref, send_sems, recv_sems, loc_sem):
        x, y, c, chips = _place()
        me, sibling = (x, y, c), (x, y, 1 - c)

        def slot(px, py, pc):
            return o_ref.at[4 * px + 2 * py + pc]

        def copy(k, block, to, src=None):
            return _remote(slot(*block) if src is None else src, slot(*block), send_sems.at[k], recv_sems.at[k], to)

        mine = pltpu.make_async_copy(v_ref, slot(*me), loc_sem)
        mine.start()
        first = [copy(0, me, sibling, src=v_ref)]
        first += [copy(1 + j, me, (*chip, c), src=v_ref) for j, chip in enumerate(chips)]
        for cp in first:
            cp.start()
        passed = [copy(4 + j, (*chip, c), sibling) for j, chip in enumerate(chips)]
        for j, chip in enumerate(chips):
            copy(1 + j, (*chip, c), me).wait_recv()
            passed[j].start()
        copy(0, sibling, me).wait_recv()
        for j, chip in enumerate(chips):
            copy(4 + j, (*chip, 1 - c), me).wait_recv()
        for cp in first + passed:
            cp.wait_send()
        mine.wait()

    return pl.pallas_call(
        body, name=name, in_specs=[ANY], out_specs=ANY,
        out_shape=jax.ShapeDtypeStruct((N_DEV,) + v.shape, v.dtype),
        scratch_shapes=[pltpu.SemaphoreType.DMA((7,)), pltpu.SemaphoreType.DMA((7,)), pltpu.SemaphoreType.DMA],
    )(v)


def _sum_devices(name, g):
    n, R, Cn = g.shape
    tr = _pick(R, 512, 8)

    def body(g_ref, o_ref):
        acc = g_ref[0]
        for d in range(1, n):
            acc = acc + g_ref[d]
        o_ref[...] = acc

    return pl.pallas_call(
        body, name=name, grid=(R // tr,), in_specs=[pl.BlockSpec((n, tr, Cn), lambda i: (0, i, 0))],
        out_specs=pl.BlockSpec((tr, Cn), lambda i: (i, 0)), out_shape=jax.ShapeDtypeStruct((R, Cn), g.dtype),
        compiler_params=_params("parallel"),
    )(g)


WEIGHTS = ["mix_norm", "conv_w_in", "conv_b_in", "conv_dw", "conv_dw_b", "conv_ln_g", "conv_ln_b", "conv_w_out",
           "conv_b_out", "ssm_lambda_re", "ssm_lambda_im", "ssm_log_dt", "ssm_b_re", "ssm_b_im", "ssm_c_re",
           "ssm_c_im", "ssm_d", "ssm_w_glu", "mlp_norm", "mlp_w_up", "mlp_w_down", "final_norm"]
LARGE = ["conv_w_in", "conv_w_out", "ssm_w_glu", "mlp_w_up", "mlp_w_down"]
SHARDED_SMALL = ["conv_dw", "ssm_d"]
REPLICATED = [n for n in WEIGHTS if n not in LARGE and n not in SHARDED_SMALL]
PACK_QUANTUM = 8 * 128


def _pack(parts):
    rows = []
    for p in parts:
        f = p.reshape(-1)
        pad = (-f.shape[0]) % PACK_QUANTUM
        if pad:
            f = jnp.pad(f, (0, pad))
        rows.append(f.reshape(-1, 128))
    return jnp.concatenate(rows, axis=0)


def _packed_rows(shape):
    return -(-math.prod(shape) // PACK_QUANTUM) * 8


def _unpack(buf, shapes):
    out, r = [], 0
    for s in shapes:
        rows = _packed_rows(s)
        out.append(buf[r:r + rows].reshape(-1)[:math.prod(s)].reshape(s))
        r += rows
    return out


def _block_diag(t, pattern):
    return jnp.einsum(pattern, t, jnp.eye(GROUPS_PER_BLOCK, dtype=t.dtype))


def _local_step(xs, tgt, p, wg, S, own=None, meta=None, reducer=None):
    T, D = xs.shape
    depth = p["mix_norm"].shape[0]
    width = p["conv_dw"].shape[1]
    G, P, C = D // SSM_C, SSM_P, SSM_C
    NG = G // GROUPS_PER_BLOCK
    lanes = GROUPS_PER_BLOCK * P
    F = (own if own is not None else wg)["mlp_w_up"][0].shape[-1] * S
    tm = _pick(T, 1024, 16)
    tmh = _pick(T, 512, 16)
    tre = _pick(T, 256, 16)
    trc = _pick(T, 128, CONV_HALO)
    tcs = _pick(T, 256, 8 * SCAN_SEQS)
    row = lambda v: v.reshape(1, -1)

    def missing(units):
        return [u for u in units if wg[u[0]][u[1]] is None]

    def store(units, stacks):
        for (n, l), st in zip(units, stacks):
            wg[n][l] = _place_own(f"place_own_{n}_{l}", own[n][l], st, meta)

    def with_fetch(fn, units, n_out):
        units = missing(units)
        res = fn(jobs=[_gather_job([own[n][l] for n, l in units])] if units else [])
        store(units, res[n_out:])
        return res[:n_out]

    def riding(fn, flops, n_out):
        if reducer is None:
            return fn()[:n_out]
        jobs, done = reducer.take(flops / MXU_FLOPS)
        res = fn(jobs=jobs)
        done(res[n_out:])
        return res[:n_out]

    def got(name, layer, g):
        gs[name][layer] = g
        if reducer is not None:
            reducer.add(name, layer, g)

    def nn_col(name, a, wname, j, offs, extras, outs, epi, tm_, fetch=()):
        w = wg[wname][j]
        K, Ns = a.shape[1], w.shape[2]
        tn, tk = _pick(Ns, MM_OUT_TILE, 128), _pick(K, MM_K_TILE, 128)
        maps = [_w_col(0, K // tk, Ns // tn, o // tn) for o in offs]
        return with_fetch(functools.partial(_mm_nn, name, a, w, maps, extras, outs, epi, tm=tm_, tn=tn, tk=tk),
                          fetch, len(outs))

    def nn_row(name, a, wname, j, extras, outs, epi, fetch=()):
        w = wg[wname][j]
        Ks, N = a.shape[1] // S, w.shape[2]
        tn, tk = _pick(N, MM_OUT_TILE, 128), _pick(Ks, MM_K_TILE, 128)
        return with_fetch(functools.partial(_mm_nn, name, a, w, [_w_row(0, Ks // tk)], extras, outs, epi,
                                            tm=tm, tn=tn, tk=tk), fetch, len(outs))

    def nt_col(name, a, wname, j, R, extras, outs, epi):
        w = wg[wname][j]
        Cs = w.shape[2]
        tr, tc = _pick(R, MM_OUT_TILE, 128), _pick(Cs, MM_K_TILE, 128)
        return riding(functools.partial(_mm_nt, name, a, w, _nt_col(0, R // tr, Cs // tc), extras, outs, epi,
                                        tm=tm, tr=tr, tc=tc), 2 * a.shape[0] * a.shape[1] * R, len(outs))

    def nt_row(name, a, wname, j, R, extras, outs, epi):
        w = wg[wname][j]
        Rs = R // S
        tr, tc = _pick(Rs, MM_OUT_TILE, 128), _pick(w.shape[2], MM_K_TILE, 128)
        return riding(functools.partial(_mm_nt, name, a, w, _nt_row(0, Rs // tr), extras, outs, epi,
                                        tm=tm, tr=tr, tc=tc), 2 * a.shape[0] * a.shape[1] * R, len(outs))

    def tn(name, a, b, sharding):
        R, N = a.shape[1], b.shape[1]
        if sharding == "col":
            tr, tc = _pick(R, MM_OUT_TILE, 128), _pick(N // S, MM_OUT_TILE, 128)
        else:
            tr, tc = _pick(R // S, MM_OUT_TILE, 128), _pick(N, MM_OUT_TILE, 128)
        return riding(functools.partial(_mm_tn, name, a, b, sharding, S, tr=tr, tc=tc, tt=_pick(T, MM_K_TILE, 128)),
                      2 * T * R * N, 1)[0]

    ssm = []
    for j in range(p["ssm_lambda_re"].shape[0]):
        rep = lambda t: jnp.repeat(t, C, axis=0)
        lam_r, lam_i = p["ssm_lambda_re"][j], p["ssm_lambda_im"][j]
        ldt = jnp.broadcast_to(p["ssm_log_dt"][j][:, None], (G, P))
        b_r = p["ssm_b_re"][j].transpose(0, 2, 1).reshape(G * C, P)
        b_i = p["ssm_b_im"][j].transpose(0, 2, 1).reshape(G * C, P)
        ar, ai, kr, ki, bbr, bbi = _ssm_prep(f"ssm_prep_{j}", rep(lam_r), rep(lam_i), rep(ldt), b_r, b_i)
        tabf, tabr = _ssm_powers(f"ssm_powers_{j}", ar[::C].reshape(NG, 1, lanes), ai[::C].reshape(NG, 1, lanes),
                                 tcs // SCAN_SEQS)
        bd = lambda t: _block_diag(t.reshape(NG, GROUPS_PER_BLOCK, C, P), "bgcp,gh->bgchp").reshape(NG, 128, lanes)
        bbd = jnp.concatenate([bd(bbr), bd(bbi)], axis=2).astype(BF16)
        cd = lambda t: _block_diag(t.reshape(NG, GROUPS_PER_BLOCK, C, P), "bgcp,gh->bhpgc").reshape(NG, lanes, 128)
        ccd = jnp.concatenate([cd(p["ssm_c_re"][j]), -cd(p["ssm_c_im"][j])], axis=1).astype(BF16)
        ssm.append(dict(lam_r=lam_r, lam_i=lam_i, ldt=ldt, b_r=b_r, b_i=b_i, kr=kr, ki=ki, tabf=tabf, tabr=tabr,
                        bbd=bbd, ccd=ccd, dskip=row(p["ssm_d"][j])))

    dwp = [jnp.pad(p["conv_dw"][j], ((0, CONV_HALO - width), (0, 0))) for j in range(p["conv_dw"].shape[0])]

    def first_weight(i):
        if i >= depth:
            return []
        return [("conv_w_in" if i % 2 == 0 else "ssm_w_glu", i // 2)]

    def next_up(i):
        return [("mlp_w_up", i)] if i < depth else []

    units = missing(first_weight(0))
    if units:
        store(units, _run_jobs("weight_gather", [_gather_job([own[n][l] for n, l in units])]))

    saved = []
    x = xs
    for i in range(depth):
        j = i // 2
        s = dict(x_in=x)
        h = _rms_fwd(f"mix_norm_fwd_{i}", x, row(p["mix_norm"][i]), tr=tre)
        s["h"] = h
        if i % 2 == 0:
            def epi_in(accs, ex):
                a_, g_ = accs[0] + ex[0], accs[1] + ex[1]
                return [a_, g_, a_ * _sigmoid(g_)]
            b_in = row(p["conv_b_in"][j])
            a_, g_, v = nn_col(f"conv_in_{j}", h, "conv_w_in", j, [0, D], [(b_in, "n", 0), (b_in, "n", D)],
                               [((T, D), BF16), ((T, D), BF16), ((T, D), F32)], epi_in, tmh,
                               fetch=[("conv_w_out", j)] + next_up(i))
            sl = _conv_fwd(f"conv_fwd_{j}", v, dwp[j], row(p["conv_dw_b"][j]), row(p["conv_ln_g"][j]),
                           row(p["conv_ln_b"][j]), tr=trc, width=width)
            x = nn_row(f"conv_out_{j}", sl, "conv_w_out", j, [(row(p["conv_b_out"][j]), "n", 0), (x, "mn", 0)],
                       [((T, D), F32)], lambda accs, ex: [accs[0] + ex[0] + ex[1]])[0]
            s.update(a=a_, g=g_, v=v, s=sl)
        else:
            q = ssm[j]
            y, states, cins = _ssm_fwd(f"ssm_fwd_{j}", h, q["bbd"], q["ccd"], q["tabf"], q["dskip"], tc=tcs)
            def epi_glu(accs, ex):
                return [accs[0], accs[1], accs[0] * _sigmoid(accs[1]) + ex[0]]
            val, gate, x = nn_col(f"ssm_glu_{j}", y, "ssm_w_glu", j, [0, D], [(x, "mn", 0)],
                                  [((T, D), BF16), ((T, D), BF16), ((T, D), F32)], epi_glu, tmh, fetch=next_up(i))
            s.update(y=y, states=states, cins=cins, val=val, gate=gate)
        s["x_mid"] = x
        h2 = _rms_fwd(f"mlp_norm_fwd_{i}", x, row(p["mlp_norm"][i]), tr=tre)
        def epi_up(accs, ex):
            r_ = jnp.maximum(accs[0], 0.0)
            return [r_, r_ * r_]
        r, rsq = nn_col(f"mlp_up_{i}", h2, "mlp_w_up", i, [0], [], [((T, F), BF16), ((T, F), BF16)], epi_up, tm,
                        fetch=[("mlp_w_down", i)])
        x = nn_row(f"mlp_down_{i}", rsq, "mlp_w_down", i, [(x, "mn", 0)], [((T, D), F32)],
                   lambda accs, ex: [accs[0] + ex[0]], fetch=first_weight(i + 1) + next_up(i + 1))[0]
        s.update(h2=h2, r=r, rsq=rsq)
        saved.append(s)

    dx, dxb, loss8, dgf = _final_loss("final_loss", x, row(p["final_norm"]), tgt, tr=tre)

    n_conv, n_ssm = p["conv_dw"].shape[0], p["ssm_d"].shape[0]
    gs = {n: [None] * p[n].shape[0] for n in WEIGHTS if n not in LARGE and n != "final_norm"}
    gs.update(conv_w_in=[None] * n_conv, conv_w_out=[None] * n_conv, ssm_w_glu=[None] * n_ssm,
              mlp_w_up=[None] * depth, mlp_w_down=[None] * depth)
    gs["final_norm"] = dgf.reshape(-1)
    for i in reversed(range(depth)):
        j = i // 2
        s = saved[i]
        dz = nt_row(f"mlp_down_dx_{i}", dxb, "mlp_w_down", i, F, [s["r"]], [((T, F), BF16)],
                    lambda acc, ex: [acc * (2.0 * ex[0].astype(F32))])[0]
        got("mlp_w_down", i, tn(f"mlp_down_dw_{i}", s["rsq"], dxb, "row"))
        got("mlp_w_up", i, tn(f"mlp_up_dw_{i}", s["h2"], dz, "col"))
        dh2 = nt_col(f"mlp_up_dx_{i}", dz, "mlp_w_up", i, D, [], [((T, D), F32)], lambda acc, ex: [acc])[0]
        dx, dxb, dg, cs = _rms_bwd(f"mlp_norm_bwd_{i}", s["x_mid"], row(p["mlp_norm"][i]), dh2, dx, tr=tre)
        gs["mlp_norm"][i] = dg.reshape(-1)
        if i % 2 == 0:
            gs["conv_b_out"][j] = cs.reshape(-1)
            got("conv_w_out", j, tn(f"conv_out_dw_{j}", s["s"], dxb, "row"))
            dsl = nt_row(f"conv_out_dx_{j}", dxb, "conv_w_out", j, D, [], [((T, D), F32)],
                         lambda acc, ex: [acc])[0]
            dv2, dlg, dlb, ddwb = _conv_bwd_a(f"conv_bwd_a_{j}", s["v"], dsl, dwp[j], row(p["conv_dw_b"][j]),
                                              row(p["conv_ln_g"][j]), row(p["conv_ln_b"][j]), tr=trc, width=width)
            du, ddw, dbin = _conv_bwd_b(f"conv_bwd_b_{j}", dv2, s["v"], s["a"], s["g"], dwp[j], tr=trc, width=width)
            gs["conv_ln_g"][j], gs["conv_ln_b"][j] = dlg.reshape(-1), dlb.reshape(-1)
            gs["conv_dw_b"][j], gs["conv_dw"][j], gs["conv_b_in"][j] = ddwb.reshape(-1), ddw[:width], dbin.reshape(-1)
            got("conv_w_in", j, tn(f"conv_in_dw_{j}", s["h"], du, "col"))
            dh = nt_col(f"conv_in_dx_{j}", du, "conv_w_in", j, D, [], [((T, D), F32)], lambda acc, ex: [acc])[0]
        else:
            q = ssm[j]
            dz2 = _glu_bwd(f"ssm_glu_bwd_{j}", dx, s["val"], s["gate"], tr=tre)
            got("ssm_w_glu", j, tn(f"ssm_glu_dw_{j}", s["y"], dz2, "col"))
            dy = nt_col(f"ssm_glu_dx_{j}", dz2, "ssm_w_glu", j, D, [], [((T, D), F32)], lambda acc, ex: [acc])[0]
            dh, dbbd, dccd, da, dd = _ssm_bwd(f"ssm_bwd_{j}", s["h"], dy, s["states"], s["cins"], q["bbd"], q["ccd"],
                                              q["tabr"], q["dskip"], tc=tcs)
            gbb = _block_diag(dbbd.reshape(NG, GROUPS_PER_BLOCK, C, 2, GROUPS_PER_BLOCK, P), "bgcrhp,gh->rbgcp")
            dbr, dbi, tkr, tki = _ssm_prep_bwd_b(f"ssm_prep_bwd_b_{j}", q["kr"], q["ki"], q["b_r"], q["b_i"],
                                                 gbb[0].reshape(G * C, P), gbb[1].reshape(G * C, P))
            unrow = lambda t: t.reshape(G, C, P).transpose(0, 2, 1)
            gs["ssm_b_re"][j], gs["ssm_b_im"][j] = unrow(dbr), unrow(dbi)
            per_c = lambda t: t.reshape(G, C, P).transpose(1, 0, 2)
            dlr, dli, dldt = _ssm_prep_bwd_a(f"ssm_prep_bwd_a_{j}", q["lam_r"], q["lam_i"], q["ldt"],
                                             da[:, 0, :lanes].reshape(G, P), da[:, 0, lanes:].reshape(G, P),
                                             per_c(tkr), per_c(tki))
            gs["ssm_lambda_re"][j], gs["ssm_lambda_im"][j], gs["ssm_log_dt"][j] = dlr, dli, dldt[:, 0]
            gcc = _block_diag(dccd.reshape(NG, 2, GROUPS_PER_BLOCK, P, GROUPS_PER_BLOCK, C), "brhpgc,gh->rbgcp")
            gs["ssm_c_re"][j], gs["ssm_c_im"][j] = gcc[0].reshape(G, C, P), -gcc[1].reshape(G, C, P)
            gs["ssm_d"][j] = dd.reshape(-1)
        dx, dxb, dg, _ = _rms_bwd(f"mix_norm_bwd_{i}", s["x_in"], row(p["mix_norm"][i]), dh, dx, tr=tre)
        gs["mix_norm"][i] = dg.reshape(-1)

    small = {n: (gs[n] if n == "final_norm" else jnp.stack(gs[n])) for n in WEIGHTS if n not in LARGE}
    large = {n: gs[n] for n in LARGE}
    return loss8[0, 0], dx, small, large


def kernel(x, mix_norm, conv_w_in, conv_b_in, conv_dw, conv_dw_b, conv_ln_g, conv_ln_b, conv_w_out, conv_b_out, ssm_lambda_re, ssm_lambda_im, ssm_log_dt, ssm_b_re, ssm_b_im, ssm_c_re, ssm_c_im, ssm_d, ssm_w_glu, mlp_norm, mlp_w_up, mlp_w_down, final_norm, loss_target, m_mix_norm, m_conv_w_in, m_conv_b_in, m_conv_dw, m_conv_dw_b, m_conv_ln_g, m_conv_ln_b, m_conv_w_out, m_conv_b_out, m_ssm_lambda_re, m_ssm_lambda_im, m_ssm_log_dt, m_ssm_b_re, m_ssm_b_im, m_ssm_c_re, m_ssm_c_im, m_ssm_d, m_ssm_w_glu, m_mlp_norm, m_mlp_w_up, m_mlp_w_down, m_final_norm, v_mix_norm, v_conv_w_in, v_conv_b_in, v_conv_dw, v_conv_dw_b, v_conv_ln_g, v_conv_ln_b, v_conv_w_out, v_conv_b_out, v_ssm_lambda_re, v_ssm_lambda_im, v_ssm_log_dt, v_ssm_b_re, v_ssm_b_im, v_ssm_c_re, v_ssm_c_im, v_ssm_d, v_ssm_w_glu, v_mlp_norm, v_mlp_w_up, v_mlp_w_down, v_final_norm):
    a = dict(locals())
    S = N_CHIPS
    w = {n: a[n] for n in WEIGHTS}
    k_chip = 2 * lax.axis_index("x") + lax.axis_index("y")

    meta = jnp.stack([lax.axis_index("c"), k_chip]).astype(jnp.int32)
    own = {n: [w[n][l].astype(BF16) for l in range(w[n].shape[0])] for n in LARGE}
    wg = {n: [None] * len(own[n]) for n in LARGE}
    sh_shapes = [w[n].shape for n in SHARDED_SMALL]
    sh_all = _small_gather("small_weight_gather", _pack([w[n] for n in SHARDED_SMALL]))
    per_chip = [_unpack(sh_all[2 * k], sh_shapes) for k in range(S)]
    p = {n: w[n] for n in REPLICATED}
    for idx, n in enumerate(SHARDED_SMALL):
        p[n] = jnp.concatenate([per_chip[k][idx] for k in range(S)], axis=-1)

    reducer = _GradReducer(meta, {n: len(own[n]) for n in LARGE})
    loss_local, dx, small, _ = _local_step(x[0], loss_target[0], p, wg, S, own, meta, reducer)
    loss = lax.psum(loss_local, ("x", "y", "c"))
    grads = {n: st.reshape(w[n].shape) for n, st in reducer.finish().items()}

    small_names = REPLICATED + SHARDED_SMALL
    g_all = _small_gather("small_grad_gather", _pack([small[n] for n in small_names]))
    g_sum = _sum_devices("small_grad_sum", g_all)
    g_parts = dict(zip(small_names, _unpack(g_sum, [small[n].shape for n in small_names])))
    for n in REPLICATED:
        grads[n] = g_parts[n]
    for n in SHARDED_SMALL:
        cols = w[n].shape[-1]
        grads[n] = lax.dynamic_slice_in_dim(g_parts[n], k_chip * cols, cols, axis=g_parts[n].ndim - 1)

    delta, new_m, new_v = {}, {}, {}
    for n in LARGE + SHARDED_SMALL:
        delta[n], new_m[n], new_v[n] = _adamw(f"adamw_{n}", w[n], grads[n], a["m_" + n], a["v_" + n])
    rep_shapes = [w[n].shape for n in REPLICATED]
    rep_rows = sum(_packed_rows(s) for s in rep_shapes)
    res = _adamw("adamw_replicated", _pack([w[n] for n in REPLICATED]), g_sum[:rep_rows],
                 _pack([a["m_" + n] for n in REPLICATED]), _pack([a["v_" + n] for n in REPLICATED]))
    for dst, buf in zip((delta, new_m, new_v), res):
        dst.update(zip(REPLICATED, _unpack(buf, rep_shapes)))

    return (loss, dx[None], *[grads[n] for n in WEIGHTS], *[delta[n] for n in WEIGHTS],
            *[new_m[n] for n in WEIGHTS], *[new_v[n] for n in WEIGHTS])
```

```python
import functools
import math

import jax
import jax.numpy as jnp
from jax import lax
from jax.experimental import pallas as pl
from jax.experimental.pallas import tpu as pltpu

F32 = jnp.float32
BF16 = jnp.bfloat16
MESH = pl.DeviceIdType.MESH

EPS = 1e-6
ADAM_LR = 0.001
ADAM_B1 = 0.9
ADAM_B2 = 0.999
ADAM_EPS = 1e-08
ADAM_WD = 0.01
ADAM_STEP = 10

N_CHIPS = 4
N_DEV = 8
SSM_C = 16
SSM_P = 64
GROUPS_PER_BLOCK = 8
CONV_HALO = 32
VMEM_LIMIT = 56 * 1024 * 1024
MM_OUT_TILE = 1024
MM_K_TILE = 2048


def _pick(n, pref, align):
    t = min(n, pref)
    t -= t % align
    while t >= align:
        if n % t == 0:
            return t
        t -= align
    return n


def _params(*sem):
    return pltpu.CompilerParams(dimension_semantics=sem, vmem_limit_bytes=VMEM_LIMIT)


def _sigmoid(x):
    return 1.0 / (1.0 + jnp.exp(-x))


def _fold8(z):
    r, n = z.shape
    return jnp.sum(z.reshape(r // 8, 8, n), axis=0)


def _w_col(j, kt, nps, off):
    def idx(m, n, k):
        return ((n + off) // nps, j * kt + k, (n + off) % nps)
    return idx


def _w_row(j, kps):
    def idx(m, n, k):
        return (k // kps, j * kps + k % kps, n)
    return idx


JOB_MIDDLE_AT = 0.7


def _grid_call(name, grid, in_specs, args, out_specs, out_shape, scratch, compute, semantics, jobs=()):
    n_in, n_out, n_scr = len(args), len(out_shape), len(scratch)
    j_ins = [t for jb in jobs for t in jb.ins]
    j_outs = [t for jb in jobs for t in jb.outs]
    steps = math.prod(grid)

    def body(*refs):
        o0 = n_in + len(j_ins)
        s0 = o0 + n_out + len(j_outs)
        if jobs:
            start, middle, finish = _jobs_bind(jobs, refs[n_in:o0], refs[o0 + n_out:s0], refs[s0 + n_scr:])
            step = 0
            for d, size in enumerate(grid):
                step = step * size + pl.program_id(d)
            pl.when(step == 0)(start)
        compute(refs[:n_in], refs[o0:o0 + n_out], refs[s0:s0 + n_scr])
        if jobs:
            pl.when(step == min(int(JOB_MIDDLE_AT * steps), steps - 1))(middle)
            pl.when(step == steps - 1)(finish)

    return pl.pallas_call(
        body, name=name, grid=grid, in_specs=list(in_specs) + [ANY] * len(j_ins),
        out_specs=list(out_specs) + [ANY] * len(j_outs), out_shape=list(out_shape) + j_outs,
        scratch_shapes=list(scratch) + _jobs_sems(jobs),
        compiler_params=_params(*(("arbitrary",) * len(grid) if jobs else semantics)),
    )(*args, *j_ins)


MM_SEMANTICS = ("parallel", "parallel", "arbitrary")


def _mm_nn(name, a, w, w_maps, extras, outs, epilogue, *, tm, tn, tk, jobs=()):
    M, K = a.shape
    N = outs[0][0][1]
    grid = (M // tm, N // tn, K // tk)
    nk, nv = grid[2], len(w_maps)
    in_specs = [pl.BlockSpec((tm, tk), lambda m, n, k: (m, k))]
    args = [a]
    for wm in w_maps:
        in_specs.append(pl.BlockSpec((None, tk, tn), wm))
        args.append(w)
    for arr, kind, off in extras:
        if kind == "mn":
            in_specs.append(pl.BlockSpec((tm, tn), lambda m, n, k: (m, n)))
        else:
            in_specs.append(pl.BlockSpec((1, tn), functools.partial(lambda m, n, k, o: (0, n + o), o=off // tn)))
        args.append(arr)
    out_specs = [pl.BlockSpec((tm, tn), lambda m, n, k: (m, n)) for _ in outs]
    out_shape = [jax.ShapeDtypeStruct(s, d) for s, d in outs]

    def compute(ins, o_refs, acc_refs):
        a_ref, w_refs, e_refs = ins[0], ins[1:1 + nv], ins[1 + nv:]
        k = pl.program_id(2)

        def write(accs):
            for o_ref, r in zip(o_refs, epilogue(accs, [e[...] for e in e_refs])):
                o_ref[...] = r.astype(o_ref.dtype)

        av = a_ref[...].astype(BF16)
        if nk == 1:
            write([jnp.dot(av, w_ref[...], preferred_element_type=F32) for w_ref in w_refs])
            return

        @pl.when(k == 0)
        def _():
            for acc in acc_refs:
                acc[...] = jnp.zeros_like(acc)

        for acc, w_ref in zip(acc_refs, w_refs):
            acc[...] += jnp.dot(av, w_ref[...], preferred_element_type=F32)
        pl.when(k == nk - 1)(lambda: write([acc[...] for acc in acc_refs]))

    scratch = [pltpu.VMEM((tm, tn), F32) for _ in range(nv if nk > 1 else 0)]
    return _grid_call(name, grid, in_specs, args, out_specs, out_shape, scratch, compute, MM_SEMANTICS, jobs)


def _mm_nt(name, a, w, w_map, extras, outs, epilogue, *, tm, tr, tc, jobs=()):
    M, N = a.shape
    R = outs[0][0][1]
    grid = (M // tm, R // tr, N // tc)
    nc = grid[2]
    in_specs = [pl.BlockSpec((tm, tc), lambda m, r, c: (m, c)), pl.BlockSpec((None, tr, tc), w_map)]
    args = [a, w]
    for arr in extras:
        in_specs.append(pl.BlockSpec((tm, tr), lambda m, r, c: (m, r)))
        args.append(arr)
    out_specs = [pl.BlockSpec((tm, tr), lambda m, r, c: (m, r)) for _ in outs]
    out_shape = [jax.ShapeDtypeStruct(s, d) for s, d in outs]

    def compute(ins, o_refs, acc_refs):
        a_ref, w_ref, e_refs = ins[0], ins[1], ins[2:]

        def partial_product():
            return lax.dot_general(a_ref[...].astype(BF16), w_ref[...], (((1,), (1,)), ((), ())),
                                   preferred_element_type=F32)

        def write(acc):
            for o_ref, r in zip(o_refs, epilogue(acc, [e[...] for e in e_refs])):
                o_ref[...] = r.astype(o_ref.dtype)

        if nc == 1:
            write(partial_product())
            return
        acc = acc_refs[0]
        c = pl.program_id(2)

        @pl.when(c == 0)
        def _():
            acc[...] = jnp.zeros_like(acc)

        acc[...] += partial_product()
        pl.when(c == nc - 1)(lambda: write(acc[...]))

    scratch = [pltpu.VMEM((tm, tr), F32)] if nc > 1 else []
    return _grid_call(name, grid, in_specs, args, out_specs, out_shape, scratch, compute, MM_SEMANTICS, jobs)


def _nt_col(j, rt, cps):
    def idx(m, r, c):
        return (c // cps, j * rt + r, c % cps)
    return idx


def _nt_row(j, rps):
    def idx(m, r, c):
        return (r // rps, j * rps + r % rps, c)
    return idx


def _mm_tn(name, a, b, out_sharding, n_shards, *, tr, tc, tt, jobs=()):
    T, R = a.shape
    N = b.shape[1]
    S = n_shards
    grid = (R // tr, N // tc, T // tt)
    if out_sharding == "col":
        nps = (N // S) // tc
        out_shape = jax.ShapeDtypeStruct((S, R, N // S), F32)
        out_spec = pl.BlockSpec((None, tr, tc), lambda r, n, t: (n // nps, r, n % nps))
    else:
        rps = (R // S) // tr
        out_shape = jax.ShapeDtypeStruct((S, R // S, N), F32)
        out_spec = pl.BlockSpec((None, tr, tc), lambda r, n, t: (r // rps, r % rps, n))

    def compute(ins, outs, _):
        a_ref, b_ref, o_ref = ins[0], ins[1], outs[0]
        t = pl.program_id(2)

        @pl.when(t == 0)
        def _():
            o_ref[...] = jnp.zeros_like(o_ref)

        o_ref[...] += lax.dot_general(a_ref[...].astype(BF16), b_ref[...].astype(BF16), (((0,), (0,)), ((), ())),
                                      preferred_element_type=F32)

    in_specs = [pl.BlockSpec((tt, tr), lambda r, n, t: (t, r)), pl.BlockSpec((tt, tc), lambda r, n, t: (t, n))]
    return _grid_call(name, grid, in_specs, [a, b], [out_spec], [out_shape], [], compute, MM_SEMANTICS, jobs)


def _rms_fwd(name, x, g, *, tr):
    T, D = x.shape

    def body(x_ref, g_ref, h_ref):
        xv = x_ref[...]
        r = lax.rsqrt(jnp.mean(xv * xv, axis=-1, keepdims=True) + EPS)
        h_ref[...] = (xv * r * g_ref[...]).astype(BF16)

    return pl.pallas_call(
        body, name=name, grid=(T // tr,),
        in_specs=[pl.BlockSpec((tr, D), lambda i: (i, 0)), pl.BlockSpec((1, D), lambda i: (0, 0))],
        out_specs=pl.BlockSpec((tr, D), lambda i: (i, 0)),
        out_shape=jax.ShapeDtypeStruct((T, D), BF16),
        compiler_params=_params("parallel"),
    )(x, g)


def _rms_bwd_rows(xv, gv, dh):
    r = lax.rsqrt(jnp.mean(xv * xv, axis=-1, keepdims=True) + EPS)
    xh = xv * r
    gdy = dh * gv
    dx = r * (gdy - xh * jnp.mean(gdy * xh, axis=-1, keepdims=True))
    return dx, dh * xh


def _rms_bwd(name, x, g, dh, dx_in, *, tr):
    T, D = x.shape
    nt = T // tr

    def body(x_ref, g_ref, dh_ref, dxi_ref, dx_ref, dxb_ref, dg_ref, cs_ref, dg_acc, cs_acc):
        i = pl.program_id(0)

        @pl.when(i == 0)
        def _():
            dg_acc[...] = jnp.zeros_like(dg_acc)
            cs_acc[...] = jnp.zeros_like(cs_acc)

        dx, dgx = _rms_bwd_rows(x_ref[...], g_ref[...], dh_ref[...].astype(F32))
        dxo = dxi_ref[...] + dx
        dx_ref[...] = dxo
        dxb_ref[...] = dxo.astype(BF16)
        dg_acc[...] += _fold8(dgx)
        cs_acc[...] += _fold8(dxo)

        @pl.when(i == nt - 1)
        def _():
            dg_ref[...] = jnp.sum(dg_acc[...], axis=0, keepdims=True)
            cs_ref[...] = jnp.sum(cs_acc[...], axis=0, keepdims=True)

    row = pl.BlockSpec((tr, D), lambda i: (i, 0))
    vec = pl.BlockSpec((1, D), lambda i: (0, 0))
    return pl.pallas_call(
        body, name=name, grid=(nt,),
        in_specs=[row, vec, row, row], out_specs=[row, row, vec, vec],
        out_shape=[jax.ShapeDtypeStruct((T, D), F32), jax.ShapeDtypeStruct((T, D), BF16),
                   jax.ShapeDtypeStruct((1, D), F32), jax.ShapeDtypeStruct((1, D), F32)],
        scratch_shapes=[pltpu.VMEM((8, D), F32), pltpu.VMEM((8, D), F32)],
        compiler_params=_params("arbitrary"),
    )(x, g, dh, dx_in)


def _final_loss(name, x, g, target, *, tr):
    T, D = x.shape
    nt = T // tr

    def body(x_ref, g_ref, t_ref, dx_ref, dxb_ref, loss_ref, dg_ref, l_acc, dg_acc):
        i = pl.program_id(0)

        @pl.when(i == 0)
        def _():
            l_acc[...] = jnp.zeros_like(l_acc)
            dg_acc[...] = jnp.zeros_like(dg_acc)

        xv = x_ref[...]
        gv = g_ref[...]
        r = lax.rsqrt(jnp.mean(xv * xv, axis=-1, keepdims=True) + EPS)
        err = xv * r * gv - t_ref[...]
        l_acc[...] += _fold8(err * err)
        dx, dgx = _rms_bwd_rows(xv, gv, err * (1.0 / D))
        dx_ref[...] = dx
        dxb_ref[...] = dx.astype(BF16)
        dg_acc[...] += _fold8(dgx)

        @pl.when(i == nt - 1)
        def _():
            tot = jnp.sum(jnp.sum(l_acc[...], axis=0, keepdims=True), axis=1, keepdims=True)
            loss_ref[...] = jnp.broadcast_to(tot * (0.5 / D), loss_ref.shape)
            dg_ref[...] = jnp.sum(dg_acc[...], axis=0, keepdims=True)

    row = pl.BlockSpec((tr, D), lambda i: (i, 0))
    vec = pl.BlockSpec((1, D), lambda i: (0, 0))
    return pl.pallas_call(
        body, name=name, grid=(nt,),
        in_specs=[row, vec, row],
        out_specs=[row, row, pl.BlockSpec((8, 128), lambda i: (0, 0)), vec],
        out_shape=[jax.ShapeDtypeStruct((T, D), F32), jax.ShapeDtypeStruct((T, D), BF16),
                   jax.ShapeDtypeStruct((8, 128), F32), jax.ShapeDtypeStruct((1, D), F32)],
        scratch_shapes=[pltpu.VMEM((8, D), F32), pltpu.VMEM((8, D), F32)],
        compiler_params=_params("arbitrary"),
    )(x, g, target)


CONV_ROWS = 64
CONV_LANES = 128


def _tap_windows(ext_ref, r0, cols, first, width):
    last = first + width - 1
    for r in range(8):
        qs = [q for q in range(last // 8 + 1) if first <= 8 * q + r <= last]
        if not qs:
            continue
        n = CONV_ROWS + 8 * qs[-1] + (8 if r else 0)
        win = ext_ref[pl.ds(r0, n), cols]
        if r:
            win = pltpu.roll(win, n - r, 0)
        for q in qs:
            yield 8 * q + r - first, win[8 * q:8 * q + CONV_ROWS]


def _conv_blocks(tr, D):
    for cb in range(D // CONV_LANES):
        for rb in range(tr // CONV_ROWS):
            yield rb * CONV_ROWS, pl.ds(cb * CONV_LANES, CONV_LANES)


def _conv_taps(ext_ref, dwp_ref, out_ref, tr, width, first):
    for r0, cols in _conv_blocks(tr, out_ref.shape[1]):
        acc = None
        for j, win in _tap_windows(ext_ref, r0, cols, first, width):
            term = dwp_ref[pl.ds(j, 1), cols] * win
            acc = term if acc is None else acc + term
        out_ref[pl.ds(r0, CONV_ROWS), cols] = acc


def _ln_rows(v2, lg, lb):
    mu = jnp.mean(v2, axis=-1, keepdims=True)
    xc = v2 - mu
    rs = lax.rsqrt(jnp.mean(xc * xc, axis=-1, keepdims=True) + EPS)
    xh = xc * rs
    return xh, rs, xh * lg + lb


def _halo_prev(tr):
    q = tr // CONV_HALO
    return lambda i: (jnp.maximum(i * q - 1, 0), 0)


def _conv_fwd(name, v, dwp, dwb, lg, lb, *, tr, width):
    T, D = v.shape
    first = CONV_HALO - (width - 1)

    def body(v_ref, halo_ref, dwp_ref, dwb_ref, lg_ref, lb_ref, s_ref, ext, conv):
        i = pl.program_id(0)
        ext[pl.ds(0, CONV_HALO), :] = jnp.where(i == 0, 0.0, halo_ref[...])
        ext[pl.ds(CONV_HALO, tr), :] = v_ref[...]
        _conv_taps(ext, dwp_ref, conv, tr, width, first)
        _, _, ln = _ln_rows(conv[...] + dwb_ref[...], lg_ref[...], lb_ref[...])
        s_ref[...] = (ln * _sigmoid(ln)).astype(BF16)

    row = pl.BlockSpec((tr, D), lambda i: (i, 0))
    vec = pl.BlockSpec((1, D), lambda i: (0, 0))
    return pl.pallas_call(
        body, name=name, grid=(T // tr,),
        in_specs=[row, pl.BlockSpec((CONV_HALO, D), _halo_prev(tr)),
                  pl.BlockSpec((CONV_HALO, D), lambda i: (0, 0)), vec, vec, vec],
        out_specs=row, out_shape=jax.ShapeDtypeStruct((T, D), BF16),
        scratch_shapes=[pltpu.VMEM((tr + CONV_HALO, D), F32), pltpu.VMEM((tr, D), F32)],
        compiler_params=_params("parallel"),
    )(v, v, dwp, dwb, lg, lb)


def _conv_bwd_a(name, v, ds, dwp, dwb, lg, lb, *, tr, width):
    T, D = v.shape
    nt = T // tr
    first = CONV_HALO - (width - 1)

    def body(v_ref, halo_ref, ds_ref, dwp_ref, dwb_ref, lg_ref, lb_ref,
             dv2_ref, dlg_ref, dlb_ref, ddwb_ref, ext, conv, a_lg, a_lb, a_dwb):
        i = pl.program_id(0)

        @pl.when(i == 0)
        def _():
            a_lg[...] = jnp.zeros_like(a_lg)
            a_lb[...] = jnp.zeros_like(a_lb)
            a_dwb[...] = jnp.zeros_like(a_dwb)

        ext[pl.ds(0, CONV_HALO), :] = jnp.where(i == 0, 0.0, halo_ref[...])
        ext[pl.ds(CONV_HALO, tr), :] = v_ref[...]
        _conv_taps(ext, dwp_ref, conv, tr, width, first)
        v2 = conv[...] + dwb_ref[...]
        lgv = lg_ref[...]
        xh, rs, ln = _ln_rows(v2, lgv, lb_ref[...])
        sg = _sigmoid(ln)
        dln = ds_ref[...] * (sg * (1.0 + ln * (1.0 - sg)))
        a_lg[...] += _fold8(dln * xh)
        a_lb[...] += _fold8(dln)
        dxh = dln * lgv
        dv2 = rs * (dxh - jnp.mean(dxh, axis=-1, keepdims=True)
                    - xh * jnp.mean(dxh * xh, axis=-1, keepdims=True))
        dv2_ref[...] = dv2
        a_dwb[...] += _fold8(dv2)

        @pl.when(i == nt - 1)
        def _():
            dlg_ref[...] = jnp.sum(a_lg[...], axis=0, keepdims=True)
            dlb_ref[...] = jnp.sum(a_lb[...], axis=0, keepdims=True)
            ddwb_ref[...] = jnp.sum(a_dwb[...], axis=0, keepdims=True)

    row = pl.BlockSpec((tr, D), lambda i: (i, 0))
    vec = pl.BlockSpec((1, D), lambda i: (0, 0))
    return pl.pallas_call(
        body, name=name, grid=(nt,),
        in_specs=[row, pl.BlockSpec((CONV_HALO, D), _halo_prev(tr)), row,
                  pl.BlockSpec((CONV_HALO, D), lambda i: (0, 0)), vec, vec, vec],
        out_specs=[row, vec, vec, vec],
        out_shape=[jax.ShapeDtypeStruct((T, D), F32)] + [jax.ShapeDtypeStruct((1, D), F32)] * 3,
        scratch_shapes=[pltpu.VMEM((tr + CONV_HALO, D), F32), pltpu.VMEM((tr, D), F32)] + [pltpu.VMEM((8, D), F32)] * 3,
        compiler_params=_params("arbitrary"),
    )(v, v, ds, dwp, dwb, lg, lb)


def _conv_bwd_b(name, dv2, v, a, g, dwp, *, tr, width):
    T, D = v.shape
    nt = T // tr
    q = tr // CONV_HALO
    first = CONV_HALO - (width - 1)
    last_halo = T // CONV_HALO - 1

    def body(dv2_ref, nxt_ref, v_ref, halo_ref, a_ref, g_ref, dwp_ref,
             du_ref, ddw_ref, dbin_ref, ext_v, ext_d, dvs, a_dw, a_b):
        i = pl.program_id(0)

        @pl.when(i == 0)
        def _():
            a_dw[...] = jnp.zeros_like(a_dw)
            a_b[...] = jnp.zeros_like(a_b)

        ext_v[pl.ds(0, CONV_HALO), :] = jnp.where(i == 0, 0.0, halo_ref[...])
        ext_v[pl.ds(CONV_HALO, tr), :] = v_ref[...]
        ext_d[pl.ds(0, tr), :] = dv2_ref[...]
        ext_d[pl.ds(tr, CONV_HALO), :] = jnp.where(i == nt - 1, 0.0, nxt_ref[...])
        for r0, cols in _conv_blocks(tr, D):
            dv = None
            for o, win in _tap_windows(ext_d, r0, cols, 0, width):
                term = dwp_ref[pl.ds(width - 1 - o, 1), cols] * win
                dv = term if dv is None else dv + term
            dvs[pl.ds(r0, CONV_ROWS), cols] = dv
            d_cur = ext_d[pl.ds(r0, CONV_ROWS), cols]
            for j, win in _tap_windows(ext_v, r0, cols, first, width):
                a_dw[j, :, cols] += _fold8(d_cur * win)
        dv = dvs[...]
        av = a_ref[...].astype(F32)
        sg = _sigmoid(g_ref[...].astype(F32))
        da = dv * sg
        dg = dv * av * sg * (1.0 - sg)
        du_ref[:, pl.ds(0, D)] = da.astype(BF16)
        du_ref[:, pl.ds(D, D)] = dg.astype(BF16)
        a_b[:, pl.ds(0, D)] += _fold8(da)
        a_b[:, pl.ds(D, D)] += _fold8(dg)

        @pl.when(i == nt - 1)
        def _():
            ddw_ref[...] = jnp.sum(a_dw[...], axis=1)
            dbin_ref[...] = jnp.sum(a_b[...], axis=0, keepdims=True)

    row = pl.BlockSpec((tr, D), lambda i: (i, 0))
    return pl.pallas_call(
        body, name=name, grid=(nt,),
        in_specs=[row, pl.BlockSpec((CONV_HALO, D), lambda i: (jnp.minimum((i + 1) * q, last_halo), 0)),
                  row, pl.BlockSpec((CONV_HALO, D), _halo_prev(tr)), row, row,
                  pl.BlockSpec((CONV_HALO, D), lambda i: (0, 0))],
        out_specs=[pl.BlockSpec((tr, 2 * D), lambda i: (i, 0)),
                   pl.BlockSpec((CONV_HALO, D), lambda i: (0, 0)),
                   pl.BlockSpec((1, 2 * D), lambda i: (0, 0))],
        out_shape=[jax.ShapeDtypeStruct((T, 2 * D), BF16), jax.ShapeDtypeStruct((CONV_HALO, D), F32),
                   jax.ShapeDtypeStruct((1, 2 * D), F32)],
        scratch_shapes=[pltpu.VMEM((tr + CONV_HALO, D), F32), pltpu.VMEM((tr + CONV_HALO, D), F32),
                        pltpu.VMEM((tr, D), F32), pltpu.VMEM((CONV_HALO, 8, D), F32), pltpu.VMEM((8, 2 * D), F32)],
        compiler_params=_params("arbitrary"),
    )(dv2, dv2, v, v, a, g, dwp)


def _glu_bwd(name, dout, val, gate, *, tr):
    T, D = dout.shape

    def body(d_ref, v_ref, g_ref, dz_ref):
        d = d_ref[...]
        sg = _sigmoid(g_ref[...].astype(F32))
        dz_ref[:, pl.ds(0, D)] = (d * sg).astype(BF16)
        dz_ref[:, pl.ds(D, D)] = (d * v_ref[...].astype(F32) * sg * (1.0 - sg)).astype(BF16)

    row = pl.BlockSpec((tr, D), lambda i: (i, 0))
    return pl.pallas_call(
        body, name=name, grid=(T // tr,), in_specs=[row, row, row],
        out_specs=pl.BlockSpec((tr, 2 * D), lambda i: (i, 0)),
        out_shape=jax.ShapeDtypeStruct((T, 2 * D), BF16),
        compiler_params=_params("parallel"),
    )(dout, val, gate)


GELU_C = math.sqrt(2.0 / math.pi)
GELU_A = 0.044715


def _gelu(x):
    return 0.5 * x * (1.0 + jnp.tanh(GELU_C * (x + GELU_A * x * x * x)))


def _gelu_grad(x):
    t = jnp.tanh(GELU_C * (x + GELU_A * x * x * x))
    return 0.5 * (1.0 + t) + 0.5 * x * (1.0 - t * t) * GELU_C * (1.0 + 3.0 * GELU_A * x * x)


def _cmul(ar, ai, br, bi):
    return ar * br - ai * bi, ar * bi + ai * br


SCAN_SEQS = 8


def _interleave_rows(x, tmp_ref):
    n = x.shape[0]
    tmp_ref[...] = x
    return jnp.concatenate([tmp_ref[pl.ds(i, SCAN_SEQS, stride=n // SCAN_SEQS), :] for i in range(n // SCAN_SEQS)],
                           axis=0)


def _deinterleave_rows(x, tmp_ref):
    n = x.shape[0]
    tmp_ref[...] = x
    return jnp.concatenate([tmp_ref[pl.ds(s, n // SCAN_SEQS, stride=SCAN_SEQS), :] for s in range(SCAN_SEQS)], axis=0)


def _scan_chunk(xr, xi, tab_ref, cin_r, cin_i, *, lanes, reverse):
    n = xr.shape[0]
    L = n // SCAN_SEQS
    re, im = pl.ds(0, lanes), pl.ds(lanes, lanes)
    one, top = (L - 1, 0) if reverse else (0, L - 1)
    a_r = jnp.broadcast_to(tab_ref[pl.ds(one, 1), re], (SCAN_SEQS, lanes))
    a_i = jnp.broadcast_to(tab_ref[pl.ds(one, 1), im], (SCAN_SEQS, lanes))
    sr = si = jnp.zeros((SCAN_SEQS, lanes), F32)
    loc_r, loc_i = [None] * L, [None] * L
    for i in (reversed(range(L)) if reverse else range(L)):
        rows = slice(SCAN_SEQS * i, SCAN_SEQS * (i + 1))
        sr, si = a_r * sr - a_i * si + xr[rows], a_r * si + a_i * sr + xi[rows]
        loc_r[i], loc_i[i] = sr, si
    top_r, top_i = tab_ref[pl.ds(top, 1), re], tab_ref[pl.ds(top, 1), im]
    sub = lax.broadcasted_iota(jnp.int32, (SCAN_SEQS, lanes), 0)
    cr, ci = cin_r, cin_i
    in_r = in_i = jnp.zeros((SCAN_SEQS, lanes), F32)
    for s in (reversed(range(SCAN_SEQS)) if reverse else range(SCAN_SEQS)):
        in_r, in_i = jnp.where(sub == s, cr, in_r), jnp.where(sub == s, ci, in_i)
        dr, di = _cmul(top_r, top_i, cr, ci)
        cr, ci = dr + sr[s:s + 1], di + si[s:s + 1]
    out_r, out_i = [None] * L, [None] * L
    for i in range(L):
        dr, di = _cmul(tab_ref[pl.ds(i, 1), re], tab_ref[pl.ds(i, 1), im], in_r, in_i)
        out_r[i], out_i[i] = loc_r[i] + dr, loc_i[i] + di
    return jnp.concatenate(out_r, axis=0), jnp.concatenate(out_i, axis=0), cr, ci


def _ssm_fwd(name, h, bbd, ccd, tab, dskip, *, tc):
    T, D = h.shape
    NG, CB, L2 = bbd.shape
    lanes = L2 // 2
    nch = T // tc

    def body(h_ref, bb_ref, cc_ref, tab_ref, d_ref, y_ref, s_ref, cin_ref, carry, tmp):
        t = pl.program_id(1)

        @pl.when(t == 0)
        def _():
            carry[...] = jnp.zeros_like(carry)

        cin_ref[...] = carry[...]
        uf = _interleave_rows(h_ref[...].astype(F32), tmp)
        bu = jnp.dot(uf.astype(BF16), bb_ref[...], preferred_element_type=F32)
        sr, si, cr, ci = _scan_chunk(bu[:, :lanes], bu[:, lanes:], tab_ref,
                                     carry[pl.ds(0, 1), pl.ds(0, lanes)], carry[pl.ds(0, 1), pl.ds(lanes, lanes)],
                                     lanes=lanes, reverse=False)
        carry[pl.ds(0, 1), pl.ds(0, lanes)] = cr
        carry[pl.ds(0, 1), pl.ds(lanes, lanes)] = ci
        s = jnp.concatenate([sr, si], axis=1).astype(BF16)
        s_ref[...] = s
        yp = jnp.dot(s, cc_ref[...], preferred_element_type=F32) + d_ref[...] * uf
        y_ref[...] = _deinterleave_rows(_gelu(yp), tmp).astype(BF16)

    return pl.pallas_call(
        body, name=name, grid=(NG, nch),
        in_specs=[pl.BlockSpec((tc, CB), lambda b, t: (t, b)),
                  pl.BlockSpec((None, CB, L2), lambda b, t: (b, 0, 0)),
                  pl.BlockSpec((None, L2, CB), lambda b, t: (b, 0, 0)),
                  pl.BlockSpec((None,) + tab.shape[1:], lambda b, t: (b, 0, 0)),
                  pl.BlockSpec((1, CB), lambda b, t: (0, b))],
        out_specs=[pl.BlockSpec((tc, CB), lambda b, t: (t, b)),
                   pl.BlockSpec((tc, L2), lambda b, t: (t, b)),
                   pl.BlockSpec((None, None, 8, L2), lambda b, t: (b, t, 0, 0))],
        out_shape=[jax.ShapeDtypeStruct((T, D), BF16), jax.ShapeDtypeStruct((T, NG * L2), BF16),
                   jax.ShapeDtypeStruct((NG, nch, 8, L2), F32)],
        scratch_shapes=[pltpu.VMEM((8, L2), F32), pltpu.VMEM((tc, CB), F32)],
        compiler_params=_params("parallel", "arbitrary"),
    )(h, bbd, ccd, tab, dskip)


def _ssm_bwd(name, h, dy, states, cins, bbd, ccd, tabr, dskip, *, tc):
    T, D = h.shape
    NG, CB, L2 = bbd.shape
    lanes = L2 // 2
    nch = T // tc

    def body(h_ref, dy_ref, s_ref, cin_ref, bb_ref, cc_ref, tabr_ref, d_ref,
             dh_ref, dbb_ref, dcc_ref, da_ref, dd_ref, gcarry, a_da, a_dd, tmp):
        t = pl.program_id(1)

        @pl.when(t == 0)
        def _():
            gcarry[...] = jnp.zeros_like(gcarry)
            a_da[...] = jnp.zeros_like(a_da)
            a_dd[...] = jnp.zeros_like(a_dd)
            dbb_ref[...] = jnp.zeros_like(dbb_ref)
            dcc_ref[...] = jnp.zeros_like(dcc_ref)

        uf = _interleave_rows(h_ref[...].astype(F32), tmp)
        u = uf.astype(BF16)
        dyv = _interleave_rows(dy_ref[...], tmp)
        cin_r = cin_ref[pl.ds(0, 1), pl.ds(0, lanes)]
        cin_i = cin_ref[pl.ds(0, 1), pl.ds(lanes, lanes)]
        s = s_ref[...]
        dv = d_ref[...]
        yp = jnp.dot(s, cc_ref[...], preferred_element_type=F32) + dv * uf
        dyp = dyv * _gelu_grad(yp)
        a_dd[...] += _fold8(dyp * uf)
        dypb = dyp.astype(BF16)
        dcc_ref[...] += lax.dot_general(s, dypb, (((0,), (0,)), ((), ())), preferred_element_type=F32)
        ds = lax.dot_general(dypb, cc_ref[...], (((1,), (1,)), ((), ())), preferred_element_type=F32)
        gr, gi, cr, ci = _scan_chunk(ds[:, :lanes], ds[:, lanes:], tabr_ref,
                                     gcarry[pl.ds(0, 1), pl.ds(0, lanes)], gcarry[pl.ds(0, 1), pl.ds(lanes, lanes)],
                                     lanes=lanes, reverse=True)
        gcarry[pl.ds(0, 1), pl.ds(0, lanes)] = cr
        gcarry[pl.ds(0, 1), pl.ds(lanes, lanes)] = ci
        gb = jnp.concatenate([gr, gi], axis=1).astype(BF16)
        du = lax.dot_general(gb, bb_ref[...], (((1,), (1,)), ((), ())), preferred_element_type=F32)
        dh_ref[...] = _deinterleave_rows(du + dv * dyp, tmp)
        dbb_ref[...] += lax.dot_general(u, gb, (((0,), (0,)), ((), ())), preferred_element_type=F32)
        sr, si = s[:, :lanes].astype(F32), s[:, lanes:].astype(F32)
        first = lax.broadcasted_iota(jnp.int32, (SCAN_SEQS, lanes), 0) == 0
        head_r = jnp.where(first, cin_r, pltpu.roll(sr[tc - SCAN_SEQS:], 1, 0))
        head_i = jnp.where(first, cin_i, pltpu.roll(si[tc - SCAN_SEQS:], 1, 0))
        pr = jnp.concatenate([head_r, sr[:tc - SCAN_SEQS]], axis=0)
        pi = jnp.concatenate([head_i, si[:tc - SCAN_SEQS]], axis=0)
        a_da[:, pl.ds(0, lanes)] += _fold8(pr * gr + pi * gi)
        a_da[:, pl.ds(lanes, lanes)] += _fold8(pr * gi - pi * gr)

        @pl.when(t == nch - 1)
        def _():
            da_ref[...] = jnp.sum(a_da[...], axis=0, keepdims=True)
            dd_ref[...] = jnp.sum(a_dd[...], axis=0, keepdims=True)

    rev = lambda b, t: (nch - 1 - t, b)
    return pl.pallas_call(
        body, name=name, grid=(NG, nch),
        in_specs=[pl.BlockSpec((tc, CB), rev), pl.BlockSpec((tc, CB), rev), pl.BlockSpec((tc, L2), rev),
                  pl.BlockSpec((None, None, 8, L2), lambda b, t: (b, nch - 1 - t, 0, 0)),
                  pl.BlockSpec((None, CB, L2), lambda b, t: (b, 0, 0)),
                  pl.BlockSpec((None, L2, CB), lambda b, t: (b, 0, 0)),
                  pl.BlockSpec((None,) + tabr.shape[1:], lambda b, t: (b, 0, 0)),
                  pl.BlockSpec((1, CB), lambda b, t: (0, b))],
        out_specs=[pl.BlockSpec((tc, CB), rev),
                   pl.BlockSpec((None, CB, L2), lambda b, t: (b, 0, 0)),
                   pl.BlockSpec((None, L2, CB), lambda b, t: (b, 0, 0)),
                   pl.BlockSpec((None, 1, L2), lambda b, t: (b, 0, 0)),
                   pl.BlockSpec((1, CB), lambda b, t: (0, b))],
        out_shape=[jax.ShapeDtypeStruct((T, D), F32), jax.ShapeDtypeStruct((NG, CB, L2), F32),
                   jax.ShapeDtypeStruct((NG, L2, CB), F32), jax.ShapeDtypeStruct((NG, 1, L2), F32),
                   jax.ShapeDtypeStruct((1, D), F32)],
        scratch_shapes=[pltpu.VMEM((8, L2), F32), pltpu.VMEM((8, L2), F32), pltpu.VMEM((8, CB), F32),
                        pltpu.VMEM((tc, CB), F32)],
        compiler_params=_params("parallel", "arbitrary"),
    )(h, dy, states, cins, bbd, ccd, tabr, dskip)


def _zoh(lr, li, ldt):
    dt = jnp.exp(ldt)
    mag = jnp.exp(lr * dt)
    ar = mag * jnp.cos(li * dt)
    ai = mag * jnp.sin(li * dt)
    den = lr * lr + li * li
    nr = ar - 1.0
    kr = (nr * lr + ai * li) / den
    ki = (ai * lr - nr * li) / den
    return dt, ar, ai, kr, ki, den


def _ssm_prep(name, lr, li, ldt, br, bi):
    shp = jax.ShapeDtypeStruct(lr.shape, F32)

    def body(lr_ref, li_ref, ldt_ref, br_ref, bi_ref, ar_ref, ai_ref, kr_ref, ki_ref, bbr_ref, bbi_ref):
        _, ar, ai, kr, ki, _ = _zoh(lr_ref[...], li_ref[...], ldt_ref[...])
        ar_ref[...] = ar
        ai_ref[...] = ai
        kr_ref[...] = kr
        ki_ref[...] = ki
        bbr, bbi = _cmul(kr, ki, br_ref[...], bi_ref[...])
        bbr_ref[...] = bbr
        bbi_ref[...] = bbi

    return pl.pallas_call(body, name=name, out_shape=[shp] * 6)(lr, li, ldt, br, bi)


def _ssm_powers(name, ar, ai, rows):
    NG, _, lanes = ar.shape

    def body(ar_ref, ai_ref, tf_ref, tr_ref):
        a_r, a_i = ar_ref[...], ai_ref[...]
        pw = [(a_r, a_i)]
        for _ in range(7):
            pw.append(_cmul(pw[-1][0], pw[-1][1], a_r, a_i))
        row = lax.broadcasted_iota(jnp.int32, (8, lanes), 0)
        fr = fi = rr = ri = jnp.zeros((8, lanes), F32)
        for n in range(8):
            fr = jnp.where(row == n, pw[n][0], fr)
            fi = jnp.where(row == n, pw[n][1], fi)
            rr = jnp.where(row == 7 - n, pw[n][0], rr)
            ri = jnp.where(row == 7 - n, pw[n][1], ri)
        top_r, top_i = pw[7]
        size = 8
        while size < rows:
            hr, hi = _cmul(fr, fi, top_r, top_i)
            fr, fi = jnp.concatenate([fr, hr], axis=0), jnp.concatenate([fi, hi], axis=0)
            hr, hi = _cmul(rr, ri, top_r, top_i)
            rr, ri = jnp.concatenate([hr, rr], axis=0), jnp.concatenate([hi, ri], axis=0)
            top_r, top_i = _cmul(top_r, top_i, top_r, top_i)
            size *= 2
        tf_ref[:, pl.ds(0, lanes)] = fr
        tf_ref[:, pl.ds(lanes, lanes)] = fi
        tr_ref[:, pl.ds(0, lanes)] = rr
        tr_ref[:, pl.ds(lanes, lanes)] = -ri

    vec = pl.BlockSpec((None, 1, lanes), lambda b: (b, 0, 0))
    tab = pl.BlockSpec((None, rows, 2 * lanes), lambda b: (b, 0, 0))
    shp = jax.ShapeDtypeStruct((NG, rows, 2 * lanes), F32)
    return pl.pallas_call(body, name=name, grid=(NG,), in_specs=[vec, vec], out_specs=[tab, tab],
                          out_shape=[shp, shp], compiler_params=_params("parallel"))(ar, ai)


def _ssm_prep_bwd_b(name, kr, ki, br, bi, gbr, gbi):
    shp = jax.ShapeDtypeStruct(kr.shape, F32)

    def body(kr_ref, ki_ref, br_ref, bi_ref, gr_ref, gi_ref, dbr_ref, dbi_ref, tr_ref, ti_ref):
        gr, gi = gr_ref[...], gi_ref[...]
        dbr, dbi = _cmul(kr_ref[...], -ki_ref[...], gr, gi)
        dbr_ref[...] = dbr
        dbi_ref[...] = dbi
        t_r, t_i = _cmul(br_ref[...], -bi_ref[...], gr, gi)
        tr_ref[...] = t_r
        ti_ref[...] = t_i

    return pl.pallas_call(body, name=name, out_shape=[shp] * 4)(kr, ki, br, bi, gbr, gbi)


def _ssm_prep_bwd_a(name, lr, li, ldt, gar, gai, tkr, tki):
    G, P = lr.shape

    def body(lr_ref, li_ref, ldt_ref, gar_ref, gai_ref, tkr_ref, tki_ref, dlr_ref, dli_ref, dldt_ref):
        lr_v, li_v = lr_ref[...], li_ref[...]
        dt, ar, ai, kr, ki, den = _zoh(lr_v, li_v, ldt_ref[...])
        gkr = jnp.sum(tkr_ref[...], axis=0)
        gki = jnp.sum(tki_ref[...], axis=0)
        ir, ii = lr_v / den, -li_v / den
        t_r, t_i = _cmul(ir, -ii, gkr, gki)
        gar_t, gai_t = gar_ref[...] + t_r, gai_ref[...] + t_i
        qr, qi = _cmul(kr, ki, ir, ii)
        t_r, t_i = _cmul(-qr, qi, gkr, gki)
        u_r, u_i = _cmul(dt * ar, -dt * ai, gar_t, gai_t)
        dlr_ref[...] = u_r + t_r
        dli_ref[...] = u_i + t_i
        la_r, la_i = _cmul(lr_v, li_v, ar, ai)
        w_r, _ = _cmul(la_r, -la_i, gar_t, gai_t)
        ddt = jnp.sum(w_r, axis=1, keepdims=True)
        dldt_ref[...] = jnp.broadcast_to(ddt * dt[:, 0:1], dldt_ref.shape)

    shp = jax.ShapeDtypeStruct((G, P), F32)
    return pl.pallas_call(body, name=name, out_shape=[shp, shp, jax.ShapeDtypeStruct((G, 128), F32)])(
        lr, li, ldt, gar, gai, tkr, tki)


ROWS_CALL_TILE_ELEMS = 256 * 1024


def _rows_call(name, fn, ins, outs):
    R, Cn = ins[0].shape
    tr = _pick(R, max(16, ROWS_CALL_TILE_ELEMS // Cn), 16)
    spec = pl.BlockSpec((tr, Cn), lambda i: (i, 0))

    def body(*refs):
        res = fn(*[r[...] for r in refs[:len(ins)]])
        for o_ref, r in zip(refs[len(ins):], res):
            o_ref[...] = r.astype(o_ref.dtype)

    return pl.pallas_call(
        body, name=name, grid=(R // tr,), in_specs=[spec] * len(ins), out_specs=[spec] * len(outs),
        out_shape=[jax.ShapeDtypeStruct((R, Cn), d) for d in outs], compiler_params=_params("parallel"),
    )(*ins)


def _adamw_math(w, g, m, v):
    m = ADAM_B1 * m + (1.0 - ADAM_B1) * g
    v = ADAM_B2 * v + (1.0 - ADAM_B2) * (g * g)
    m_hat = m / (1.0 - ADAM_B1 ** ADAM_STEP)
    v_hat = v / (1.0 - ADAM_B2 ** ADAM_STEP)
    delta = -ADAM_LR * (m_hat / (jnp.sqrt(v_hat) + ADAM_EPS) + ADAM_WD * w)
    return delta, m, v


def _adamw(name, w, g, m, v):
    shape = w.shape
    cols = shape[-1]
    to2d = lambda t: t.reshape(-1, cols)
    res = _rows_call(name, _adamw_math, [to2d(w), to2d(g), to2d(m), to2d(v)], [F32, F32, F32])
    return [r.reshape(shape) for r in res]


ANY = pl.BlockSpec(memory_space=pl.ANY)


def _place():
    x, y, c = lax.axis_index("x"), lax.axis_index("y"), lax.axis_index("c")
    chips = [(1 - x, y), (x, 1 - y), (1 - x, 1 - y)]
    return x, y, c, chips


def _remote(src, dst, send_sem, recv_sem, dev):
    return pltpu.make_async_remote_copy(src_ref=src, dst_ref=dst, send_sem=send_sem, recv_sem=recv_sem,
                                        device_id=dev, device_id_type=MESH)


def _gather_phases(shapes, w_refs, o_refs, send_sems, recv_sems):
    x, y, c, chips = _place()
    me_k = 2 * x + y
    sibling = (x, y, 1 - c)

    def ici(i, r):
        half = shapes[i][0] // 2
        mine = pl.ds(c * half, half)
        cx, cy = chips[r]
        return _remote(w_refs[i].at[mine], o_refs[i].at[me_k, mine],
                       send_sems.at[6 * i + r], recv_sems.at[6 * i + r], (cx, cy, c))

    def d2d(i, r, core):
        half = shapes[i][0] // 2
        cx, cy = chips[r]
        rows = o_refs[i].at[2 * cx + cy, pl.ds(core * half, half)]
        return rows, _remote(rows, rows, send_sems.at[6 * i + 3 + r], recv_sems.at[6 * i + 3 + r], sibling)

    pairs = [(i, r) for i in range(len(shapes)) for r in range(3)]

    def send():
        for i, r in pairs:
            ici(i, r).start()

    def forward():
        for i, r in pairs:
            got, fwd = d2d(i, r, c)
            _remote(got, got, send_sems.at[6 * i + r], recv_sems.at[6 * i + r], sibling).wait_recv()
            fwd.start()

    def drain():
        for i, r in pairs:
            d2d(i, r, 1 - c)[1].wait_recv()
        for i, r in pairs:
            ici(i, r).wait_send()
            d2d(i, r, c)[1].wait_send()

    return send, forward, drain


class _Job:
    def __init__(self, ins, outs, n_sems, bind):
        self.ins, self.outs, self.n_sems, self.bind = list(ins), list(outs), n_sems, bind


def _jobs_sems(jobs):
    return [pltpu.SemaphoreType.DMA((jb.n_sems,)) for jb in jobs for _ in range(2)]


def _jobs_bind(jobs, in_refs, out_refs, sem_refs):
    bound, i, o = [], 0, 0
    for k, jb in enumerate(jobs):
        bound.append(jb.bind(in_refs[i:i + len(jb.ins)], out_refs[o:o + len(jb.outs)], sem_refs[2 * k], sem_refs[2 * k + 1]))
        i, o = i + len(jb.ins), o + len(jb.outs)

    def phase(p):
        def run():
            for b in bound:
                b[p]()
        return run

    return phase(0), phase(1), phase(2)


def _run_jobs(name, jobs):
    ins = [a for jb in jobs for a in jb.ins]
    outs = [o for jb in jobs for o in jb.outs]

    def body(*refs):
        start, middle, finish = _jobs_bind(jobs, refs[:len(ins)], refs[len(ins):len(ins) + len(outs)],
                                           refs[len(ins) + len(outs):])
        start()
        middle()
        finish()

    return pl.pallas_call(body, name=name, in_specs=[ANY] * len(ins), out_specs=[ANY] * len(outs), out_shape=outs,
                          scratch_shapes=_jobs_sems(jobs))(*ins)


def _gather_job(ws):
    shapes = [w.shape for w in ws]
    return _Job(ws, [jax.ShapeDtypeStruct((N_CHIPS,) + w.shape, w.dtype) for w in ws], 6 * len(ws),
                functools.partial(_gather_phases, shapes))


def _nothing():
    pass


def _pair_exchange_job(gs):
    def bind(g_refs, r_refs, send_sems, recv_sems):
        x, y, c, _ = _place()

        def copies():
            for i, g in enumerate(gs):
                half = g.shape[1] // 2
                yield _remote(g_refs[i].at[:, pl.ds((1 - c) * half, half)], r_refs[i],
                              send_sems.at[i], recv_sems.at[i], (x, y, 1 - c))

        def start():
            for cp in copies():
                cp.start()

        def finish():
            for cp in copies():
                cp.wait_recv()
            for cp in copies():
                cp.wait_send()

        return start, _nothing, finish

    return _Job(gs, [jax.ShapeDtypeStruct((g.shape[0], g.shape[1] // 2, g.shape[2]), g.dtype) for g in gs], len(gs), bind)


def _chip_exchange_job(pbs):
    def bind(pb_refs, rv_refs, send_sems, recv_sems):
        x, y, c, chips = _place()

        def copies():
            for i in range(len(pbs)):
                for r, (cx, cy) in enumerate(chips):
                    yield _remote(pb_refs[i].at[2 * cx + cy], rv_refs[3 * i + r],
                                  send_sems.at[3 * i + r], recv_sems.at[3 * i + r], (cx, cy, c))

        def start():
            for cp in copies():
                cp.start()

        def finish():
            for cp in copies():
                cp.wait_recv()
            for cp in copies():
                cp.wait_send()

        return start, _nothing, finish

    return _Job(pbs, [jax.ShapeDtypeStruct(pb.shape[1:], pb.dtype) for pb in pbs for _ in range(3)], 3 * len(pbs), bind)


def _tile_rows(rows, cols):
    return _pick(rows, max(16, ROWS_CALL_TILE_ELEMS // cols), 16)


def _place_own(name, own, stack, meta):
    R, Cn = own.shape
    tr = _tile_rows(R, Cn)

    def body(m_ref, own_ref, stack_ref, o_ref):
        o_ref[...] = own_ref[...]

    return pl.pallas_call(
        body, name=name, out_shape=jax.ShapeDtypeStruct(stack.shape, stack.dtype), input_output_aliases={2: 0},
        grid_spec=pltpu.PrefetchScalarGridSpec(
            num_scalar_prefetch=1, grid=(R // tr,),
            in_specs=[pl.BlockSpec((tr, Cn), lambda i, m: (i, 0)), ANY],
            out_specs=pl.BlockSpec((None, tr, Cn), lambda i, m: (m[1], i, 0))),
        compiler_params=_params("parallel"),
    )(meta, own, stack)


def _pair_sum(name, g, recv, meta):
    S, R, Cn = g.shape
    H = R // 2
    tr = _tile_rows(H, Cn)
    nh = H // tr

    def body(m_ref, g_ref, r_ref, p32_ref, p16_ref):
        v = g_ref[...] + r_ref[...]
        p32_ref[...] = v
        p16_ref[...] = v.astype(BF16)

    blk = pl.BlockSpec((None, tr, Cn), lambda s, i, m: (s, i, 0))
    return pl.pallas_call(
        body, name=name,
        out_shape=[jax.ShapeDtypeStruct((S, H, Cn), F32), jax.ShapeDtypeStruct((S, H, Cn), BF16)],
        grid_spec=pltpu.PrefetchScalarGridSpec(
            num_scalar_prefetch=1, grid=(S, nh),
            in_specs=[pl.BlockSpec((None, tr, Cn), lambda s, i, m: (s, m[0] * nh + i, 0)), blk],
            out_specs=[blk, blk]),
        compiler_params=_params("parallel", "parallel"),
    )(meta, g, recv)


def _chip_sum(name, p32, rvs, stack, layer, n_layers, meta):
    S, H, Cn = p32.shape
    tr = _tile_rows(H, Cn)
    nh = H // tr
    half = pl.BlockSpec((tr, Cn), lambda i, m: (i, 0))
    in_specs = [pl.BlockSpec((None, tr, Cn), lambda i, m: (m[1], i, 0)), half, half, half]
    args = [meta, p32, *rvs]
    aliases = {}
    if stack is not None:
        in_specs.append(ANY)
        args.append(stack)
        aliases = {len(args) - 1: 0}

    def body(m_ref, p_ref, r0_ref, r1_ref, r2_ref, *rest):
        rest[-1][...] = p_ref[...] + r0_ref[...].astype(F32) + r1_ref[...].astype(F32) + r2_ref[...].astype(F32)

    return pl.pallas_call(
        body, name=name, out_shape=jax.ShapeDtypeStruct((n_layers * 2 * H, Cn), F32), input_output_aliases=aliases,
        grid_spec=pltpu.PrefetchScalarGridSpec(
            num_scalar_prefetch=1, grid=(nh,), in_specs=in_specs,
            out_specs=pl.BlockSpec((tr, Cn), lambda i, m: (layer * 2 * nh + m[0] * nh + i, 0))),
        compiler_params=_params("parallel"),
    )(*args)


def _pair_gather(stacks, halves):
    ng = len(stacks)
    copies = [(g, l) for g in range(ng) for l in range(stacks[g].shape[0] // (2 * halves[g]))]
    n = len(copies)

    def body(*refs):
        o_refs = refs[ng:2 * ng]
        send_sems, recv_sems = refs[2 * ng:]
        x, y, c, _ = _place()
        sibling = (x, y, 1 - c)
        cps = []
        for k, (g, l) in enumerate(copies):
            H = halves[g]
            rows = o_refs[g].at[pl.ds(l * 2 * H + c * H, H)]
            cp = _remote(rows, rows, send_sems.at[k], recv_sems.at[k], sibling)
            cp.start()
            cps.append(cp)
        for k, (g, l) in enumerate(copies):
            H = halves[g]
            theirs = o_refs[g].at[pl.ds(l * 2 * H + (1 - c) * H, H)]
            _remote(theirs, theirs, send_sems.at[k], recv_sems.at[k], sibling).wait_recv()
        for cp in cps:
            cp.wait_send()

    return pl.pallas_call(
        body, name="grad_pair_gather", in_specs=[ANY] * ng, out_specs=[ANY] * ng,
        out_shape=[jax.ShapeDtypeStruct(s.shape, s.dtype) for s in stacks],
        input_output_aliases={g: g for g in range(ng)},
        scratch_shapes=[pltpu.SemaphoreType.DMA((n,)), pltpu.SemaphoreType.DMA((n,))],
    )(*stacks)


GRAD_BYTES = 704e6
ICI_SECONDS_PER_BYTE = 1.9e-3 / GRAD_BYTES
D2D_SECONDS_PER_BYTE = 0.53e-3 / GRAD_BYTES
MXU_FLOPS = 7.5e14
CARRIER_FILL = 0.8


class _GradReducer:
    def __init__(self, meta, layers):
        self.meta, self.layers = meta, dict(layers)
        self.fresh, self.summed = [], []
        self.stack = {n: None for n in layers}
        self.half = {}
        self.hurry = False

    def add(self, name, layer, g):
        self.fresh.append((name, layer, g))

    def take(self, seconds):
        budget, used = CARRIER_FILL * seconds, 0.0
        jobs, plan = [], []
        while self.summed:
            cost = 2 * 4 * self.summed[0][2].size * ICI_SECONDS_PER_BYTE
            if used + cost > budget and (jobs or not self.hurry):
                break
            used += cost
            item = self.summed.pop(0)
            jobs.append(_chip_exchange_job([item[3]]))
            plan.append(("chip", item))
        if self.fresh:
            fresh, self.fresh = self.fresh, []
            jobs.append(_pair_exchange_job([g for _, _, g in fresh]))
            plan.append(("pair", fresh))
        return jobs, functools.partial(self._done, plan)

    def _done(self, plan, outs):
        i = 0
        for kind, item in plan:
            if kind == "chip":
                name, layer, p32, _ = item
                self.stack[name] = _chip_sum(f"grad_chip_sum_{name}_{layer}", p32, outs[i:i + 3], self.stack[name],
                                             layer, self.layers[name], self.meta)
                i += 3
            else:
                for (name, layer, g), recv in zip(item, outs[i:i + len(item)]):
                    p32, p16 = _pair_sum(f"grad_pair_sum_{name}_{layer}", g, recv, self.meta)
                    self.half[name] = p32.shape[1]
                    self.summed.append((name, layer, p32, p16))
                i += len(item)

    def finish(self):
        k = 0
        while self.fresh or self.summed:
            jobs, done = self.take(float("inf"))
            done(_run_jobs(f"grad_exchange_tail_{k}", jobs))
            k += 1
        names = list(self.layers)
        return dict(zip(names, _pair_gather([self.stack[n] for n in names], [self.half[n] for n in names])))


def _small_gather(name, v):
    def body(v_ref, o_ref, send_sems, recv_sems, loc_sem):
        x, y, c, chips = _place()
        me, sibling = (x, y, c), (x, y, 1 - c)

        def slot(px, py, pc):
            return o_ref.at[4 * px + 2 * py + pc]

        def copy(k, block, to, src=None):
            return _remote(slot(*block) if src is None else src, slot(*block), send_sems.at[k], recv_sems.at[k], to)

        mine = pltpu.make_async_copy(v_ref, slot(*me), loc_sem)
        mine.start()
        first = [copy(0, me, sibling, src=v_ref)]
        first += [copy(1 + j, me, (*chip, c), src=v_ref) for j, chip in enumerate(chips)]
        for cp in first:
            cp.start()
        passed = [copy(4 + j, (*chip, c), sibling) for j, chip in enumerate(chips)]
        for j, chip in enumerate(chips):
            copy(1 + j, (*chip, c), me).wait_recv()
            passed[j].start()
        copy(0, sibling, me).wait_recv()
        for j, chip in enumerate(chips):
            copy(4 + j, (*chip, 1 - c), me).wait_recv()
        for cp in first + passed:
            cp.wait_send()
        mine.wait()

    return pl.pallas_call(
        body, name=name, in_specs=[ANY], out_specs=ANY,
        out_shape=jax.ShapeDtypeStruct((N_DEV,) + v.shape, v.dtype),
        scratch_shapes=[pltpu.SemaphoreType.DMA((7,)), pltpu.SemaphoreType.DMA((7,)), pltpu.SemaphoreType.DMA],
    )(v)


SUM_ROWS = 512


def _sum_devices(name, g):
    n, R, Cn = g.shape
    tr = _pick(R, SUM_ROWS, 8)

    def body(g_ref, o_ref):
        acc = g_ref[0]
        for d in range(1, n):
            acc = acc + g_ref[d]
        o_ref[...] = acc

    return pl.pallas_call(
        body, name=name, grid=(R // tr,), in_specs=[pl.BlockSpec((n, tr, Cn), lambda i: (0, i, 0))],
        out_specs=pl.BlockSpec((tr, Cn), lambda i: (i, 0)), out_shape=jax.ShapeDtypeStruct((R, Cn), g.dtype),
        compiler_params=_params("parallel"),
    )(g)


WEIGHTS = ["mix_norm", "conv_w_in", "conv_b_in", "conv_dw", "conv_dw_b", "conv_ln_g", "conv_ln_b", "conv_w_out",
           "conv_b_out", "ssm_lambda_re", "ssm_lambda_im", "ssm_log_dt", "ssm_b_re", "ssm_b_im", "ssm_c_re",
           "ssm_c_im", "ssm_d", "ssm_w_glu", "mlp_norm", "mlp_w_up", "mlp_w_down", "final_norm"]
LARGE = ["conv_w_in", "conv_w_out", "ssm_w_glu", "mlp_w_up", "mlp_w_down"]
SHARDED_SMALL = ["conv_dw", "ssm_d"]
REPLICATED = [n for n in WEIGHTS if n not in LARGE and n not in SHARDED_SMALL]
PACK_QUANTUM = 8 * 128


def _pack(parts, row_multiple=8):
    rows = []
    for p in parts:
        f = p.reshape(-1)
        pad = (-f.shape[0]) % PACK_QUANTUM
        if pad:
            f = jnp.pad(f, (0, pad))
        rows.append(f.reshape(-1, 128))
    total = sum(r.shape[0] for r in rows)
    if total % row_multiple:
        rows.append(jnp.zeros((row_multiple - total % row_multiple, 128), rows[0].dtype))
    return jnp.concatenate(rows, axis=0)


def _packed_rows(shape):
    return -(-math.prod(shape) // PACK_QUANTUM) * 8


def _unpack(buf, shapes):
    out, r = [], 0
    for s in shapes:
        rows = _packed_rows(s)
        out.append(buf[r:r + rows].reshape(-1)[:math.prod(s)].reshape(s))
        r += rows
    return out


def _block_diag(t, pattern):
    return jnp.einsum(pattern, t, jnp.eye(GROUPS_PER_BLOCK, dtype=t.dtype))


def _local_step(xs, tgt, p, wg, S, own=None, meta=None, reducer=None):
    T, D = xs.shape
    depth = p["mix_norm"].shape[0]
    width = p["conv_dw"].shape[1]
    G, P, C = D // SSM_C, SSM_P, SSM_C
    NG = G // GROUPS_PER_BLOCK
    lanes = GROUPS_PER_BLOCK * P
    F = (own if own is not None else wg)["mlp_w_up"][0].shape[-1] * S
    tm = _pick(T, 1024, 16)
    tmh = _pick(T, 512, 16)
    tre = _pick(T, 256, 16)
    trc = _pick(T, 128, CONV_HALO)
    tcs = _pick(T, 256, 8 * SCAN_SEQS)
    row = lambda v: v.reshape(1, -1)

    def missing(units):
        return [u for u in units if wg[u[0]][u[1]] is None]

    def store(units, stacks):
        for (n, l), st in zip(units, stacks):
            wg[n][l] = _place_own(f"place_own_{n}_{l}", own[n][l], st, meta)

    def with_fetch(fn, units, n_out):
        units = missing(units)
        res = fn(jobs=[_gather_job([own[n][l] for n, l in units])] if units else [])
        store(units, res[n_out:])
        return res[:n_out]

    def riding(fn, flops, n_out):
        if reducer is None:
            return fn()[:n_out]
        jobs, done = reducer.take(flops / MXU_FLOPS)
        res = fn(jobs=jobs)
        done(res[n_out:])
        return res[:n_out]

    def got(name, layer, g):
        gs[name][layer] = g
        if reducer is not None:
            reducer.add(name, layer, g)

    def nn_col(name, a, wname, j, offs, extras, outs, epi, tm_, fetch=()):
        w = wg[wname][j]
        K, Ns = a.shape[1], w.shape[2]
        tn, tk = _pick(Ns, MM_OUT_TILE, 128), _pick(K, MM_K_TILE, 128)
        maps = [_w_col(0, K // tk, Ns // tn, o // tn) for o in offs]
        return with_fetch(functools.partial(_mm_nn, name, a, w, maps, extras, outs, epi, tm=tm_, tn=tn, tk=tk),
                          fetch, len(outs))

    def nn_row(name, a, wname, j, extras, outs, epi, fetch=()):
        w = wg[wname][j]
        Ks, N = a.shape[1] // S, w.shape[2]
        tn, tk = _pick(N, MM_OUT_TILE, 128), _pick(Ks, MM_K_TILE, 128)
        return with_fetch(functools.partial(_mm_nn, name, a, w, [_w_row(0, Ks // tk)], extras, outs, epi,
                                            tm=tm, tn=tn, tk=tk), fetch, len(outs))

    def nt_col(name, a, wname, j, R, extras, outs, epi):
        w = wg[wname][j]
        Cs = w.shape[2]
        tr, tc = _pick(R, MM_OUT_TILE, 128), _pick(Cs, MM_K_TILE, 128)
        return riding(functools.partial(_mm_nt, name, a, w, _nt_col(0, R // tr, Cs // tc), extras, outs, epi,
                                        tm=tm, tr=tr, tc=tc), 2 * a.shape[0] * a.shape[1] * R, len(outs))

    def nt_row(name, a, wname, j, R, extras, outs, epi):
        w = wg[wname][j]
        Rs = R // S
        tr, tc = _pick(Rs, MM_OUT_TILE, 128), _pick(w.shape[2], MM_K_TILE, 128)
        return riding(functools.partial(_mm_nt, name, a, w, _nt_row(0, Rs // tr), extras, outs, epi,
                                        tm=tm, tr=tr, tc=tc), 2 * a.shape[0] * a.shape[1] * R, len(outs))

    def tn(name, a, b, sharding):
        R, N = a.shape[1], b.shape[1]
        if sharding == "col":
            tr, tc = _pick(R, MM_OUT_TILE, 128), _pick(N // S, MM_OUT_TILE, 128)
        else:
            tr, tc = _pick(R // S, MM_OUT_TILE, 128), _pick(N, MM_OUT_TILE, 128)
        return riding(functools.partial(_mm_tn, name, a, b, sharding, S, tr=tr, tc=tc, tt=_pick(T, MM_K_TILE, 128)),
                      2 * T * R * N, 1)[0]

    ssm = []
    for j in range(p["ssm_lambda_re"].shape[0]):
        rep = lambda t: jnp.repeat(t, C, axis=0)
        lam_r, lam_i = p["ssm_lambda_re"][j], p["ssm_lambda_im"][j]
        ldt = jnp.broadcast_to(p["ssm_log_dt"][j][:, None], (G, P))
        b_r = p["ssm_b_re"][j].transpose(0, 2, 1).reshape(G * C, P)
        b_i = p["ssm_b_im"][j].transpose(0, 2, 1).reshape(G * C, P)
        ar, ai, kr, ki, bbr, bbi = _ssm_prep(f"ssm_prep_{j}", rep(lam_r), rep(lam_i), rep(ldt), b_r, b_i)
        tabf, tabr = _ssm_powers(f"ssm_powers_{j}", ar[::C].reshape(NG, 1, lanes), ai[::C].reshape(NG, 1, lanes),
                                 tcs // SCAN_SEQS)
        bd = lambda t: _block_diag(t.reshape(NG, GROUPS_PER_BLOCK, C, P), "bgcp,gh->bgchp").reshape(NG, 128, lanes)
        bbd = jnp.concatenate([bd(bbr), bd(bbi)], axis=2).astype(BF16)
        cd = lambda t: _block_diag(t.reshape(NG, GROUPS_PER_BLOCK, C, P), "bgcp,gh->bhpgc").reshape(NG, lanes, 128)
        ccd = jnp.concatenate([cd(p["ssm_c_re"][j]), -cd(p["ssm_c_im"][j])], axis=1).astype(BF16)
        ssm.append(dict(lam_r=lam_r, lam_i=lam_i, ldt=ldt, b_r=b_r, b_i=b_i, kr=kr, ki=ki, tabf=tabf, tabr=tabr,
                        bbd=bbd, ccd=ccd, dskip=row(p["ssm_d"][j])))

    dwp = [jnp.pad(p["conv_dw"][j], ((0, CONV_HALO - width), (0, 0))) for j in range(p["conv_dw"].shape[0])]

    def first_weight(i):
        if i >= depth:
            return []
        return [("conv_w_in" if i % 2 == 0 else "ssm_w_glu", i // 2)]

    def next_up(i):
        return [("mlp_w_up", i)] if i < depth else []

    units = missing(first_weight(0))
    if units:
        store(units, _run_jobs("weight_gather", [_gather_job([own[n][l] for n, l in units])]))

    saved = []
    x = xs
    for i in range(depth):
        j = i // 2
        s = dict(x_in=x)
        h = _rms_fwd(f"mix_norm_fwd_{i}", x, row(p["mix_norm"][i]), tr=tre)
        s["h"] = h
        if i % 2 == 0:
            def epi_in(accs, ex):
                a_, g_ = accs[0] + ex[0], accs[1] + ex[1]
                return [a_, g_, a_ * _sigmoid(g_)]
            b_in = row(p["conv_b_in"][j])
            a_, g_, v = nn_col(f"conv_in_{j}", h, "conv_w_in", j, [0, D], [(b_in, "n", 0), (b_in, "n", D)],
                               [((T, D), BF16), ((T, D), BF16), ((T, D), F32)], epi_in, tmh,
                               fetch=[("conv_w_out", j)] + next_up(i))
            sl = _conv_fwd(f"conv_fwd_{j}", v, dwp[j], row(p["conv_dw_b"][j]), row(p["conv_ln_g"][j]),
                           row(p["conv_ln_b"][j]), tr=trc, width=width)
            x = nn_row(f"conv_out_{j}", sl, "conv_w_out", j, [(row(p["conv_b_out"][j]), "n", 0), (x, "mn", 0)],
                       [((T, D), F32)], lambda accs, ex: [accs[0] + ex[0] + ex[1]])[0]
            s.update(a=a_, g=g_, v=v, s=sl)
        else:
            q = ssm[j]
            y, states, cins = _ssm_fwd(f"ssm_fwd_{j}", h, q["bbd"], q["ccd"], q["tabf"], q["dskip"], tc=tcs)
            def epi_glu(accs, ex):
                return [accs[0], accs[1], accs[0] * _sigmoid(accs[1]) + ex[0]]
            val, gate, x = nn_col(f"ssm_glu_{j}", y, "ssm_w_glu", j, [0, D], [(x, "mn", 0)],
                                  [((T, D), BF16), ((T, D), BF16), ((T, D), F32)], epi_glu, tmh, fetch=next_up(i))
            s.update(y=y, states=states, cins=cins, val=val, gate=gate)
        s["x_mid"] = x
        h2 = _rms_fwd(f"mlp_norm_fwd_{i}", x, row(p["mlp_norm"][i]), tr=tre)
        def epi_up(accs, ex):
            r_ = jnp.maximum(accs[0], 0.0)
            return [r_, r_ * r_]
        r, rsq = nn_col(f"mlp_up_{i}", h2, "mlp_w_up", i, [0], [], [((T, F), BF16), ((T, F), BF16)], epi_up, tm,
                        fetch=[("mlp_w_down", i)])
        x = nn_row(f"mlp_down_{i}", rsq, "mlp_w_down", i, [(x, "mn", 0)], [((T, D), F32)],
                   lambda accs, ex: [accs[0] + ex[0]], fetch=first_weight(i + 1) + next_up(i + 1))[0]
        s.update(h2=h2, r=r, rsq=rsq)
        saved.append(s)

    dx, dxb, loss8, dgf = _final_loss("final_loss", x, row(p["final_norm"]), tgt, tr=tre)

    n_conv, n_ssm = p["conv_dw"].shape[0], p["ssm_d"].shape[0]
    gs = {n: [None] * p[n].shape[0] for n in WEIGHTS if n not in LARGE and n != "final_norm"}
    gs.update(conv_w_in=[None] * n_conv, conv_w_out=[None] * n_conv, ssm_w_glu=[None] * n_ssm,
              mlp_w_up=[None] * depth, mlp_w_down=[None] * depth)
    gs["final_norm"] = dgf.reshape(-1)
    for i in reversed(range(depth)):
        j = i // 2
        s = saved[i]
        if reducer is not None:
            reducer.hurry = i == 0
        dz = nt_row(f"mlp_down_dx_{i}", dxb, "mlp_w_down", i, F, [s["r"]], [((T, F), BF16)],
                    lambda acc, ex: [acc * (2.0 * ex[0].astype(F32))])[0]
        got("mlp_w_down", i, tn(f"mlp_down_dw_{i}", s["rsq"], dxb, "row"))
        got("mlp_w_up", i, tn(f"mlp_up_dw_{i}", s["h2"], dz, "col"))
        dh2 = nt_col(f"mlp_up_dx_{i}", dz, "mlp_w_up", i, D, [], [((T, D), F32)], lambda acc, ex: [acc])[0]
        dx, dxb, dg, cs = _rms_bwd(f"mlp_norm_bwd_{i}", s["x_mid"], row(p["mlp_norm"][i]), dh2, dx, tr=tre)
        gs["mlp_norm"][i] = dg.reshape(-1)
        if i % 2 == 0:
            gs["conv_b_out"][j] = cs.reshape(-1)
            got("conv_w_out", j, tn(f"conv_out_dw_{j}", s["s"], dxb, "row"))
            dsl = nt_row(f"conv_out_dx_{j}", dxb, "conv_w_out", j, D, [], [((T, D), F32)],
                         lambda acc, ex: [acc])[0]
            dv2, dlg, dlb, ddwb = _conv_bwd_a(f"conv_bwd_a_{j}", s["v"], dsl, dwp[j], row(p["conv_dw_b"][j]),
                                              row(p["conv_ln_g"][j]), row(p["conv_ln_b"][j]), tr=trc, width=width)
            du, ddw, dbin = _conv_bwd_b(f"conv_bwd_b_{j}", dv2, s["v"], s["a"], s["g"], dwp[j], tr=trc, width=width)
            gs["conv_ln_g"][j], gs["conv_ln_b"][j] = dlg.reshape(-1), dlb.reshape(-1)
            gs["conv_dw_b"][j], gs["conv_dw"][j], gs["conv_b_in"][j] = ddwb.reshape(-1), ddw[:width], dbin.reshape(-1)
            got("conv_w_in", j, tn(f"conv_in_dw_{j}", s["h"], du, "col"))
            dh = nt_col(f"conv_in_dx_{j}", du, "conv_w_in", j, D, [], [((T, D), F32)], lambda acc, ex: [acc])[0]
        else:
            q = ssm[j]
            dz2 = _glu_bwd(f"ssm_glu_bwd_{j}", dx, s["val"], s["gate"], tr=tre)
            got("ssm_w_glu", j, tn(f"ssm_glu_dw_{j}", s["y"], dz2, "col"))
            dy = nt_col(f"ssm_glu_dx_{j}", dz2, "ssm_w_glu", j, D, [], [((T, D), F32)], lambda acc, ex: [acc])[0]
            dh, dbbd, dccd, da, dd = _ssm_bwd(f"ssm_bwd_{j}", s["h"], dy, s["states"], s["cins"], q["bbd"], q["ccd"],
                                              q["tabr"], q["dskip"], tc=tcs)
            gbb = _block_diag(dbbd.reshape(NG, GROUPS_PER_BLOCK, C, 2, GROUPS_PER_BLOCK, P), "bgcrhp,gh->rbgcp")
            dbr, dbi, tkr, tki = _ssm_prep_bwd_b(f"ssm_prep_bwd_b_{j}", q["kr"], q["ki"], q["b_r"], q["b_i"],
                                                 gbb[0].reshape(G * C, P), gbb[1].reshape(G * C, P))
            unrow = lambda t: t.reshape(G, C, P).transpose(0, 2, 1)
            gs["ssm_b_re"][j], gs["ssm_b_im"][j] = unrow(dbr), unrow(dbi)
            per_c = lambda t: t.reshape(G, C, P).transpose(1, 0, 2)
            dlr, dli, dldt = _ssm_prep_bwd_a(f"ssm_prep_bwd_a_{j}", q["lam_r"], q["lam_i"], q["ldt"],
                                             da[:, 0, :lanes].reshape(G, P), da[:, 0, lanes:].reshape(G, P),
                                             per_c(tkr), per_c(tki))
            gs["ssm_lambda_re"][j], gs["ssm_lambda_im"][j], gs["ssm_log_dt"][j] = dlr, dli, dldt[:, 0]
            gcc = _block_diag(dccd.reshape(NG, 2, GROUPS_PER_BLOCK, P, GROUPS_PER_BLOCK, C), "brhpgc,gh->rbgcp")
            gs["ssm_c_re"][j], gs["ssm_c_im"][j] = gcc[0].reshape(G, C, P), -gcc[1].reshape(G, C, P)
            gs["ssm_d"][j] = dd.reshape(-1)
        dx, dxb, dg, _ = _rms_bwd(f"mix_norm_bwd_{i}", s["x_in"], row(p["mix_norm"][i]), dh, dx, tr=tre)
        gs["mix_norm"][i] = dg.reshape(-1)

    small = {n: (gs[n] if n == "final_norm" else jnp.stack(gs[n])) for n in WEIGHTS if n not in LARGE}
    large = {n: gs[n] for n in LARGE}
    return loss8[0, 0], dx, small, large


def kernel(x, mix_norm, conv_w_in, conv_b_in, conv_dw, conv_dw_b, conv_ln_g, conv_ln_b, conv_w_out, conv_b_out, ssm_lambda_re, ssm_lambda_im, ssm_log_dt, ssm_b_re, ssm_b_im, ssm_c_re, ssm_c_im, ssm_d, ssm_w_glu, mlp_norm, mlp_w_up, mlp_w_down, final_norm, loss_target, m_mix_norm, m_conv_w_in, m_conv_b_in, m_conv_dw, m_conv_dw_b, m_conv_ln_g, m_conv_ln_b, m_conv_w_out, m_conv_b_out, m_ssm_lambda_re, m_ssm_lambda_im, m_ssm_log_dt, m_ssm_b_re, m_ssm_b_im, m_ssm_c_re, m_ssm_c_im, m_ssm_d, m_ssm_w_glu, m_mlp_norm, m_mlp_w_up, m_mlp_w_down, m_final_norm, v_mix_norm, v_conv_w_in, v_conv_b_in, v_conv_dw, v_conv_dw_b, v_conv_ln_g, v_conv_ln_b, v_conv_w_out, v_conv_b_out, v_ssm_lambda_re, v_ssm_lambda_im, v_ssm_log_dt, v_ssm_b_re, v_ssm_b_im, v_ssm_c_re, v_ssm_c_im, v_ssm_d, v_ssm_w_glu, v_mlp_norm, v_mlp_w_up, v_mlp_w_down, v_final_norm):
    a = dict(locals())
    S = N_CHIPS
    w = {n: a[n] for n in WEIGHTS}
    k_chip = 2 * lax.axis_index("x") + lax.axis_index("y")

    meta = jnp.stack([lax.axis_index("c"), k_chip]).astype(jnp.int32)
    own = {n: [w[n][l].astype(BF16) for l in range(w[n].shape[0])] for n in LARGE}
    wg = {n: [None] * len(own[n]) for n in LARGE}
    sh_shapes = [w[n].shape for n in SHARDED_SMALL]
    sh_all = _small_gather("small_weight_gather", _pack([w[n] for n in SHARDED_SMALL]))
    per_chip = [_unpack(sh_all[2 * k], sh_shapes) for k in range(S)]
    p = {n: w[n] for n in REPLICATED}
    for idx, n in enumerate(SHARDED_SMALL):
        p[n] = jnp.concatenate([per_chip[k][idx] for k in range(S)], axis=-1)

    reducer = _GradReducer(meta, {n: len(own[n]) for n in LARGE})
    loss_local, dx, small, _ = _local_step(x[0], loss_target[0], p, wg, S, own, meta, reducer)
    loss = lax.psum(loss_local, ("x", "y", "c"))
    grads = {n: st.reshape(w[n].shape) for n, st in reducer.finish().items()}

    small_names = REPLICATED + SHARDED_SMALL
    g_all = _small_gather("small_grad_gather", _pack([small[n] for n in small_names], SUM_ROWS))
    g_sum = _sum_devices("small_grad_sum", g_all)
    g_parts = dict(zip(small_names, _unpack(g_sum, [small[n].shape for n in small_names])))
    for n in REPLICATED:
        grads[n] = g_parts[n]
    for n in SHARDED_SMALL:
        cols = w[n].shape[-1]
        grads[n] = lax.dynamic_slice_in_dim(g_parts[n], k_chip * cols, cols, axis=g_parts[n].ndim - 1)

    delta, new_m, new_v = {}, {}, {}
    for n in LARGE + SHARDED_SMALL:
        delta[n], new_m[n], new_v[n] = _adamw(f"adamw_{n}", w[n], grads[n], a["m_" + n], a["v_" + n])
    rep_shapes = [w[n].shape for n in REPLICATED]
    rep_rows = sum(_packed_rows(s) for s in rep_shapes)
    res = _adamw("adamw_replicated", _pack([w[n] for n in REPLICATED]), g_sum[:rep_rows],
                 _pack([a["m_" + n] for n in REPLICATED]), _pack([a["v_" + n] for n in REPLICATED]))
    for dst, buf in zip((delta, new_m, new_v), res):
        dst.update(zip(REPLICATED, _unpack(buf, rep_shapes)))

    return (loss, dx[None], *[grads[n] for n in WEIGHTS], *[delta[n] for n in WEIGHTS],
            *[new_m[n] for n in WEIGHTS], *[new_v[n] for n in WEIGHTS])
```

```python
import functools
import math

import jax
import jax.numpy as jnp
from jax import lax
from jax.experimental import pallas as pl
from jax.experimental.pallas import tpu as pltpu

F32 = jnp.float32
BF16 = jnp.bfloat16
MESH = pl.DeviceIdType.MESH

EPS = 1e-6
ADAM_LR = 0.001
ADAM_B1 = 0.9
ADAM_B2 = 0.999
ADAM_EPS = 1e-08
ADAM_WD = 0.01
ADAM_STEP = 10

N_CHIPS = 4
N_DEV = 8
SSM_C = 16
SSM_P = 64
GROUPS_PER_BLOCK = 8
CONV_HALO = 32
VMEM_LIMIT = 56 * 1024 * 1024
MM_OUT_TILE = 1024
MM_K_TILE = 2048


def _pick(n, pref, align):
    t = min(n, pref)
    t -= t % align
    while t >= align:
        if n % t == 0:
            return t
        t -= align
    return n


def _params(*sem):
    return pltpu.CompilerParams(dimension_semantics=sem, vmem_limit_bytes=VMEM_LIMIT)


def _sigmoid(x):
    return 1.0 / (1.0 + jnp.exp(-x))


def _fold8(z):
    r, n = z.shape
    return jnp.sum(z.reshape(r // 8, 8, n), axis=0)


def _w_col(j, kt, nps, off):
    def idx(m, n, k):
        return ((n + off) // nps, j * kt + k, (n + off) % nps)
    return idx


def _w_row(j, kps):
    def idx(m, n, k):
        return (k // kps, j * kps + k % kps, n)
    return idx


JOB_MIDDLE_AT = 0.7


def _grid_call(name, grid, in_specs, args, out_specs, out_shape, scratch, compute, semantics, jobs=()):
    n_in, n_out, n_scr = len(args), len(out_shape), len(scratch)
    j_ins = [t for jb in jobs for t in jb.ins]
    j_outs = [t for jb in jobs for t in jb.outs]
    steps = math.prod(grid)

    def body(*refs):
        o0 = n_in + len(j_ins)
        s0 = o0 + n_out + len(j_outs)
        if jobs:
            start, middle, finish = _jobs_bind(jobs, refs[n_in:o0], refs[o0 + n_out:s0], refs[s0 + n_scr:])
            step = 0
            for d, size in enumerate(grid):
                step = step * size + pl.program_id(d)
            pl.when(step == 0)(start)
        compute(refs[:n_in], refs[o0:o0 + n_out], refs[s0:s0 + n_scr])
        if jobs:
            pl.when(step == min(int(JOB_MIDDLE_AT * steps), steps - 1))(middle)
            pl.when(step == steps - 1)(finish)

    return pl.pallas_call(
        body, name=name, grid=grid, in_specs=list(in_specs) + [ANY] * len(j_ins),
        out_specs=list(out_specs) + [ANY] * len(j_outs), out_shape=list(out_shape) + j_outs,
        scratch_shapes=list(scratch) + _jobs_sems(jobs),
        compiler_params=_params(*(("arbitrary",) * len(grid) if jobs else semantics)),
    )(*args, *j_ins)


MM_SEMANTICS = ("parallel", "parallel", "arbitrary")


def _mm_nn(name, a, w, w_maps, extras, outs, epilogue, *, tm, tn, tk, jobs=()):
    M, K = a.shape
    N = outs[0][0][1]
    grid = (M // tm, N // tn, K // tk)
    nk, nv = grid[2], len(w_maps)
    in_specs = [pl.BlockSpec((tm, tk), lambda m, n, k: (m, k))]
    args = [a]
    for wm in w_maps:
        in_specs.append(pl.BlockSpec((None, tk, tn), wm))
        args.append(w)
    for arr, kind, off in extras:
        if kind == "mn":
            in_specs.append(pl.BlockSpec((tm, tn), lambda m, n, k: (m, n)))
        else:
            in_specs.append(pl.BlockSpec((1, tn), functools.partial(lambda m, n, k, o: (0, n + o), o=off // tn)))
        args.append(arr)
    out_specs = [pl.BlockSpec((tm, tn), lambda m, n, k: (m, n)) for _ in outs]
    out_shape = [jax.ShapeDtypeStruct(s, d) for s, d in outs]

    def compute(ins, o_refs, acc_refs):
        a_ref, w_refs, e_refs = ins[0], ins[1:1 + nv], ins[1 + nv:]
        k = pl.program_id(2)

        def write(accs):
            for o_ref, r in zip(o_refs, epilogue(accs, [e[...] for e in e_refs])):
                o_ref[...] = r.astype(o_ref.dtype)

        av = a_ref[...].astype(BF16)
        if nk == 1:
            write([jnp.dot(av, w_ref[...], preferred_element_type=F32) for w_ref in w_refs])
            return

        @pl.when(k == 0)
        def _():
            for acc in acc_refs:
                acc[...] = jnp.zeros_like(acc)

        for acc, w_ref in zip(acc_refs, w_refs):
            acc[...] += jnp.dot(av, w_ref[...], preferred_element_type=F32)
        pl.when(k == nk - 1)(lambda: write([acc[...] for acc in acc_refs]))

    scratch = [pltpu.VMEM((tm, tn), F32) for _ in range(nv if nk > 1 else 0)]
    return _grid_call(name, grid, in_specs, args, out_specs, out_shape, scratch, compute, MM_SEMANTICS, jobs)


def _mm_nt(name, a, w, w_map, extras, outs, epilogue, *, tm, tr, tc, jobs=()):
    M, N = a.shape
    R = outs[0][0][1]
    grid = (M // tm, R // tr, N // tc)
    nc = grid[2]
    in_specs = [pl.BlockSpec((tm, tc), lambda m, r, c: (m, c)), pl.BlockSpec((None, tr, tc), w_map)]
    args = [a, w]
    for arr in extras:
        in_specs.append(pl.BlockSpec((tm, tr), lambda m, r, c: (m, r)))
        args.append(arr)
    out_specs = [pl.BlockSpec((tm, tr), lambda m, r, c: (m, r)) for _ in outs]
    out_shape = [jax.ShapeDtypeStruct(s, d) for s, d in outs]

    def compute(ins, o_refs, acc_refs):
        a_ref, w_ref, e_refs = ins[0], ins[1], ins[2:]

        def partial_product():
            return lax.dot_general(a_ref[...].astype(BF16), w_ref[...], (((1,), (1,)), ((), ())),
                                   preferred_element_type=F32)

        def write(acc):
            for o_ref, r in zip(o_refs, epilogue(acc, [e[...] for e in e_refs])):
                o_ref[...] = r.astype(o_ref.dtype)

        if nc == 1:
            write(partial_product())
            return
        acc = acc_refs[0]
        c = pl.program_id(2)

        @pl.when(c == 0)
        def _():
            acc[...] = jnp.zeros_like(acc)

        acc[...] += partial_product()
        pl.when(c == nc - 1)(lambda: write(acc[...]))

    scratch = [pltpu.VMEM((tm, tr), F32)] if nc > 1 else []
    return _grid_call(name, grid, in_specs, args, out_specs, out_shape, scratch, compute, MM_SEMANTICS, jobs)


def _nt_col(j, rt, cps):
    def idx(m, r, c):
        return (c // cps, j * rt + r, c % cps)
    return idx


def _nt_row(j, rps):
    def idx(m, r, c):
        return (r // rps, j * rps + r % rps, c)
    return idx


def _mm_tn(name, a, b, out_sharding, n_shards, *, tr, tc, tt, jobs=()):
    T, R = a.shape
    N = b.shape[1]
    S = n_shards
    grid = (R // tr, N // tc, T // tt)
    if out_sharding == "col":
        nps = (N // S) // tc
        out_shape = jax.ShapeDtypeStruct((S, R, N // S), F32)
        out_spec = pl.BlockSpec((None, tr, tc), lambda r, n, t: (n // nps, r, n % nps))
    else:
        rps = (R // S) // tr
        out_shape = jax.ShapeDtypeStruct((S, R // S, N), F32)
        out_spec = pl.BlockSpec((None, tr, tc), lambda r, n, t: (r // rps, r % rps, n))

    def compute(ins, outs, _):
        a_ref, b_ref, o_ref = ins[0], ins[1], outs[0]
        t = pl.program_id(2)

        @pl.when(t == 0)
        def _():
            o_ref[...] = jnp.zeros_like(o_ref)

        o_ref[...] += lax.dot_general(a_ref[...].astype(BF16), b_ref[...].astype(BF16), (((0,), (0,)), ((), ())),
                                      preferred_element_type=F32)

    in_specs = [pl.BlockSpec((tt, tr), lambda r, n, t: (t, r)), pl.BlockSpec((tt, tc), lambda r, n, t: (t, n))]
    return _grid_call(name, grid, in_specs, [a, b], [out_spec], [out_shape], [], compute, MM_SEMANTICS, jobs)


def _rms_fwd(name, x, g, *, tr):
    T, D = x.shape

    def body(x_ref, g_ref, h_ref):
        xv = x_ref[...]
        r = lax.rsqrt(jnp.mean(xv * xv, axis=-1, keepdims=True) + EPS)
        h_ref[...] = (xv * r * g_ref[...]).astype(BF16)

    return pl.pallas_call(
        body, name=name, grid=(T // tr,),
        in_specs=[pl.BlockSpec((tr, D), lambda i: (i, 0)), pl.BlockSpec((1, D), lambda i: (0, 0))],
        out_specs=pl.BlockSpec((tr, D), lambda i: (i, 0)),
        out_shape=jax.ShapeDtypeStruct((T, D), BF16),
        compiler_params=_params("parallel"),
    )(x, g)


def _rms_bwd_rows(xv, gv, dh):
    r = lax.rsqrt(jnp.mean(xv * xv, axis=-1, keepdims=True) + EPS)
    xh = xv * r
    gdy = dh * gv
    dx = r * (gdy - xh * jnp.mean(gdy * xh, axis=-1, keepdims=True))
    return dx, dh * xh


def _rms_bwd(name, x, g, dh, dx_in, *, tr):
    T, D = x.shape
    nt = T // tr

    def body(x_ref, g_ref, dh_ref, dxi_ref, dx_ref, dxb_ref, dg_ref, cs_ref, dg_acc, cs_acc):
        i = pl.program_id(0)

        @pl.when(i == 0)
        def _():
            dg_acc[...] = jnp.zeros_like(dg_acc)
            cs_acc[...] = jnp.zeros_like(cs_acc)

        dx, dgx = _rms_bwd_rows(x_ref[...], g_ref[...], dh_ref[...].astype(F32))
        dxo = dxi_ref[...] + dx
        dx_ref[...] = dxo
        dxb_ref[...] = dxo.astype(BF16)
        dg_acc[...] += _fold8(dgx)
        cs_acc[...] += _fold8(dxo)

        @pl.when(i == nt - 1)
        def _():
            dg_ref[...] = jnp.sum(dg_acc[...], axis=0, keepdims=True)
            cs_ref[...] = jnp.sum(cs_acc[...], axis=0, keepdims=True)

    row = pl.BlockSpec((tr, D), lambda i: (i, 0))
    vec = pl.BlockSpec((1, D), lambda i: (0, 0))
    return pl.pallas_call(
        body, name=name, grid=(nt,),
        in_specs=[row, vec, row, row], out_specs=[row, row, vec, vec],
        out_shape=[jax.ShapeDtypeStruct((T, D), F32), jax.ShapeDtypeStruct((T, D), BF16),
                   jax.ShapeDtypeStruct((1, D), F32), jax.ShapeDtypeStruct((1, D), F32)],
        scratch_shapes=[pltpu.VMEM((8, D), F32), pltpu.VMEM((8, D), F32)],
        compiler_params=_params("arbitrary"),
    )(x, g, dh, dx_in)


def _final_loss(name, x, g, target, *, tr):
    T, D = x.shape
    nt = T // tr

    def body(x_ref, g_ref, t_ref, dx_ref, dxb_ref, loss_ref, dg_ref, l_acc, dg_acc):
        i = pl.program_id(0)

        @pl.when(i == 0)
        def _():
            l_acc[...] = jnp.zeros_like(l_acc)
            dg_acc[...] = jnp.zeros_like(dg_acc)

        xv = x_ref[...]
        gv = g_ref[...]
        r = lax.rsqrt(jnp.mean(xv * xv, axis=-1, keepdims=True) + EPS)
        err = xv * r * gv - t_ref[...]
        l_acc[...] += _fold8(err * err)
        dx, dgx = _rms_bwd_rows(xv, gv, err * (1.0 / D))
        dx_ref[...] = dx
        dxb_ref[...] = dx.astype(BF16)
        dg_acc[...] += _fold8(dgx)

        @pl.when(i == nt - 1)
        def _():
            tot = jnp.sum(jnp.sum(l_acc[...], axis=0, keepdims=True), axis=1, keepdims=True)
            loss_ref[...] = jnp.broadcast_to(tot * (0.5 / D), loss_ref.shape)
            dg_ref[...] = jnp.sum(dg_acc[...], axis=0, keepdims=True)

    row = pl.BlockSpec((tr, D), lambda i: (i, 0))
    vec = pl.BlockSpec((1, D), lambda i: (0, 0))
    return pl.pallas_call(
        body, name=name, grid=(nt,),
        in_specs=[row, vec, row],
        out_specs=[row, row, pl.BlockSpec((8, 128), lambda i: (0, 0)), vec],
        out_shape=[jax.ShapeDtypeStruct((T, D), F32), jax.ShapeDtypeStruct((T, D), BF16),
                   jax.ShapeDtypeStruct((8, 128), F32), jax.ShapeDtypeStruct((1, D), F32)],
        scratch_shapes=[pltpu.VMEM((8, D), F32), pltpu.VMEM((8, D), F32)],
        compiler_params=_params("arbitrary"),
    )(x, g, target)


CONV_ROWS = 64
CONV_LANES = 128


def _tap_windows(ext_ref, r0, cols, first, width):
    last = first + width - 1
    for r in range(8):
        qs = [q for q in range(last // 8 + 1) if first <= 8 * q + r <= last]
        if not qs:
            continue
        n = CONV_ROWS + 8 * qs[-1] + (8 if r else 0)
        win = ext_ref[pl.ds(r0, n), cols]
        if r:
            win = pltpu.roll(win, n - r, 0)
        for q in qs:
            yield 8 * q + r - first, win[8 * q:8 * q + CONV_ROWS]


def _conv_blocks(tr, D):
    for cb in range(D // CONV_LANES):
        for rb in range(tr // CONV_ROWS):
            yield rb * CONV_ROWS, pl.ds(cb * CONV_LANES, CONV_LANES)


def _conv_taps(ext_ref, dwp_ref, out_ref, tr, width, first):
    for r0, cols in _conv_blocks(tr, out_ref.shape[1]):
        acc = None
        for j, win in _tap_windows(ext_ref, r0, cols, first, width):
            term = dwp_ref[pl.ds(j, 1), cols] * win
            acc = term if acc is None else acc + term
        out_ref[pl.ds(r0, CONV_ROWS), cols] = acc


def _ln_rows(v2, lg, lb):
    mu = jnp.mean(v2, axis=-1, keepdims=True)
    xc = v2 - mu
    rs = lax.rsqrt(jnp.mean(xc * xc, axis=-1, keepdims=True) + EPS)
    xh = xc * rs
    return xh, rs, xh * lg + lb


def _halo_prev(tr):
    q = tr // CONV_HALO
    return lambda i: (jnp.maximum(i * q - 1, 0), 0)


def _conv_fwd(name, v, dwp, dwb, lg, lb, *, tr, width):
    T, D = v.shape
    first = CONV_HALO - (width - 1)

    def body(v_ref, halo_ref, dwp_ref, dwb_ref, lg_ref, lb_ref, s_ref, ext, conv):
        i = pl.program_id(0)
        ext[pl.ds(0, CONV_HALO), :] = jnp.where(i == 0, 0.0, halo_ref[...])
        ext[pl.ds(CONV_HALO, tr), :] = v_ref[...]
        _conv_taps(ext, dwp_ref, conv, tr, width, first)
        _, _, ln = _ln_rows(conv[...] + dwb_ref[...], lg_ref[...], lb_ref[...])
        s_ref[...] = (ln * _sigmoid(ln)).astype(BF16)

    row = pl.BlockSpec((tr, D), lambda i: (i, 0))
    vec = pl.BlockSpec((1, D), lambda i: (0, 0))
    return pl.pallas_call(
        body, name=name, grid=(T // tr,),
        in_specs=[row, pl.BlockSpec((CONV_HALO, D), _halo_prev(tr)),
                  pl.BlockSpec((CONV_HALO, D), lambda i: (0, 0)), vec, vec, vec],
        out_specs=row, out_shape=jax.ShapeDtypeStruct((T, D), BF16),
        scratch_shapes=[pltpu.VMEM((tr + CONV_HALO, D), F32), pltpu.VMEM((tr, D), F32)],
        compiler_params=_params("parallel"),
    )(v, v, dwp, dwb, lg, lb)


def _conv_bwd_a(name, v, ds, dwp, dwb, lg, lb, *, tr, width):
    T, D = v.shape
    nt = T // tr
    first = CONV_HALO - (width - 1)

    def body(v_ref, halo_ref, ds_ref, dwp_ref, dwb_ref, lg_ref, lb_ref,
             dv2_ref, dlg_ref, dlb_ref, ddwb_ref, ext, conv, a_lg, a_lb, a_dwb):
        i = pl.program_id(0)

        @pl.when(i == 0)
        def _():
            a_lg[...] = jnp.zeros_like(a_lg)
            a_lb[...] = jnp.zeros_like(a_lb)
            a_dwb[...] = jnp.zeros_like(a_dwb)

        ext[pl.ds(0, CONV_HALO), :] = jnp.where(i == 0, 0.0, halo_ref[...])
        ext[pl.ds(CONV_HALO, tr), :] = v_ref[...]
        _conv_taps(ext, dwp_ref, conv, tr, width, first)
        v2 = conv[...] + dwb_ref[...]
        lgv = lg_ref[...]
        xh, rs, ln = _ln_rows(v2, lgv, lb_ref[...])
        sg = _sigmoid(ln)
        dln = ds_ref[...] * (sg * (1.0 + ln * (1.0 - sg)))
        a_lg[...] += _fold8(dln * xh)
        a_lb[...] += _fold8(dln)
        dxh = dln * lgv
        dv2 = rs * (dxh - jnp.mean(dxh, axis=-1, keepdims=True)
                    - xh * jnp.mean(dxh * xh, axis=-1, keepdims=True))
        dv2_ref[...] = dv2
        a_dwb[...] += _fold8(dv2)

        @pl.when(i == nt - 1)
        def _():
            dlg_ref[...] = jnp.sum(a_lg[...], axis=0, keepdims=True)
            dlb_ref[...] = jnp.sum(a_lb[...], axis=0, keepdims=True)
            ddwb_ref[...] = jnp.sum(a_dwb[...], axis=0, keepdims=True)

    row = pl.BlockSpec((tr, D), lambda i: (i, 0))
    vec = pl.BlockSpec((1, D), lambda i: (0, 0))
    return pl.pallas_call(
        body, name=name, grid=(nt,),
        in_specs=[row, pl.BlockSpec((CONV_HALO, D), _halo_prev(tr)), row,
                  pl.BlockSpec((CONV_HALO, D), lambda i: (0, 0)), vec, vec, vec],
        out_specs=[row, vec, vec, vec],
        out_shape=[jax.ShapeDtypeStruct((T, D), F32)] + [jax.ShapeDtypeStruct((1, D), F32)] * 3,
        scratch_shapes=[pltpu.VMEM((tr + CONV_HALO, D), F32), pltpu.VMEM((tr, D), F32)] + [pltpu.VMEM((8, D), F32)] * 3,
        compiler_params=_params("arbitrary"),
    )(v, v, ds, dwp, dwb, lg, lb)


def _conv_bwd_b(name, dv2, v, a, g, dwp, *, tr, width):
    T, D = v.shape
    nt = T // tr
    q = tr // CONV_HALO
    first = CONV_HALO - (width - 1)
    last_halo = T // CONV_HALO - 1

    def body(dv2_ref, nxt_ref, v_ref, halo_ref, a_ref, g_ref, dwp_ref,
             du_ref, ddw_ref, dbin_ref, ext_v, ext_d, dvs, a_dw, a_b):
        i = pl.program_id(0)

        @pl.when(i == 0)
        def _():
            a_dw[...] = jnp.zeros_like(a_dw)
            a_b[...] = jnp.zeros_like(a_b)

        ext_v[pl.ds(0, CONV_HALO), :] = jnp.where(i == 0, 0.0, halo_ref[...])
        ext_v[pl.ds(CONV_HALO, tr), :] = v_ref[...]
        ext_d[pl.ds(0, tr), :] = dv2_ref[...]
        ext_d[pl.ds(tr, CONV_HALO), :] = jnp.where(i == nt - 1, 0.0, nxt_ref[...])
        for r0, cols in _conv_blocks(tr, D):
            dv = None
            for o, win in _tap_windows(ext_d, r0, cols, 0, width):
                term = dwp_ref[pl.ds(width - 1 - o, 1), cols] * win
                dv = term if dv is None else dv + term
            dvs[pl.ds(r0, CONV_ROWS), cols] = dv
            d_cur = ext_d[pl.ds(r0, CONV_ROWS), cols]
            for j, win in _tap_windows(ext_v, r0, cols, first, width):
                a_dw[j, :, cols] += _fold8(d_cur * win)
        dv = dvs[...]
        av = a_ref[...].astype(F32)
        sg = _sigmoid(g_ref[...].astype(F32))
        da = dv * sg
        dg = dv * av * sg * (1.0 - sg)
        du_ref[:, pl.ds(0, D)] = da.astype(BF16)
        du_ref[:, pl.ds(D, D)] = dg.astype(BF16)
        a_b[:, pl.ds(0, D)] += _fold8(da)
        a_b[:, pl.ds(D, D)] += _fold8(dg)

        @pl.when(i == nt - 1)
        def _():
            ddw_ref[...] = jnp.sum(a_dw[...], axis=1)
            dbin_ref[...] = jnp.sum(a_b[...], axis=0, keepdims=True)

    row = pl.BlockSpec((tr, D), lambda i: (i, 0))
    return pl.pallas_call(
        body, name=name, grid=(nt,),
        in_specs=[row, pl.BlockSpec((CONV_HALO, D), lambda i: (jnp.minimum((i + 1) * q, last_halo), 0)),
                  row, pl.BlockSpec((CONV_HALO, D), _halo_prev(tr)), row, row,
                  pl.BlockSpec((CONV_HALO, D), lambda i: (0, 0))],
        out_specs=[pl.BlockSpec((tr, 2 * D), lambda i: (i, 0)),
                   pl.BlockSpec((CONV_HALO, D), lambda i: (0, 0)),
                   pl.BlockSpec((1, 2 * D), lambda i: (0, 0))],
        out_shape=[jax.ShapeDtypeStruct((T, 2 * D), BF16), jax.ShapeDtypeStruct((CONV_HALO, D), F32),
                   jax.ShapeDtypeStruct((1, 2 * D), F32)],
        scratch_shapes=[pltpu.VMEM((tr + CONV_HALO, D), F32), pltpu.VMEM((tr + CONV_HALO, D), F32),
                        pltpu.VMEM((tr, D), F32), pltpu.VMEM((CONV_HALO, 8, D), F32), pltpu.VMEM((8, 2 * D), F32)],
        compiler_params=_params("arbitrary"),
    )(dv2, dv2, v, v, a, g, dwp)


def _glu_bwd(name, dout, val, gate, *, tr):
    T, D = dout.shape

    def body(d_ref, v_ref, g_ref, dz_ref):
        d = d_ref[...]
        sg = _sigmoid(g_ref[...].astype(F32))
        dz_ref[:, pl.ds(0, D)] = (d * sg).astype(BF16)
        dz_ref[:, pl.ds(D, D)] = (d * v_ref[...].astype(F32) * sg * (1.0 - sg)).astype(BF16)

    row = pl.BlockSpec((tr, D), lambda i: (i, 0))
    return pl.pallas_call(
        body, name=name, grid=(T // tr,), in_specs=[row, row, row],
        out_specs=pl.BlockSpec((tr, 2 * D), lambda i: (i, 0)),
        out_shape=jax.ShapeDtypeStruct((T, 2 * D), BF16),
        compiler_params=_params("parallel"),
    )(dout, val, gate)


GELU_C = math.sqrt(2.0 / math.pi)
GELU_A = 0.044715


def _gelu(x):
    return 0.5 * x * (1.0 + jnp.tanh(GELU_C * (x + GELU_A * x * x * x)))


def _gelu_grad(x):
    t = jnp.tanh(GELU_C * (x + GELU_A * x * x * x))
    return 0.5 * (1.0 + t) + 0.5 * x * (1.0 - t * t) * GELU_C * (1.0 + 3.0 * GELU_A * x * x)


def _cmul(ar, ai, br, bi):
    return ar * br - ai * bi, ar * bi + ai * br


SCAN_SEQS = 8


def _interleave_rows(x, tmp_ref):
    n = x.shape[0]
    tmp_ref[...] = x
    return jnp.concatenate([tmp_ref[pl.ds(i, SCAN_SEQS, stride=n // SCAN_SEQS), :] for i in range(n // SCAN_SEQS)],
                           axis=0)


def _deinterleave_rows(x, tmp_ref):
    n = x.shape[0]
    tmp_ref[...] = x
    return jnp.concatenate([tmp_ref[pl.ds(s, n // SCAN_SEQS, stride=SCAN_SEQS), :] for s in range(SCAN_SEQS)], axis=0)


def _scan_chunk(xr, xi, tab_ref, cin_r, cin_i, *, lanes, reverse):
    n = xr.shape[0]
    L = n // SCAN_SEQS
    re, im = pl.ds(0, lanes), pl.ds(lanes, lanes)
    one, top = (L - 1, 0) if reverse else (0, L - 1)
    a_r = jnp.broadcast_to(tab_ref[pl.ds(one, 1), re], (SCAN_SEQS, lanes))
    a_i = jnp.broadcast_to(tab_ref[pl.ds(one, 1), im], (SCAN_SEQS, lanes))
    sr = si = jnp.zeros((SCAN_SEQS, lanes), F32)
    loc_r, loc_i = [None] * L, [None] * L
    for i in (reversed(range(L)) if reverse else range(L)):
        rows = slice(SCAN_SEQS * i, SCAN_SEQS * (i + 1))
        sr, si = a_r * sr - a_i * si + xr[rows], a_r * si + a_i * sr + xi[rows]
        loc_r[i], loc_i[i] = sr, si
    top_r, top_i = tab_ref[pl.ds(top, 1), re], tab_ref[pl.ds(top, 1), im]
    sub = lax.broadcasted_iota(jnp.int32, (SCAN_SEQS, lanes), 0)
    cr, ci = cin_r, cin_i
    in_r = in_i = jnp.zeros((SCAN_SEQS, lanes), F32)
    for s in (reversed(range(SCAN_SEQS)) if reverse else range(SCAN_SEQS)):
        in_r, in_i = jnp.where(sub == s, cr, in_r), jnp.where(sub == s, ci, in_i)
        dr, di = _cmul(top_r, top_i, cr, ci)
        cr, ci = dr + sr[s:s + 1], di + si[s:s + 1]
    out_r, out_i = [None] * L, [None] * L
    for i in range(L):
        dr, di = _cmul(tab_ref[pl.ds(i, 1), re], tab_ref[pl.ds(i, 1), im], in_r, in_i)
        out_r[i], out_i[i] = loc_r[i] + dr, loc_i[i] + di
    return jnp.concatenate(out_r, axis=0), jnp.concatenate(out_i, axis=0), cr, ci


def _ssm_fwd(name, h, bbd, ccd, tab, dskip, *, tc):
    T, D = h.shape
    NG, CB, L2 = bbd.shape
    lanes = L2 // 2
    nch = T // tc

    def body(h_ref, bb_ref, cc_ref, tab_ref, d_ref, y_ref, s_ref, cin_ref, carry, tmp):
        t = pl.program_id(1)

        @pl.when(t == 0)
        def _():
            carry[...] = jnp.zeros_like(carry)

        cin_ref[...] = carry[...]
        uf = _interleave_rows(h_ref[...].astype(F32), tmp)
        bu = jnp.dot(uf.astype(BF16), bb_ref[...], preferred_element_type=F32)
        sr, si, cr, ci = _scan_chunk(bu[:, :lanes], bu[:, lanes:], tab_ref,
                                     carry[pl.ds(0, 1), pl.ds(0, lanes)], carry[pl.ds(0, 1), pl.ds(lanes, lanes)],
                                     lanes=lanes, reverse=False)
        carry[pl.ds(0, 1), pl.ds(0, lanes)] = cr
        carry[pl.ds(0, 1), pl.ds(lanes, lanes)] = ci
        s = jnp.concatenate([sr, si], axis=1).astype(BF16)
        s_ref[...] = s
        yp = jnp.dot(s, cc_ref[...], preferred_element_type=F32) + d_ref[...] * uf
        y_ref[...] = _deinterleave_rows(_gelu(yp), tmp).astype(BF16)

    return pl.pallas_call(
        body, name=name, grid=(NG, nch),
        in_specs=[pl.BlockSpec((tc, CB), lambda b, t: (t, b)),
                  pl.BlockSpec((None, CB, L2), lambda b, t: (b, 0, 0)),
                  pl.BlockSpec((None, L2, CB), lambda b, t: (b, 0, 0)),
                  pl.BlockSpec((None,) + tab.shape[1:], lambda b, t: (b, 0, 0)),
                  pl.BlockSpec((1, CB), lambda b, t: (0, b))],
        out_specs=[pl.BlockSpec((tc, CB), lambda b, t: (t, b)),
                   pl.BlockSpec((tc, L2), lambda b, t: (t, b)),
                   pl.BlockSpec((None, None, 8, L2), lambda b, t: (b, t, 0, 0))],
        out_shape=[jax.ShapeDtypeStruct((T, D), BF16), jax.ShapeDtypeStruct((T, NG * L2), BF16),
                   jax.ShapeDtypeStruct((NG, nch, 8, L2), F32)],
        scratch_shapes=[pltpu.VMEM((8, L2), F32), pltpu.VMEM((tc, CB), F32)],
        compiler_params=_params("parallel", "arbitrary"),
    )(h, bbd, ccd, tab, dskip)


def _ssm_bwd(name, h, dy, states, cins, bbd, ccd, tabr, dskip, *, tc):
    T, D = h.shape
    NG, CB, L2 = bbd.shape
    lanes = L2 // 2
    nch = T // tc

    def body(h_ref, dy_ref, s_ref, cin_ref, bb_ref, cc_ref, tabr_ref, d_ref,
             dh_ref, dbb_ref, dcc_ref, da_ref, dd_ref, gcarry, a_da, a_dd, tmp):
        t = pl.program_id(1)

        @pl.when(t == 0)
        def _():
            gcarry[...] = jnp.zeros_like(gcarry)
            a_da[...] = jnp.zeros_like(a_da)
            a_dd[...] = jnp.zeros_like(a_dd)
            dbb_ref[...] = jnp.zeros_like(dbb_ref)
            dcc_ref[...] = jnp.zeros_like(dcc_ref)

        uf = _interleave_rows(h_ref[...].astype(F32), tmp)
        u = uf.astype(BF16)
        dyv = _interleave_rows(dy_ref[...], tmp)
        cin_r = cin_ref[pl.ds(0, 1), pl.ds(0, lanes)]
        cin_i = cin_ref[pl.ds(0, 1), pl.ds(lanes, lanes)]
        s = s_ref[...]
        dv = d_ref[...]
        yp = jnp.dot(s, cc_ref[...], preferred_element_type=F32) + dv * uf
        dyp = dyv * _gelu_grad(yp)
        a_dd[...] += _fold8(dyp * uf)
        dypb = dyp.astype(BF16)
        dcc_ref[...] += lax.dot_general(s, dypb, (((0,), (0,)), ((), ())), preferred_element_type=F32)
        ds = lax.dot_general(dypb, cc_ref[...], (((1,), (1,)), ((), ())), preferred_element_type=F32)
        gr, gi, cr, ci = _scan_chunk(ds[:, :lanes], ds[:, lanes:], tabr_ref,
                                     gcarry[pl.ds(0, 1), pl.ds(0, lanes)], gcarry[pl.ds(0, 1), pl.ds(lanes, lanes)],
                                     lanes=lanes, reverse=True)
        gcarry[pl.ds(0, 1), pl.ds(0, lanes)] = cr
        gcarry[pl.ds(0, 1), pl.ds(lanes, lanes)] = ci
        gb = jnp.concatenate([gr, gi], axis=1).astype(BF16)
        du = lax.dot_general(gb, bb_ref[...], (((1,), (1,)), ((), ())), preferred_element_type=F32)
        dh_ref[...] = _deinterleave_rows(du + dv * dyp, tmp)
        dbb_ref[...] += lax.dot_general(u, gb, (((0,), (0,)), ((), ())), preferred_element_type=F32)
        sr, si = s[:, :lanes].astype(F32), s[:, lanes:].astype(F32)
        first = lax.broadcasted_iota(jnp.int32, (SCAN_SEQS, lanes), 0) == 0
        head_r = jnp.where(first, cin_r, pltpu.roll(sr[tc - SCAN_SEQS:], 1, 0))
        head_i = jnp.where(first, cin_i, pltpu.roll(si[tc - SCAN_SEQS:], 1, 0))
        pr = jnp.concatenate([head_r, sr[:tc - SCAN_SEQS]], axis=0)
        pi = jnp.concatenate([head_i, si[:tc - SCAN_SEQS]], axis=0)
        a_da[:, pl.ds(0, lanes)] += _fold8(pr * gr + pi * gi)
        a_da[:, pl.ds(lanes, lanes)] += _fold8(pr * gi - pi * gr)

        @pl.when(t == nch - 1)
        def _():
            da_ref[...] = jnp.sum(a_da[...], axis=0, keepdims=True)
            dd_ref[...] = jnp.sum(a_dd[...], axis=0, keepdims=True)

    rev = lambda b, t: (nch - 1 - t, b)
    return pl.pallas_call(
        body, name=name, grid=(NG, nch),
        in_specs=[pl.BlockSpec((tc, CB), rev), pl.BlockSpec((tc, CB), rev), pl.BlockSpec((tc, L2), rev),
                  pl.BlockSpec((None, None, 8, L2), lambda b, t: (b, nch - 1 - t, 0, 0)),
                  pl.BlockSpec((None, CB, L2), lambda b, t: (b, 0, 0)),
                  pl.BlockSpec((None, L2, CB), lambda b, t: (b, 0, 0)),
                  pl.BlockSpec((None,) + tabr.shape[1:], lambda b, t: (b, 0, 0)),
                  pl.BlockSpec((1, CB), lambda b, t: (0, b))],
        out_specs=[pl.BlockSpec((tc, CB), rev),
                   pl.BlockSpec((None, CB, L2), lambda b, t: (b, 0, 0)),
                   pl.BlockSpec((None, L2, CB), lambda b, t: (b, 0, 0)),
                   pl.BlockSpec((None, 1, L2), lambda b, t: (b, 0, 0)),
                   pl.BlockSpec((1, CB), lambda b, t: (0, b))],
        out_shape=[jax.ShapeDtypeStruct((T, D), F32), jax.ShapeDtypeStruct((NG, CB, L2), F32),
                   jax.ShapeDtypeStruct((NG, L2, CB), F32), jax.ShapeDtypeStruct((NG, 1, L2), F32),
                   jax.ShapeDtypeStruct((1, D), F32)],
        scratch_shapes=[pltpu.VMEM((8, L2), F32), pltpu.VMEM((8, L2), F32), pltpu.VMEM((8, CB), F32),
                        pltpu.VMEM((tc, CB), F32)],
        compiler_params=_params("parallel", "arbitrary"),
    )(h, dy, states, cins, bbd, ccd, tabr, dskip)


def _zoh(lr, li, ldt):
    dt = jnp.exp(ldt)
    mag = jnp.exp(lr * dt)
    ar = mag * jnp.cos(li * dt)
    ai = mag * jnp.sin(li * dt)
    den = lr * lr + li * li
    nr = ar - 1.0
    kr = (nr * lr + ai * li) / den
    ki = (ai * lr - nr * li) / den
    return dt, ar, ai, kr, ki, den


def _ssm_prep(name, lr, li, ldt, br, bi):
    shp = jax.ShapeDtypeStruct(lr.shape, F32)

    def body(lr_ref, li_ref, ldt_ref, br_ref, bi_ref, ar_ref, ai_ref, kr_ref, ki_ref, bbr_ref, bbi_ref):
        _, ar, ai, kr, ki, _ = _zoh(lr_ref[...], li_ref[...], ldt_ref[...])
        ar_ref[...] = ar
        ai_ref[...] = ai
        kr_ref[...] = kr
        ki_ref[...] = ki
        bbr, bbi = _cmul(kr, ki, br_ref[...], bi_ref[...])
        bbr_ref[...] = bbr
        bbi_ref[...] = bbi

    return pl.pallas_call(body, name=name, out_shape=[shp] * 6)(lr, li, ldt, br, bi)


def _ssm_powers(name, ar, ai, rows):
    NG, _, lanes = ar.shape

    def body(ar_ref, ai_ref, tf_ref, tr_ref):
        a_r, a_i = ar_ref[...], ai_ref[...]
        pw = [(a_r, a_i)]
        for _ in range(7):
            pw.append(_cmul(pw[-1][0], pw[-1][1], a_r, a_i))
        row = lax.broadcasted_iota(jnp.int32, (8, lanes), 0)
        fr = fi = rr = ri = jnp.zeros((8, lanes), F32)
        for n in range(8):
            fr = jnp.where(row == n, pw[n][0], fr)
            fi = jnp.where(row == n, pw[n][1], fi)
            rr = jnp.where(row == 7 - n, pw[n][0], rr)
            ri = jnp.where(row == 7 - n, pw[n][1], ri)
        top_r, top_i = pw[7]
        size = 8
        while size < rows:
            hr, hi = _cmul(fr, fi, top_r, top_i)
            fr, fi = jnp.concatenate([fr, hr], axis=0), jnp.concatenate([fi, hi], axis=0)
            hr, hi = _cmul(rr, ri, top_r, top_i)
            rr, ri = jnp.concatenate([hr, rr], axis=0), jnp.concatenate([hi, ri], axis=0)
            top_r, top_i = _cmul(top_r, top_i, top_r, top_i)
            size *= 2
        tf_ref[:, pl.ds(0, lanes)] = fr
        tf_ref[:, pl.ds(lanes, lanes)] = fi
        tr_ref[:, pl.ds(0, lanes)] = rr
        tr_ref[:, pl.ds(lanes, lanes)] = -ri

    vec = pl.BlockSpec((None, 1, lanes), lambda b: (b, 0, 0))
    tab = pl.BlockSpec((None, rows, 2 * lanes), lambda b: (b, 0, 0))
    shp = jax.ShapeDtypeStruct((NG, rows, 2 * lanes), F32)
    return pl.pallas_call(body, name=name, grid=(NG,), in_specs=[vec, vec], out_specs=[tab, tab],
                          out_shape=[shp, shp], compiler_params=_params("parallel"))(ar, ai)


def _ssm_prep_bwd_b(name, kr, ki, br, bi, gbr, gbi):
    shp = jax.ShapeDtypeStruct(kr.shape, F32)

    def body(kr_ref, ki_ref, br_ref, bi_ref, gr_ref, gi_ref, dbr_ref, dbi_ref, tr_ref, ti_ref):
        gr, gi = gr_ref[...], gi_ref[...]
        dbr, dbi = _cmul(kr_ref[...], -ki_ref[...], gr, gi)
        dbr_ref[...] = dbr
        dbi_ref[...] = dbi
        t_r, t_i = _cmul(br_ref[...], -bi_ref[...], gr, gi)
        tr_ref[...] = t_r
        ti_ref[...] = t_i

    return pl.pallas_call(body, name=name, out_shape=[shp] * 4)(kr, ki, br, bi, gbr, gbi)


def _ssm_prep_bwd_a(name, lr, li, ldt, gar, gai, tkr, tki):
    G, P = lr.shape

    def body(lr_ref, li_ref, ldt_ref, gar_ref, gai_ref, tkr_ref, tki_ref, dlr_ref, dli_ref, dldt_ref):
        lr_v, li_v = lr_ref[...], li_ref[...]
        dt, ar, ai, kr, ki, den = _zoh(lr_v, li_v, ldt_ref[...])
        gkr = jnp.sum(tkr_ref[...], axis=0)
        gki = jnp.sum(tki_ref[...], axis=0)
        ir, ii = lr_v / den, -li_v / den
        t_r, t_i = _cmul(ir, -ii, gkr, gki)
        gar_t, gai_t = gar_ref[...] + t_r, gai_ref[...] + t_i
        qr, qi = _cmul(kr, ki, ir, ii)
        t_r, t_i = _cmul(-qr, qi, gkr, gki)
        u_r, u_i = _cmul(dt * ar, -dt * ai, gar_t, gai_t)
        dlr_ref[...] = u_r + t_r
        dli_ref[...] = u_i + t_i
        la_r, la_i = _cmul(lr_v, li_v, ar, ai)
        w_r, _ = _cmul(la_r, -la_i, gar_t, gai_t)
        ddt = jnp.sum(w_r, axis=1, keepdims=True)
        dldt_ref[...] = jnp.broadcast_to(ddt * dt[:, 0:1], dldt_ref.shape)

    shp = jax.ShapeDtypeStruct((G, P), F32)
    return pl.pallas_call(body, name=name, out_shape=[shp, shp, jax.ShapeDtypeStruct((G, 128), F32)])(
        lr, li, ldt, gar, gai, tkr, tki)


ROWS_CALL_TILE_ELEMS = 256 * 1024


def _rows_call(name, fn, ins, outs):
    R, Cn = ins[0].shape
    tr = _pick(R, max(16, ROWS_CALL_TILE_ELEMS // Cn), 16)
    spec = pl.BlockSpec((tr, Cn), lambda i: (i, 0))

    def body(*refs):
        res = fn(*[r[...] for r in refs[:len(ins)]])
        for o_ref, r in zip(refs[len(ins):], res):
            o_ref[...] = r.astype(o_ref.dtype)

    return pl.pallas_call(
        body, name=name, grid=(R // tr,), in_specs=[spec] * len(ins), out_specs=[spec] * len(outs),
        out_shape=[jax.ShapeDtypeStruct((R, Cn), d) for d in outs], compiler_params=_params("parallel"),
    )(*ins)


def _adamw_math(w, g, m, v):
    m = ADAM_B1 * m + (1.0 - ADAM_B1) * g
    v = ADAM_B2 * v + (1.0 - ADAM_B2) * (g * g)
    m_hat = m / (1.0 - ADAM_B1 ** ADAM_STEP)
    v_hat = v / (1.0 - ADAM_B2 ** ADAM_STEP)
    delta = -ADAM_LR * (m_hat / (jnp.sqrt(v_hat) + ADAM_EPS) + ADAM_WD * w)
    return delta, m, v


def _adamw(name, w, g, m, v):
    shape = w.shape
    cols = shape[-1]
    to2d = lambda t: t.reshape(-1, cols)
    res = _rows_call(name, _adamw_math, [to2d(w), to2d(g), to2d(m), to2d(v)], [F32, F32, F32])
    return [r.reshape(shape) for r in res]


ANY = pl.BlockSpec(memory_space=pl.ANY)


def _place():
    x, y, c = lax.axis_index("x"), lax.axis_index("y"), lax.axis_index("c")
    chips = [(1 - x, y), (x, 1 - y), (1 - x, 1 - y)]
    return x, y, c, chips


def _remote(src, dst, send_sem, recv_sem, dev):
    return pltpu.make_async_remote_copy(src_ref=src, dst_ref=dst, send_sem=send_sem, recv_sem=recv_sem,
                                        device_id=dev, device_id_type=MESH)


def _gather_phases(shapes, w_refs, o_refs, send_sems, recv_sems):
    x, y, c, chips = _place()
    me_k = 2 * x + y
    sibling = (x, y, 1 - c)

    def ici(i, r):
        half = shapes[i][0] // 2
        mine = pl.ds(c * half, half)
        cx, cy = chips[r]
        return _remote(w_refs[i].at[mine], o_refs[i].at[me_k, mine],
                       send_sems.at[6 * i + r], recv_sems.at[6 * i + r], (cx, cy, c))

    def d2d(i, r, core):
        half = shapes[i][0] // 2
        cx, cy = chips[r]
        rows = o_refs[i].at[2 * cx + cy, pl.ds(core * half, half)]
        return rows, _remote(rows, rows, send_sems.at[6 * i + 3 + r], recv_sems.at[6 * i + 3 + r], sibling)

    pairs = [(i, r) for i in range(len(shapes)) for r in range(3)]

    def send():
        for i, r in pairs:
            ici(i, r).start()

    def forward():
        for i, r in pairs:
            got, fwd = d2d(i, r, c)
            _remote(got, got, send_sems.at[6 * i + r], recv_sems.at[6 * i + r], sibling).wait_recv()
            fwd.start()

    def drain():
        for i, r in pairs:
            d2d(i, r, 1 - c)[1].wait_recv()
        for i, r in pairs:
            ici(i, r).wait_send()
            d2d(i, r, c)[1].wait_send()

    return send, forward, drain


class _Job:
    def __init__(self, ins, outs, n_sems, bind):
        self.ins, self.outs, self.n_sems, self.bind = list(ins), list(outs), n_sems, bind


def _jobs_sems(jobs):
    return [pltpu.SemaphoreType.DMA((jb.n_sems,)) for jb in jobs for _ in range(2)]


def _jobs_bind(jobs, in_refs, out_refs, sem_refs):
    bound, i, o = [], 0, 0
    for k, jb in enumerate(jobs):
        bound.append(jb.bind(in_refs[i:i + len(jb.ins)], out_refs[o:o + len(jb.outs)], sem_refs[2 * k], sem_refs[2 * k + 1]))
        i, o = i + len(jb.ins), o + len(jb.outs)

    def phase(p):
        def run():
            for b in bound:
                b[p]()
        return run

    return phase(0), phase(1), phase(2)


def _run_jobs(name, jobs):
    ins = [a for jb in jobs for a in jb.ins]
    outs = [o for jb in jobs for o in jb.outs]

    def body(*refs):
        start, middle, finish = _jobs_bind(jobs, refs[:len(ins)], refs[len(ins):len(ins) + len(outs)],
                                           refs[len(ins) + len(outs):])
        start()
        middle()
        finish()

    return pl.pallas_call(body, name=name, in_specs=[ANY] * len(ins), out_specs=[ANY] * len(outs), out_shape=outs,
                          scratch_shapes=_jobs_sems(jobs))(*ins)


def _gather_job(ws):
    shapes = [w.shape for w in ws]
    return _Job(ws, [jax.ShapeDtypeStruct((N_CHIPS,) + w.shape, w.dtype) for w in ws], 6 * len(ws),
                functools.partial(_gather_phases, shapes))


def _nothing():
    pass


def _pair_exchange_job(gs):
    def bind(g_refs, r_refs, send_sems, recv_sems):
        x, y, c, _ = _place()

        def copies():
            for i, g in enumerate(gs):
                half = g.shape[1] // 2
                yield _remote(g_refs[i].at[:, pl.ds((1 - c) * half, half)], r_refs[i],
                              send_sems.at[i], recv_sems.at[i], (x, y, 1 - c))

        def start():
            for cp in copies():
                cp.start()

        def finish():
            for cp in copies():
                cp.wait_recv()
            for cp in copies():
                cp.wait_send()

        return start, _nothing, finish

    return _Job(gs, [jax.ShapeDtypeStruct((g.shape[0], g.shape[1] // 2, g.shape[2]), g.dtype) for g in gs], len(gs), bind)


def _chip_exchange_job(pbs):
    def bind(pb_refs, rv_refs, send_sems, recv_sems):
        x, y, c, chips = _place()

        def copies():
            for i in range(len(pbs)):
                for r, (cx, cy) in enumerate(chips):
                    yield _remote(pb_refs[i].at[2 * cx + cy], rv_refs[3 * i + r],
                                  send_sems.at[3 * i + r], recv_sems.at[3 * i + r], (cx, cy, c))

        def start():
            for cp in copies():
                cp.start()

        def finish():
            for cp in copies():
                cp.wait_recv()
            for cp in copies():
                cp.wait_send()

        return start, _nothing, finish

    return _Job(pbs, [jax.ShapeDtypeStruct(pb.shape[1:], pb.dtype) for pb in pbs for _ in range(3)], 3 * len(pbs), bind)


def _tile_rows(rows, cols):
    return _pick(rows, max(16, ROWS_CALL_TILE_ELEMS // cols), 16)


def _place_own(name, own, stack, meta):
    R, Cn = own.shape
    tr = _tile_rows(R, Cn)

    def body(m_ref, own_ref, stack_ref, o_ref):
        o_ref[...] = own_ref[...]

    return pl.pallas_call(
        body, name=name, out_shape=jax.ShapeDtypeStruct(stack.shape, stack.dtype), input_output_aliases={2: 0},
        grid_spec=pltpu.PrefetchScalarGridSpec(
            num_scalar_prefetch=1, grid=(R // tr,),
            in_specs=[pl.BlockSpec((tr, Cn), lambda i, m: (i, 0)), ANY],
            out_specs=pl.BlockSpec((None, tr, Cn), lambda i, m: (m[1], i, 0))),
        compiler_params=_params("parallel"),
    )(meta, own, stack)


def _pair_sum(name, g, recv, meta):
    S, R, Cn = g.shape
    H = R // 2
    tr = _tile_rows(H, Cn)
    nh = H // tr

    def body(m_ref, g_ref, r_ref, p32_ref, p16_ref):
        v = g_ref[...] + r_ref[...]
        p32_ref[...] = v
        p16_ref[...] = v.astype(BF16)

    blk = pl.BlockSpec((None, tr, Cn), lambda s, i, m: (s, i, 0))
    return pl.pallas_call(
        body, name=name,
        out_shape=[jax.ShapeDtypeStruct((S, H, Cn), F32), jax.ShapeDtypeStruct((S, H, Cn), BF16)],
        grid_spec=pltpu.PrefetchScalarGridSpec(
            num_scalar_prefetch=1, grid=(S, nh),
            in_specs=[pl.BlockSpec((None, tr, Cn), lambda s, i, m: (s, m[0] * nh + i, 0)), blk],
            out_specs=[blk, blk]),
        compiler_params=_params("parallel", "parallel"),
    )(meta, g, recv)


def _chip_sum(name, p32, rvs, stack, layer, n_layers, meta):
    S, H, Cn = p32.shape
    tr = _tile_rows(H, Cn)
    nh = H // tr
    half = pl.BlockSpec((tr, Cn), lambda i, m: (i, 0))
    in_specs = [pl.BlockSpec((None, tr, Cn), lambda i, m: (m[1], i, 0)), half, half, half]
    args = [meta, p32, *rvs]
    aliases = {}
    if stack is not None:
        in_specs.append(ANY)
        args.append(stack)
        aliases = {len(args) - 1: 0}

    def body(m_ref, p_ref, r0_ref, r1_ref, r2_ref, *rest):
        rest[-1][...] = p_ref[...] + r0_ref[...].astype(F32) + r1_ref[...].astype(F32) + r2_ref[...].astype(F32)

    return pl.pallas_call(
        body, name=name, out_shape=jax.ShapeDtypeStruct((n_layers * 2 * H, Cn), F32), input_output_aliases=aliases,
        grid_spec=pltpu.PrefetchScalarGridSpec(
            num_scalar_prefetch=1, grid=(nh,), in_specs=in_specs,
            out_specs=pl.BlockSpec((tr, Cn), lambda i, m: (layer * 2 * nh + m[0] * nh + i, 0))),
        compiler_params=_params("parallel"),
    )(*args)


def _pair_gather(stacks, halves):
    ng = len(stacks)
    copies = [(g, l) for g in range(ng) for l in range(stacks[g].shape[0] // (2 * halves[g]))]
    n = len(copies)

    def body(*refs):
        o_refs = refs[ng:2 * ng]
        send_sems, recv_sems = refs[2 * ng:]
        x, y, c, _ = _place()
        sibling = (x, y, 1 - c)
        cps = []
        for k, (g, l) in enumerate(copies):
            H = halves[g]
            rows = o_refs[g].at[pl.ds(l * 2 * H + c * H, H)]
            cp = _remote(rows, rows, send_sems.at[k], recv_sems.at[k], sibling)
            cp.start()
            cps.append(cp)
        for k, (g, l) in enumerate(copies):
            H = halves[g]
            theirs = o_refs[g].at[pl.ds(l * 2 * H + (1 - c) * H, H)]
            _remote(theirs, theirs, send_sems.at[k], recv_sems.at[k], sibling).wait_recv()
        for cp in cps:
            cp.wait_send()

    return pl.pallas_call(
        body, name="grad_pair_gather", in_specs=[ANY] * ng, out_specs=[ANY] * ng,
        out_shape=[jax.ShapeDtypeStruct(s.shape, s.dtype) for s in stacks],
        input_output_aliases={g: g for g in range(ng)},
        scratch_shapes=[pltpu.SemaphoreType.DMA((n,)), pltpu.SemaphoreType.DMA((n,))],
    )(*stacks)


GRAD_BYTES = 704e6
ICI_SECONDS_PER_BYTE = 1.9e-3 / GRAD_BYTES
D2D_SECONDS_PER_BYTE = 0.53e-3 / GRAD_BYTES
MXU_FLOPS = 7.5e14
CARRIER_FILL = 0.8


class _GradReducer:
    def __init__(self, meta, layers):
        self.meta, self.layers = meta, dict(layers)
        self.fresh, self.summed = [], []
        self.stack = {n: None for n in layers}
        self.half = {}
        self.hurry = False

    def add(self, name, layer, g):
        self.fresh.append((name, layer, g))

    def take(self, seconds):
        budget, used = CARRIER_FILL * seconds, 0.0
        jobs, plan = [], []
        while self.summed:
            cost = 2 * 4 * self.summed[0][2].size * ICI_SECONDS_PER_BYTE
            if used + cost > budget and (jobs or not self.hurry):
                break
            used += cost
            item = self.summed.pop(0)
            jobs.append(_chip_exchange_job([item[3]]))
            plan.append(("chip", item))
        if self.fresh:
            fresh, self.fresh = self.fresh, []
            jobs.append(_pair_exchange_job([g for _, _, g in fresh]))
            plan.append(("pair", fresh))
        return jobs, functools.partial(self._done, plan)

    def _done(self, plan, outs):
        i = 0
        for kind, item in plan:
            if kind == "chip":
                name, layer, p32, _ = item
                self.stack[name] = _chip_sum(f"grad_chip_sum_{name}_{layer}", p32, outs[i:i + 3], self.stack[name],
                                             layer, self.layers[name], self.meta)
                i += 3
            else:
                for (name, layer, g), recv in zip(item, outs[i:i + len(item)]):
                    p32, p16 = _pair_sum(f"grad_pair_sum_{name}_{layer}", g, recv, self.meta)
                    self.half[name] = p32.shape[1]
                    self.summed.append((name, layer, p32, p16))
                i += len(item)

    def finish(self):
        k = 0
        while self.fresh or self.summed:
            jobs, done = self.take(float("inf"))
            done(_run_jobs(f"grad_exchange_tail_{k}", jobs))
            k += 1
        names = list(self.layers)
        return dict(zip(names, _pair_gather([self.stack[n] for n in names], [self.half[n] for n in names])))


def _small_gather(name, v):
    def body(v_ref, o_ref, send_sems, recv_sems, loc_sem):
        x, y, c, chips = _place()
        me, sibling = (x, y, c), (x, y, 1 - c)

        def slot(px, py, pc):
            return o_ref.at[4 * px + 2 * py + pc]

        def copy(k, block, to, src=None):
            return _remote(slot(*block) if src is None else src, slot(*block), send_sems.at[k], recv_sems.at[k], to)

        mine = pltpu.make_async_copy(v_ref, slot(*me), loc_sem)
        mine.start()
        first = [copy(0, me, sibling, src=v_ref)]
        first += [copy(1 + j, me, (*chip, c), src=v_ref) for j, chip in enumerate(chips)]
        for cp in first:
            cp.start()
        passed = [copy(4 + j, (*chip, c), sibling) for j, chip in enumerate(chips)]
        for j, chip in enumerate(chips):
            copy(1 + j, (*chip, c), me).wait_recv()
            passed[j].start()
        copy(0, sibling, me).wait_recv()
        for j, chip in enumerate(chips):
            copy(4 + j, (*chip, 1 - c), me).wait_recv()
        for cp in first + passed:
            cp.wait_send()
        mine.wait()

    return pl.pallas_call(
        body, name=name, in_specs=[ANY], out_specs=ANY,
        out_shape=jax.ShapeDtypeStruct((N_DEV,) + v.shape, v.dtype),
        scratch_shapes=[pltpu.SemaphoreType.DMA((7,)), pltpu.SemaphoreType.DMA((7,)), pltpu.SemaphoreType.DMA],
    )(v)


SUM_ROWS = 512


def _sum_devices(name, g):
    n, R, Cn = g.shape
    tr = _pick(R, SUM_ROWS, 8)

    def body(g_ref, o_ref):
        acc = g_ref[0]
        for d in range(1, n):
            acc = acc + g_ref[d]
        o_ref[...] = acc

    return pl.pallas_call(
        body, name=name, grid=(R // tr,), in_specs=[pl.BlockSpec((n, tr, Cn), lambda i: (0, i, 0))],
        out_specs=pl.BlockSpec((tr, Cn), lambda i: (i, 0)), out_shape=jax.ShapeDtypeStruct((R, Cn), g.dtype),
        compiler_params=_params("parallel"),
    )(g)


WEIGHTS = ["mix_norm", "conv_w_in", "conv_b_in", "conv_dw", "conv_dw_b", "conv_ln_g", "conv_ln_b", "conv_w_out",
           "conv_b_out", "ssm_lambda_re", "ssm_lambda_im", "ssm_log_dt", "ssm_b_re", "ssm_b_im", "ssm_c_re",
           "ssm_c_im", "ssm_d", "ssm_w_glu", "mlp_norm", "mlp_w_up", "mlp_w_down", "final_norm"]
LARGE = ["conv_w_in", "conv_w_out", "ssm_w_glu", "mlp_w_up", "mlp_w_down"]
SHARDED_SMALL = ["conv_dw", "ssm_d"]
REPLICATED = [n for n in WEIGHTS if n not in LARGE and n not in SHARDED_SMALL]
PACK_QUANTUM = 8 * 128


def _pack(parts, row_multiple=8):
    rows = []
    for p in parts:
        f = p.reshape(-1)
        pad = (-f.shape[0]) % PACK_QUANTUM
        if pad:
            f = jnp.pad(f, (0, pad))
        rows.append(f.reshape(-1, 128))
    total = sum(r.shape[0] for r in rows)
    if total % row_multiple:
        rows.append(jnp.zeros((row_multiple - total % row_multiple, 128), rows[0].dtype))
    return jnp.concatenate(rows, axis=0)


def _packed_rows(shape):
    return -(-math.prod(shape) // PACK_QUANTUM) * 8


def _unpack(buf, shapes):
    out, r = [], 0
    for s in shapes:
        rows = _packed_rows(s)
        out.append(buf[r:r + rows].reshape(-1)[:math.prod(s)].reshape(s))
        r += rows
    return out


def _block_diag(t, pattern):
    return jnp.einsum(pattern, t, jnp.eye(GROUPS_PER_BLOCK, dtype=t.dtype))


def _local_step(xs, tgt, p, wg, S, own=None, meta=None, reducer=None):
    T, D = xs.shape
    depth = p["mix_norm"].shape[0]
    width = p["conv_dw"].shape[1]
    G, P, C = D // SSM_C, SSM_P, SSM_C
    NG = G // GROUPS_PER_BLOCK
    lanes = GROUPS_PER_BLOCK * P
    F = (own if own is not None else wg)["mlp_w_up"][0].shape[-1] * S
    tm = _pick(T, 1024, 16)
    tmh = _pick(T, 512, 16)
    tre = _pick(T, 256, 16)
    trc = _pick(T, 128, CONV_HALO)
    tcs = _pick(T, 256, 8 * SCAN_SEQS)
    row = lambda v: v.reshape(1, -1)

    def missing(units):
        return [u for u in units if wg[u[0]][u[1]] is None]

    def store(units, stacks):
        for (n, l), st in zip(units, stacks):
            wg[n][l] = _place_own(f"place_own_{n}_{l}", own[n][l], st, meta)

    def with_fetch(fn, units, n_out):
        units = missing(units)
        res = fn(jobs=[_gather_job([own[n][l] for n, l in units])] if units else [])
        store(units, res[n_out:])
        return res[:n_out]

    def riding(fn, flops, n_out):
        if reducer is None:
            return fn()[:n_out]
        jobs, done = reducer.take(flops / MXU_FLOPS)
        res = fn(jobs=jobs)
        done(res[n_out:])
        return res[:n_out]

    def got(name, layer, g):
        gs[name][layer] = g
        if reducer is not None:
            reducer.add(name, layer, g)

    def nn_col(name, a, wname, j, offs, extras, outs, epi, tm_, fetch=()):
        w = wg[wname][j]
        K, Ns = a.shape[1], w.shape[2]
        tn, tk = _pick(Ns, MM_OUT_TILE, 128), _pick(K, MM_K_TILE, 128)
        maps = [_w_col(0, K // tk, Ns // tn, o // tn) for o in offs]
        return with_fetch(functools.partial(_mm_nn, name, a, w, maps, extras, outs, epi, tm=tm_, tn=tn, tk=tk),
                          fetch, len(outs))

    def nn_row(name, a, wname, j, extras, outs, epi, fetch=()):
        w = wg[wname][j]
        Ks, N = a.shape[1] // S, w.shape[2]
        tn, tk = _pick(N, MM_OUT_TILE, 128), _pick(Ks, MM_K_TILE, 128)
        return with_fetch(functools.partial(_mm_nn, name, a, w, [_w_row(0, Ks // tk)], extras, outs, epi,
                                            tm=tm, tn=tn, tk=tk), fetch, len(outs))

    def nt_col(name, a, wname, j, R, extras, outs, epi):
        w = wg[wname][j]
        Cs = w.shape[2]
        tr, tc = _pick(R, MM_OUT_TILE, 128), _pick(Cs, MM_K_TILE, 128)
        return riding(functools.partial(_mm_nt, name, a, w, _nt_col(0, R // tr, Cs // tc), extras, outs, epi,
                                        tm=tm, tr=tr, tc=tc), 2 * a.shape[0] * a.shape[1] * R, len(outs))

    def nt_row(name, a, wname, j, R, extras, outs, epi):
        w = wg[wname][j]
        Rs = R // S
        tr, tc = _pick(Rs, MM_OUT_TILE, 128), _pick(w.shape[2], MM_K_TILE, 128)
        return riding(functools.partial(_mm_nt, name, a, w, _nt_row(0, Rs // tr), extras, outs, epi,
                                        tm=tm, tr=tr, tc=tc), 2 * a.shape[0] * a.shape[1] * R, len(outs))

    def tn(name, a, b, sharding):
        R, N = a.shape[1], b.shape[1]
        if sharding == "col":
            tr, tc = _pick(R, MM_OUT_TILE, 128), _pick(N // S, MM_OUT_TILE, 128)
        else:
            tr, tc = _pick(R // S, MM_OUT_TILE, 128), _pick(N, MM_OUT_TILE, 128)
        return riding(functools.partial(_mm_tn, name, a, b, sharding, S, tr=tr, tc=tc, tt=_pick(T, MM_K_TILE, 128)),
                      2 * T * R * N, 1)[0]

    ssm = []
    for j in range(p["ssm_lambda_re"].shape[0]):
        rep = lambda t: jnp.repeat(t, C, axis=0)
        lam_r, lam_i = p["ssm_lambda_re"][j], p["ssm_lambda_im"][j]
        ldt = jnp.broadcast_to(p["ssm_log_dt"][j][:, None], (G, P))
        b_r = p["ssm_b_re"][j].transpose(0, 2, 1).reshape(G * C, P)
        b_i = p["ssm_b_im"][j].transpose(0, 2, 1).reshape(G * C, P)
        ar, ai, kr, ki, bbr, bbi = _ssm_prep(f"ssm_prep_{j}", rep(lam_r), rep(lam_i), rep(ldt), b_r, b_i)
        tabf, tabr = _ssm_powers(f"ssm_powers_{j}", ar[::C].reshape(NG, 1, lanes), ai[::C].reshape(NG, 1, lanes),
                                 tcs // SCAN_SEQS)
        bd = lambda t: _block_diag(t.reshape(NG, GROUPS_PER_BLOCK, C, P), "bgcp,gh->bgchp").reshape(NG, 128, lanes)
        bbd = jnp.concatenate([bd(bbr), bd(bbi)], axis=2).astype(BF16)
        cd = lambda t: _block_diag(t.reshape(NG, GROUPS_PER_BLOCK, C, P), "bgcp,gh->bhpgc").reshape(NG, lanes, 128)
        ccd = jnp.concatenate([cd(p["ssm_c_re"][j]), -cd(p["ssm_c_im"][j])], axis=1).astype(BF16)
        ssm.append(dict(lam_r=lam_r, lam_i=lam_i, ldt=ldt, b_r=b_r, b_i=b_i, kr=kr, ki=ki, tabf=tabf, tabr=tabr,
                        bbd=bbd, ccd=ccd, dskip=row(p["ssm_d"][j])))

    dwp = [jnp.pad(p["conv_dw"][j], ((0, CONV_HALO - width), (0, 0))) for j in range(p["conv_dw"].shape[0])]

    def first_weight(i):
        if i >= depth:
            return []
        return [("conv_w_in" if i % 2 == 0 else "ssm_w_glu", i // 2)]

    def next_up(i):
        return [("mlp_w_up", i)] if i < depth else []

    units = missing(first_weight(0))
    if units:
        store(units, _run_jobs("weight_gather", [_gather_job([own[n][l] for n, l in units])]))

    saved = []
    x = xs
    for i in range(depth):
        j = i // 2
        s = dict(x_in=x)
        h = _rms_fwd(f"mix_norm_fwd_{i}", x, row(p["mix_norm"][i]), tr=tre)
        s["h"] = h
        if i % 2 == 0:
            def epi_in(accs, ex):
                a_, g_ = accs[0] + ex[0], accs[1] + ex[1]
                return [a_, g_, a_ * _sigmoid(g_)]
            b_in = row(p["conv_b_in"][j])
            a_, g_, v = nn_col(f"conv_in_{j}", h, "conv_w_in", j, [0, D], [(b_in, "n", 0), (b_in, "n", D)],
                               [((T, D), BF16), ((T, D), BF16), ((T, D), F32)], epi_in, tmh,
                               fetch=[("conv_w_out", j)] + next_up(i))
            sl = _conv_fwd(f"conv_fwd_{j}", v, dwp[j], row(p["conv_dw_b"][j]), row(p["conv_ln_g"][j]),
                           row(p["conv_ln_b"][j]), tr=trc, width=width)
            x = nn_row(f"conv_out_{j}", sl, "conv_w_out", j, [(row(p["conv_b_out"][j]), "n", 0), (x, "mn", 0)],
                       [((T, D), F32)], lambda accs, ex: [accs[0] + ex[0] + ex[1]], fetch=first_weight(i + 1))[0]
            s.update(a=a_, g=g_, v=v, s=sl)
        else:
            q = ssm[j]
            y, states, cins = _ssm_fwd(f"ssm_fwd_{j}", h, q["bbd"], q["ccd"], q["tabf"], q["dskip"], tc=tcs)
            def epi_glu(accs, ex):
                return [accs[0], accs[1], accs[0] * _sigmoid(accs[1]) + ex[0]]
            val, gate, x = nn_col(f"ssm_glu_{j}", y, "ssm_w_glu", j, [0, D], [(x, "mn", 0)],
                                  [((T, D), BF16), ((T, D), BF16), ((T, D), F32)], epi_glu, tmh,
                                  fetch=next_up(i) + first_weight(i + 1))
            s.update(y=y, states=states, cins=cins, val=val, gate=gate)
        s["x_mid"] = x
        h2 = _rms_fwd(f"mlp_norm_fwd_{i}", x, row(p["mlp_norm"][i]), tr=tre)
        def epi_up(accs, ex):
            r_ = jnp.maximum(accs[0], 0.0)
            return [r_, r_ * r_]
        r, rsq = nn_col(f"mlp_up_{i}", h2, "mlp_w_up", i, [0], [], [((T, F), BF16), ((T, F), BF16)], epi_up, tm,
                        fetch=[("mlp_w_down", i)])
        x = nn_row(f"mlp_down_{i}", rsq, "mlp_w_down", i, [(x, "mn", 0)], [((T, D), F32)],
                   lambda accs, ex: [accs[0] + ex[0]], fetch=first_weight(i + 1) + next_up(i + 1))[0]
        s.update(h2=h2, r=r, rsq=rsq)
        saved.append(s)

    dx, dxb, loss8, dgf = _final_loss("final_loss", x, row(p["final_norm"]), tgt, tr=tre)

    n_conv, n_ssm = p["conv_dw"].shape[0], p["ssm_d"].shape[0]
    gs = {n: [None] * p[n].shape[0] for n in WEIGHTS if n not in LARGE and n != "final_norm"}
    gs.update(conv_w_in=[None] * n_conv, conv_w_out=[None] * n_conv, ssm_w_glu=[None] * n_ssm,
              mlp_w_up=[None] * depth, mlp_w_down=[None] * depth)
    gs["final_norm"] = dgf.reshape(-1)
    for i in reversed(range(depth)):
        j = i // 2
        s = saved[i]
        if reducer is not None:
            reducer.hurry = i == 0
        dz = nt_row(f"mlp_down_dx_{i}", dxb, "mlp_w_down", i, F, [s["r"]], [((T, F), BF16)],
                    lambda acc, ex: [acc * (2.0 * ex[0].astype(F32))])[0]
        got("mlp_w_down", i, tn(f"mlp_down_dw_{i}", s["rsq"], dxb, "row"))
        got("mlp_w_up", i, tn(f"mlp_up_dw_{i}", s["h2"], dz, "col"))
        dh2 = nt_col(f"mlp_up_dx_{i}", dz, "mlp_w_up", i, D, [], [((T, D), F32)], lambda acc, ex: [acc])[0]
        dx, dxb, dg, cs = _rms_bwd(f"mlp_norm_bwd_{i}", s["x_mid"], row(p["mlp_norm"][i]), dh2, dx, tr=tre)
        gs["mlp_norm"][i] = dg.reshape(-1)
        if i % 2 == 0:
            gs["conv_b_out"][j] = cs.reshape(-1)
            got("conv_w_out", j, tn(f"conv_out_dw_{j}", s["s"], dxb, "row"))
            dsl = nt_row(f"conv_out_dx_{j}", dxb, "conv_w_out", j, D, [], [((T, D), F32)],
                         lambda acc, ex: [acc])[0]
            dv2, dlg, dlb, ddwb = _conv_bwd_a(f"conv_bwd_a_{j}", s["v"], dsl, dwp[j], row(p["conv_dw_b"][j]),
                                              row(p["conv_ln_g"][j]), row(p["conv_ln_b"][j]), tr=trc, width=width)
            du, ddw, dbin = _conv_bwd_b(f"conv_bwd_b_{j}", dv2, s["v"], s["a"], s["g"], dwp[j], tr=trc, width=width)
            gs["conv_ln_g"][j], gs["conv_ln_b"][j] = dlg.reshape(-1), dlb.reshape(-1)
            gs["conv_dw_b"][j], gs["conv_dw"][j], gs["conv_b_in"][j] = ddwb.reshape(-1), ddw[:width], dbin.reshape(-1)
            got("conv_w_in", j, tn(f"conv_in_dw_{j}", s["h"], du, "col"))
            dh = nt_col(f"conv_in_dx_{j}", du, "conv_w_in", j, D, [], [((T, D), F32)], lambda acc, ex: [acc])[0]
        else:
            q = ssm[j]
            dz2 = _glu_bwd(f"ssm_glu_bwd_{j}", dx, s["val"], s["gate"], tr=tre)
            got("ssm_w_glu", j, tn(f"ssm_glu_dw_{j}", s["y"], dz2, "col"))
            dy = nt_col(f"ssm_glu_dx_{j}", dz2, "ssm_w_glu", j, D, [], [((T, D), F32)], lambda acc, ex: [acc])[0]
            dh, dbbd, dccd, da, dd = _ssm_bwd(f"ssm_bwd_{j}", s["h"], dy, s["states"], s["cins"], q["bbd"], q["ccd"],
                                              q["tabr"], q["dskip"], tc=tcs)
            gbb = _block_diag(dbbd.reshape(NG, GROUPS_PER_BLOCK, C, 2, GROUPS_PER_BLOCK, P), "bgcrhp,gh->rbgcp")
            dbr, dbi, tkr, tki = _ssm_prep_bwd_b(f"ssm_prep_bwd_b_{j}", q["kr"], q["ki"], q["b_r"], q["b_i"],
                                                 gbb[0].reshape(G * C, P), gbb[1].reshape(G * C, P))
            unrow = lambda t: t.reshape(G, C, P).transpose(0, 2, 1)
            gs["ssm_b_re"][j], gs["ssm_b_im"][j] = unrow(dbr), unrow(dbi)
            per_c = lambda t: t.reshape(G, C, P).transpose(1, 0, 2)
            dlr, dli, dldt = _ssm_prep_bwd_a(f"ssm_prep_bwd_a_{j}", q["lam_r"], q["lam_i"], q["ldt"],
                                             da[:, 0, :lanes].reshape(G, P), da[:, 0, lanes:].reshape(G, P),
                                             per_c(tkr), per_c(tki))
            gs["ssm_lambda_re"][j], gs["ssm_lambda_im"][j], gs["ssm_log_dt"][j] = dlr, dli, dldt[:, 0]
            gcc = _block_diag(dccd.reshape(NG, 2, GROUPS_PER_BLOCK, P, GROUPS_PER_BLOCK, C), "brhpgc,gh->rbgcp")
            gs["ssm_c_re"][j], gs["ssm_c_im"][j] = gcc[0].reshape(G, C, P), -gcc[1].reshape(G, C, P)
            gs["ssm_d"][j] = dd.reshape(-1)
        dx, dxb, dg, _ = _rms_bwd(f"mix_norm_bwd_{i}", s["x_in"], row(p["mix_norm"][i]), dh, dx, tr=tre)
        gs["mix_norm"][i] = dg.reshape(-1)

    small = {n: (gs[n] if n == "final_norm" else jnp.stack(gs[n])) for n in WEIGHTS if n not in LARGE}
    large = {n: gs[n] for n in LARGE}
    return loss8[0, 0], dx, small, large


def kernel(x, mix_norm, conv_w_in, conv_b_in, conv_dw, conv_dw_b, conv_ln_g, conv_ln_b, conv_w_out, conv_b_out, ssm_lambda_re, ssm_lambda_im, ssm_log_dt, ssm_b_re, ssm_b_im, ssm_c_re, ssm_c_im, ssm_d, ssm_w_glu, mlp_norm, mlp_w_up, mlp_w_down, final_norm, loss_target, m_mix_norm, m_conv_w_in, m_conv_b_in, m_conv_dw, m_conv_dw_b, m_conv_ln_g, m_conv_ln_b, m_conv_w_out, m_conv_b_out, m_ssm_lambda_re, m_ssm_lambda_im, m_ssm_log_dt, m_ssm_b_re, m_ssm_b_im, m_ssm_c_re, m_ssm_c_im, m_ssm_d, m_ssm_w_glu, m_mlp_norm, m_mlp_w_up, m_mlp_w_down, m_final_norm, v_mix_norm, v_conv_w_in, v_conv_b_in, v_conv_dw, v_conv_dw_b, v_conv_ln_g, v_conv_ln_b, v_conv_w_out, v_conv_b_out, v_ssm_lambda_re, v_ssm_lambda_im, v_ssm_log_dt, v_ssm_b_re, v_ssm_b_im, v_ssm_c_re, v_ssm_c_im, v_ssm_d, v_ssm_w_glu, v_mlp_norm, v_mlp_w_up, v_mlp_w_down, v_final_norm):
    a = dict(locals())
    S = N_CHIPS
    w = {n: a[n] for n in WEIGHTS}
    k_chip = 2 * lax.axis_index("x") + lax.axis_index("y")

    meta = jnp.stack([lax.axis_index("c"), k_chip]).astype(jnp.int32)
    own = {n: [w[n][l].astype(BF16) for l in range(w[n].shape[0])] for n in LARGE}
    wg = {n: [None] * len(own[n]) for n in LARGE}
    sh_shapes = [w[n].shape for n in SHARDED_SMALL]
    sh_all = _small_gather("small_weight_gather", _pack([w[n] for n in SHARDED_SMALL]))
    per_chip = [_unpack(sh_all[2 * k], sh_shapes) for k in range(S)]
    p = {n: w[n] for n in REPLICATED}
    for idx, n in enumerate(SHARDED_SMALL):
        p[n] = jnp.concatenate([per_chip[k][idx] for k in range(S)], axis=-1)

    reducer = _GradReducer(meta, {n: len(own[n]) for n in LARGE})
    loss_local, dx, small, _ = _local_step(x[0], loss_target[0], p, wg, S, own, meta, reducer)
    loss = lax.psum(loss_local, ("x", "y", "c"))
    grads = {n: st.reshape(w[n].shape) for n, st in reducer.finish().items()}

    small_names = REPLICATED + SHARDED_SMALL
    g_all = _small_gather("small_grad_gather", _pack([small[n] for n in small_names], SUM_ROWS))
    g_sum = _sum_devices("small_grad_sum", g_all)
    g_parts = dict(zip(small_names, _unpack(g_sum, [small[n].shape for n in small_names])))
    for n in REPLICATED:
        grads[n] = g_parts[n]
    for n in SHARDED_SMALL:
        cols = w[n].shape[-1]
        grads[n] = lax.dynamic_slice_in_dim(g_parts[n], k_chip * cols, cols, axis=g_parts[n].ndim - 1)

    delta, new_m, new_v = {}, {}, {}
    for n in LARGE + SHARDED_SMALL:
        delta[n], new_m[n], new_v[n] = _adamw(f"adamw_{n}", w[n], grads[n], a["m_" + n], a["v_" + n])
    rep_shapes = [w[n].shape for n in REPLICATED]
    rep_rows = sum(_packed_rows(s) for s in rep_shapes)
    res = _adamw("adamw_replicated", _pack([w[n] for n in REPLICATED]), g_sum[:rep_rows],
                 _pack([a["m_" + n] for n in REPLICATED]), _pack([a["v_" + n] for n in REPLICATED]))
    for dst, buf in zip((delta, new_m, new_v), res):
        dst.update(zip(REPLICATED, _unpack(buf, rep_shapes)))

    return (loss, dx[None], *[grads[n] for n in WEIGHTS], *[delta[n] for n in WEIGHTS],
            *[new_m[n] for n in WEIGHTS], *[new_v[n] for n in WEIGHTS])
```
